```python
import math
import jax, jax.numpy as jnp
from jax import lax
import numpy as np

D_MODEL = 2048
BATCH = 16
SEQ = 256
DEPTH = 2
DEC_BATCH = 2
DEC_SEQ = 2048
PAST_LEN = 512

GRID_W = 64
N_HEADS = 8
HEAD_DIM = 128
V_DIM = 2 * HEAD_DIM
QK_WIDTH = N_HEADS * 2 * HEAD_DIM
V_WIDTH = N_HEADS * V_DIM
POOL_WINDOWS = (2, 4, 8, 16)
POOL_GROUP = D_MODEL // 8
POOL_WIDTH = len(POOL_WINDOWS) * POOL_GROUP
CONV_WIDTH = D_MODEL // 2
N_BRANCH = 3
IN_COLS = POOL_WIDTH + 2 * QK_WIDTH + V_WIDTH + 3 * CONV_WIDTH + N_BRANCH * D_MODEL
D_FF = ((8 * D_MODEL // 3 + 127) // 128) * 128
ROPE_FREQS = HEAD_DIM // 4
ROPE_BASE = 10000.0
EPS = 1e-6
Q_BLOCK = 128
N_MOD = 6

kernel_name = 'hybrid_diffusion_prefix_trunk_step'


def rms_norm(x, g):
    x32 = x.astype(jnp.float32)
    y = x32 * lax.rsqrt(jnp.mean(x32 * x32, axis=-1, keepdims=True) + EPS)
    return (y * g.astype(jnp.float32)).astype(x.dtype)


def depthwise_conv3(x, w, b):
    L = x.shape[1]
    xp = jnp.pad(x, ((0, 0), (1, 1), (0, 0)))
    return xp[:, :L] * w[0] + xp[:, 1:L + 1] * w[1] + xp[:, 2:] * w[2] + b


def pool_mixer(u, w_groups, scale):
    B, L, _ = u.shape
    cs = jnp.concatenate([jnp.zeros((B, 1, POOL_WIDTH), jnp.float32),
                          jnp.cumsum(u.astype(jnp.float32), axis=1)], axis=1)
    t = jnp.arange(L)
    outs = []
    for gi, w in enumerate(POOL_WINDOWS):
        half = w // 2
        lo = jnp.clip(t - half, 0, L)
        hi = jnp.clip(t + w - half, 0, L)
        cg = cs[..., gi * POOL_GROUP:(gi + 1) * POOL_GROUP]
        s = jnp.take(cg, hi, axis=1) - jnp.take(cg, lo, axis=1)
        outs.append(s / (hi - lo).astype(jnp.float32)[:, None])
    mean = jnp.concatenate(outs, axis=-1).astype(u.dtype)
    d = (mean - u).reshape(B, L, len(POOL_WINDOWS), POOL_GROUP)
    y = jnp.einsum('blgc,gcd->blgd', d, w_groups).reshape(B, L, POOL_WIDTH)
    return y * scale


def axial_rope_tables(L):
    rows = L // GRID_W
    row = jnp.repeat(jnp.arange(rows), GRID_W)
    col = jnp.tile(jnp.arange(GRID_W), rows)
    inv = ROPE_BASE ** (-jnp.arange(ROPE_FREQS, dtype=jnp.float32) / ROPE_FREQS)
    ang = jnp.stack([row, col], axis=-1).astype(jnp.float32)[:, :, None] * inv
    return jnp.cos(ang), jnp.sin(ang)


def apply_axial_rope(x, cos, sin):
    xs = x.reshape(x.shape[:-1] + (2, 2, ROPE_FREQS))
    x1, x2 = xs[..., 0, :], xs[..., 1, :]
    c = cos[:, None, None].astype(x.dtype)
    s = sin[:, None, None].astype(x.dtype)
    return jnp.stack([x1 * c - x2 * s, x1 * s + x2 * c], axis=-2).reshape(x.shape)


def diff_attention(q, k, v, lam):
    B, Lq = q.shape[:2]
    nb = Lq // Q_BLOCK
    qb = jnp.moveaxis(q.reshape((B, nb, Q_BLOCK) + q.shape[2:]), 1, 0)
    scale = HEAD_DIM ** -0.5

    def block(qi):
        s = jnp.einsum('bqhmd,bkhmd->bhmqk', qi, k).astype(jnp.float32) * scale
        p = jax.nn.softmax(s, axis=-1)
        pd = (p[:, :, 0] - lam * p[:, :, 1]).astype(v.dtype)
        return jnp.einsum('bhqk,bkhv->bqhv', pd, v)

    o = lax.map(block, qb)
    return jnp.moveaxis(o, 0, 1).reshape(B, Lq, N_HEADS, V_DIM)


def token_mixer(h, lp, lam_init, ctx_kv, rope):
    B, L, _ = h.shape
    z = h @ lp['w_in']
    sizes = [POOL_WIDTH, QK_WIDTH, QK_WIDTH, V_WIDTH, CONV_WIDTH, CONV_WIDTH, CONV_WIDTH, N_BRANCH * D_MODEL]
    cuts = np.cumsum(sizes)[:-1].tolist()
    u_pool, q, k, v, cb, cc, cx, g = jnp.split(z, cuts, axis=-1)
    y_pool = pool_mixer(u_pool, lp['pool_w'], lp['pool_scale'])
    q = q.reshape(B, L, N_HEADS, 2, HEAD_DIM)
    k = k.reshape(B, L, N_HEADS, 2, HEAD_DIM)
    v = v.reshape(B, L, N_HEADS, V_DIM)
    if rope is not None:
        q = apply_axial_rope(q, rope[0], rope[1])
        k = apply_axial_rope(k, rope[0], rope[1])
    if ctx_kv is None:
        k_all, v_all = k, v
    else:
        k_all = jnp.concatenate([ctx_kv[0].astype(k.dtype), k], axis=1)
        v_all = jnp.concatenate([ctx_kv[1].astype(v.dtype), v], axis=1)
    f32 = jnp.float32
    lam = (jnp.exp(jnp.sum(lp['lq1'].astype(f32) * lp['lk1'].astype(f32)))
           - jnp.exp(jnp.sum(lp['lq2'].astype(f32) * lp['lk2'].astype(f32))) + lam_init)
    o = diff_attention(q, k_all, v_all, lam)
    o = rms_norm(o, lp['g_subln']) * (1.0 - lam_init)
    y_attn = o.reshape(B, L, V_WIDTH)
    y_conv = cb * depthwise_conv3(cc * cx, lp['conv_w'], lp['conv_b'])
    gates = jax.nn.sigmoid(g.astype(f32)).astype(h.dtype).reshape(B, L, N_BRANCH, D_MODEL)
    merged = (gates[..., 0, :] * (y_pool @ lp['w_br_pool'])
              + gates[..., 1, :] * (y_attn @ lp['w_br_attn'])
              + gates[..., 2, :] * (y_conv @ lp['w_br_conv']))
    return merged @ lp['w_out'], k, v


def conv_ffn(h, lp):
    u = depthwise_conv3(h @ lp['ffn_w_up'], lp['ffn_conv_w'], lp['ffn_conv_b'])
    gt, val = jnp.split(u, 2, axis=-1)
    return (jax.nn.gelu(gt, approximate=True) * val) @ lp['ffn_w_down']


def trunk_layer(x, mod, lp, lam_init, ctx_kv, rope):
    sh1, sc1, gt1, sh2, sc2, gt2 = [m[:, None, :] for m in jnp.split(mod, N_MOD, axis=-1)]
    h = rms_norm(x, lp['g_pre_mix']) * (1 + sc1) + sh1
    m, k, v = token_mixer(h, lp, lam_init, ctx_kv, rope)
    x = x + gt1 * rms_norm(m, lp['g_post_mix'])
    h = rms_norm(x, lp['g_pre_ffn']) * (1 + sc2) + sh2
    x = x + gt2 * rms_norm(conv_ffn(h, lp), lp['g_post_ffn'])
    return x, k, v


def setup_inputs(seed: int = 0) -> dict:
    key = jax.random.key(seed)
    ks = jax.random.split(key, 32)
    n = lambda i, shape: jax.random.normal(ks[i], shape, jnp.float32)
    D = D_MODEL
    return {
        'x_prompt': n(0, (BATCH, SEQ, D)),
        'x_sample': n(1, (DEC_BATCH, DEC_SEQ, D)),
        'cache_k': n(2, (DEC_BATCH, DEPTH, PAST_LEN, N_HEADS, 2, HEAD_DIM)),
        'cache_v': n(3, (DEC_BATCH, DEPTH, PAST_LEN, N_HEADS, V_DIM)),
        'c': n(4, (DEC_BATCH, D)),
        'c_ctx': n(5, (D,)),
        'w_mod': n(6, (DEPTH, D, N_MOD * D)) * (0.5 * D ** -0.5),
        'b_mod': n(7, (DEPTH, N_MOD * D)) * 0.01,
        'g_pre_mix': 1.0 + 0.05 * n(8, (DEPTH, D)),
        'g_post_mix': 1.0 + 0.05 * n(9, (DEPTH, D)),
        'g_pre_ffn': 1.0 + 0.05 * n(10, (DEPTH, D)),
        'g_post_ffn': 1.0 + 0.05 * n(11, (DEPTH, D)),
        'w_in': n(12, (DEPTH, D, IN_COLS)) * D ** -0.5,
        'pool_w': n(13, (DEPTH, len(POOL_WINDOWS), POOL_GROUP, POOL_GROUP)) * POOL_GROUP ** -0.5,
        'pool_scale': 1.0 + 0.05 * n(14, (DEPTH, POOL_WIDTH)),
        'lambda_q1': 0.1 * n(15, (DEPTH, HEAD_DIM)),
        'lambda_k1': 0.1 * n(16, (DEPTH, HEAD_DIM)),
        'lambda_q2': 0.1 * n(17, (DEPTH, HEAD_DIM)),
        'lambda_k2': 0.1 * n(18, (DEPTH, HEAD_DIM)),
        'g_subln': 1.0 + 0.05 * n(19, (DEPTH, V_DIM)),
        'conv_w': n(20, (DEPTH, 3, CONV_WIDTH)) * 3 ** -0.5,
        'conv_b': 0.01 * n(21, (DEPTH, CONV_WIDTH)),
        'w_br_pool': n(22, (DEPTH, POOL_WIDTH, D)) * POOL_WIDTH ** -0.5,
        'w_br_attn': n(23, (DEPTH, V_WIDTH, D)) * V_WIDTH ** -0.5,
        'w_br_conv': n(24, (DEPTH, CONV_WIDTH, D)) * CONV_WIDTH ** -0.5,
        'w_out': n(25, (DEPTH, D, D)) * D ** -0.5,
        'ffn_w_up': n(26, (DEPTH, D, 2 * D_FF)) * D ** -0.5,
        'ffn_conv_w': n(27, (DEPTH, 3, 2 * D_FF)) * 3 ** -0.5,
        'ffn_conv_b': 0.01 * n(28, (DEPTH, 2 * D_FF)),
        'ffn_w_down': n(29, (DEPTH, D_FF, D)) * D_FF ** -0.5,
    }


def reference(x_prompt, x_sample, cache_k, cache_v, c, c_ctx, w_mod, b_mod, g_pre_mix, g_post_mix,
              g_pre_ffn, g_post_ffn, w_in, pool_w, pool_scale, lambda_q1, lambda_k1, lambda_q2, lambda_k2,
              g_subln, conv_w, conv_b, w_br_pool, w_br_attn, w_br_conv, w_out, ffn_w_up, ffn_conv_w,
              ffn_conv_b, ffn_w_down):
    rope = axial_rope_tables(x_sample.shape[1])
    xc = x_prompt
    xl = x_sample
    new_k = []
    new_v = []
    for l in range(DEPTH):
        lp = dict(w_in=w_in[l], pool_w=pool_w[l], pool_scale=pool_scale[l],
                  lq1=lambda_q1[l], lk1=lambda_k1[l], lq2=lambda_q2[l], lk2=lambda_k2[l],
                  g_subln=g_subln[l], conv_w=conv_w[l], conv_b=conv_b[l],
                  w_br_pool=w_br_pool[l], w_br_attn=w_br_attn[l], w_br_conv=w_br_conv[l], w_out=w_out[l],
                  ffn_w_up=ffn_w_up[l], ffn_conv_w=ffn_conv_w[l], ffn_conv_b=ffn_conv_b[l],
                  ffn_w_down=ffn_w_down[l], g_pre_mix=g_pre_mix[l], g_post_mix=g_post_mix[l],
                  g_pre_ffn=g_pre_ffn[l], g_post_ffn=g_post_ffn[l])
        lam_init = 0.8 - 0.6 * math.exp(-0.3 * l)
        mod_ctx = jax.nn.silu(c_ctx)[None, :] @ w_mod[l] + b_mod[l]
        mod_lat = jax.nn.silu(c) @ w_mod[l] + b_mod[l]
        xc, kc, vc = trunk_layer(xc, mod_ctx, lp, lam_init, None, None)
        new_k.append(kc)
        new_v.append(vc)
        xl, _, _ = trunk_layer(xl, mod_lat, lp, lam_init, (cache_k[:, l], cache_v[:, l]), rope)
    return (xc, xl, jnp.stack(new_k, axis=1), jnp.stack(new_v, axis=1))
```

```python
import functools
import math

import jax
import jax.numpy as jnp
from jax import lax
from jax.experimental import pallas as pl
from jax.experimental.pallas import tpu as pltpu

F32 = jnp.float32
BF16 = jnp.bfloat16

GRID_W = 64
N_HEADS = 8
HEAD_DIM = 128
V_DIM = 2 * HEAD_DIM
POOL_WINDOWS = (2, 4, 8, 16)
ROPE_FREQS = HEAD_DIM // 4
ROPE_BASE = 10000.0
EPS = 1e-6
N_MOD = 6
N_BRANCH = 3

LANES = 128
BF16_SUBLANES = 16
VMEM_LIMIT_BYTES = 56 * 1024 * 1024
MOD_ROWS = 8

TM_IN = 1024
TN_IN = 512
TM_MIX = 512
POOL_SUB = 256
TQ_LAT = 256
TF_FFN = 512
TN_MOD = 1024


def _cparams(n_axes):
    return pltpu.CompilerParams(dimension_semantics=("arbitrary",) * n_axes,
                                vmem_limit_bytes=VMEM_LIMIT_BYTES)


def _rms(x):
    return x * lax.rsqrt(jnp.mean(x * x, axis=-1, keepdims=True) + EPS)


def _mod_kernel(c_ref, w_ref, b_ref, o_ref):
    c = c_ref[...]
    s = (c * jax.nn.sigmoid(c)).astype(BF16)
    o_ref[...] = jnp.dot(s, w_ref[...].astype(BF16), preferred_element_type=F32) + b_ref[...]


def _modulation(cvec, w_mod, b_mod):
    depth, d, n = w_mod.shape
    return pl.pallas_call(
        _mod_kernel,
        grid=(depth, n // TN_MOD),
        in_specs=[
            pl.BlockSpec((MOD_ROWS, d), lambda l, j: (0, 0)),
            pl.BlockSpec((None, d, TN_MOD), lambda l, j: (l, 0, j)),
            pl.BlockSpec((None, 1, TN_MOD), lambda l, j: (l, 0, j)),
        ],
        out_specs=pl.BlockSpec((None, MOD_ROWS, TN_MOD), lambda l, j: (l, 0, j)),
        out_shape=jax.ShapeDtypeStruct((depth, MOD_ROWS, n), F32),
        compiler_params=_cparams(2),
        name="modulation",
    )(cvec, w_mod, b_mod.reshape(depth, 1, n))


def _rope_tile(acc, c, s):
    lane = lax.broadcasted_iota(jnp.int32, c.shape, 1)
    upper = (lane & ROPE_FREQS) != 0
    outs = []
    for g in range(acc.shape[1] // LANES):
        xg = acc[:, g * LANES:(g + 1) * LANES]
        partner = jnp.where(upper, pltpu.roll(xg, ROPE_FREQS, axis=1),
                            pltpu.roll(xg, LANES - ROPE_FREQS, axis=1))
        outs.append(xg * c + partner * s)
    return jnp.concatenate(outs, axis=1)


def _in_proj_kernel(*refs, rope, kv_out, qk_tiles, k_tiles, v_tiles, norm_rows):
    x_ref, sc_ref, sh_ref, g_ref, w_ref = refs[:5]
    pos = 5
    if rope:
        cos_ref, sin_ref = refs[pos:pos + 2]
        pos += 2
    z_ref = refs[pos]
    pos += 1
    if kv_out:
        k_ref, v_ref = refs[pos:pos + 2]
        pos += 2
    h_ref = refs[pos]
    j = pl.program_id(1)

    @pl.when(j == 0)
    def _():
        scale = 1.0 + sc_ref[0]
        shift = sh_ref[0]
        gain = g_ref[...]

        def body(r, carry):
            rows = pl.ds(pl.multiple_of(r * norm_rows, norm_rows), norm_rows)
            h = _rms(x_ref[rows, :]) * gain * scale + shift
            h_ref[rows, :] = h.astype(BF16)
            return carry

        lax.fori_loop(0, x_ref.shape[0] // norm_rows, body, 0)

    acc = jnp.dot(h_ref[...], w_ref[...], preferred_element_type=F32)

    if rope:
        is_qk = j < qk_tiles

        @pl.when(is_qk)
        def _():
            z_ref[...] = _rope_tile(acc, cos_ref[...], sin_ref[...]).astype(BF16)

        @pl.when(jnp.logical_not(is_qk))
        def _():
            z_ref[...] = acc.astype(BF16)
    else:
        z_ref[...] = acc.astype(BF16)

    if kv_out:
        @pl.when((j >= k_tiles[0]) & (j < k_tiles[1]))
        def _():
            k_ref[...] = acc

        @pl.when((j >= v_tiles[0]) & (j < v_tiles[1]))
        def _():
            v_ref[...] = acc


def _in_proj(x, mod3, g_pre, w_in, layer, mod_row, *, qk_w, rope_tabs=None, kv_out=False):
    t, d = x.shape
    n = w_in.shape[2]
    tm, tn = TM_IN, TN_IN
    nq = qk_w // tn
    rope = rope_tabs is not None

    def mod_idx(m):
        return lambda i, j: (mod_row(i * tm) * N_MOD + m, 0, 0)

    in_specs = [
        pl.BlockSpec((tm, d), lambda i, j: (i, 0)),
        pl.BlockSpec((1, 1, d), mod_idx(1)),
        pl.BlockSpec((1, 1, d), mod_idx(0)),
        pl.BlockSpec((1, d), lambda i, j: (0, 0)),
        pl.BlockSpec((None, d, tn), lambda i, j: (layer, 0, j)),
    ]
    args = [x, mod3, mod3, g_pre, w_in]
    if rope:
        tab_blocks = rope_tabs[0].shape[0] // tm
        tab_spec = pl.BlockSpec((tm, LANES), lambda i, j: (i % tab_blocks, 0))
        in_specs += [tab_spec, tab_spec]
        args += list(rope_tabs)
    out_specs = [pl.BlockSpec((tm, tn), lambda i, j: (i, j))]
    out_shape = [jax.ShapeDtypeStruct((t, n), BF16)]
    if kv_out:
        out_specs += [
            pl.BlockSpec((tm, tn), lambda i, j: (i, jnp.clip(j - nq, 0, nq - 1))),
            pl.BlockSpec((tm, tn), lambda i, j: (i, jnp.clip(j - 2 * nq, 0, nq - 1))),
        ]
        out_shape += [jax.ShapeDtypeStruct((t, qk_w), F32)] * 2
    kern = functools.partial(_in_proj_kernel, rope=rope, kv_out=kv_out, qk_tiles=2 * nq,
                             k_tiles=(nq, 2 * nq), v_tiles=(2 * nq, 3 * nq), norm_rows=256)
    return pl.pallas_call(
        kern,
        grid=(t // tm, n // tn),
        in_specs=in_specs,
        out_specs=out_specs,
        out_shape=out_shape,
        scratch_shapes=[pltpu.VMEM((tm, d), BF16)],
        compiler_params=_cparams(2),
        name="in_proj_rope" if rope else "in_proj_kv",
    )(*args)


def _lambda(lq1, lk1, lq2, lk2, lam_init):
    return (jnp.exp(jnp.sum(lq1 * lk1, axis=-1, keepdims=True))
            - jnp.exp(jnp.sum(lq2 * lk2, axis=-1, keepdims=True)) + lam_init)


def _diff_attn_head(q, ks, vs, lam, gain, lam_init):
    scale = HEAD_DIM ** -0.5

    def softmax_parts(m):
        cols = slice(m * HEAD_DIM, (m + 1) * HEAD_DIM)
        ss = [lax.dot_general(q[:, cols], k[:, cols], (((1,), (1,)), ((), ())),
                              preferred_element_type=F32) * scale for k in ks]
        mx = functools.reduce(jnp.maximum, [jnp.max(s, axis=-1, keepdims=True) for s in ss])
        es = [jnp.exp(s - mx) for s in ss]
        den = functools.reduce(jnp.add, [jnp.sum(e, axis=-1, keepdims=True) for e in es])
        return es, 1.0 / den

    e1, inv1 = softmax_parts(0)
    e2, inv2 = softmax_parts(1)
    w2 = lam * inv2
    o = functools.reduce(jnp.add, [
        jnp.dot((a * inv1 - b * w2).astype(BF16), v, preferred_element_type=F32)
        for a, b, v in zip(e1, e2, vs)])
    return _rms(o) * gain * (1.0 - lam_init)


def _attn_ctx_kernel(q_ref, k_ref, v_ref, lq1, lk1, lq2, lk2, g_ref, o_ref, *, lam_init):
    lam = _lambda(lq1[...], lk1[...], lq2[...], lk2[...], lam_init)
    gain = g_ref[...]
    for h in range(N_HEADS):
        cols = slice(h * V_DIM, (h + 1) * V_DIM)
        o = _diff_attn_head(q_ref[:, cols], [k_ref[:, cols]], [v_ref[:, cols]], lam, gain, lam_init)
        o_ref[:, cols] = o.astype(BF16)


def _attn_lat_kernel(q_ref, k_ref, v_ref, ck_ref, cv_ref, lq1, lk1, lq2, lk2, g_ref, o_ref, *,
                     lam_init):
    lam = _lambda(lq1[...], lk1[...], lq2[...], lk2[...], lam_init)
    ks = [ck_ref[...].astype(BF16), k_ref[...]]
    vs = [cv_ref[...].astype(BF16), v_ref[...]]
    o = _diff_attn_head(q_ref[...], ks, vs, lam, g_ref[...], lam_init)
    o_ref[...] = o.astype(BF16)


def _lam_specs(layer, n_axes):
    idx = (lambda b: (layer, 0, 0)) if n_axes == 1 else (lambda b, h, qi: (layer, 0, 0))
    return [pl.BlockSpec((None, 1, HEAD_DIM), idx)] * 4 + [pl.BlockSpec((None, 1, V_DIM), idx)]


def _attention_ctx(z, lam_args, layer, lam_init, *, seq, width):
    t = z.shape[0]
    kern = functools.partial(_attn_ctx_kernel, lam_init=lam_init)
    return pl.pallas_call(
        kern,
        grid=(t // seq,),
        in_specs=[pl.BlockSpec((seq, width), lambda b: (b, 0)),
                  pl.BlockSpec((seq, width), lambda b: (b, 1)),
                  pl.BlockSpec((seq, width), lambda b: (b, 2))] + _lam_specs(layer, 1),
        out_specs=pl.BlockSpec((seq, width), lambda b: (b, 0)),
        out_shape=jax.ShapeDtypeStruct((t, width), BF16),
        compiler_params=_cparams(1),
        name="attention_ctx",
    )(z, z, z, *lam_args)


def _attention_lat(z, cache_k, cache_v, lam_args, layer, lam_init, *, seq, width):
    t = z.shape[0]
    nb = t // seq
    nq = seq // TQ_LAT
    past = cache_k.shape[2]
    kern = functools.partial(_attn_lat_kernel, lam_init=lam_init)
    return pl.pallas_call(
        kern,
        grid=(nb, N_HEADS, nq),
        in_specs=[
            pl.BlockSpec((TQ_LAT, V_DIM), lambda b, h, qi: (b * nq + qi, h)),
            pl.BlockSpec((seq, V_DIM), lambda b, h, qi: (b, N_HEADS + h)),
            pl.BlockSpec((seq, V_DIM), lambda b, h, qi: (b, 2 * N_HEADS + h)),
            pl.BlockSpec((None, None, past, V_DIM), lambda b, h, qi: (b, layer, 0, h)),
            pl.BlockSpec((None, None, past, V_DIM), lambda b, h, qi: (b, layer, 0, h)),
        ] + _lam_specs(layer, 3),
        out_specs=pl.BlockSpec((TQ_LAT, V_DIM), lambda b, h, qi: (b * nq + qi, h)),
        out_shape=jax.ShapeDtypeStruct((t, width), BF16),
        compiler_params=_cparams(3),
        name="attention_lat",
    )(z, z, z, cache_k, cache_v, *lam_args)


def _seq_masks(row0, rows, seq):
    p = (row0 + lax.broadcasted_iota(jnp.int32, (rows, 1), 0)) % seq
    return (p != 0).astype(F32), (p != seq - 1).astype(F32)


def _pool_mixer(u_ref, upad_ref, pw_ref, ps_ref, lhs_ref, row0, seq):
    halo = BF16_SUBLANES
    tm = u_ref.shape[0]
    pg = pw_ref.shape[-1]
    span = POOL_SUB + 2 * halo
    t_idx = lax.broadcasted_iota(jnp.int32, (POOL_SUB, span), 0)
    c_idx = lax.broadcasted_iota(jnp.int32, (POOL_SUB, span), 1)
    off = c_idx - halo - t_idx
    t_col = lax.broadcasted_iota(jnp.int32, (POOL_SUB, 1), 0)
    for r in range(tm // POOL_SUB):
        pos0 = (row0 + r * POOL_SUB) % seq
        pos_c = pos0 + c_idx - halo
        in_seq = (pos_c >= 0) & (pos_c < seq)
        pos_t = pos0 + t_col
        for gi, w in enumerate(POOL_WINDOWS):
            half = w // 2
            band = ((off >= -half) & (off < w - half) & in_seq).astype(F32).astype(BF16)
            cnt = jnp.minimum(pos_t + (w - half), seq) - jnp.maximum(pos_t - half, 0)
            cols = slice(gi * pg, (gi + 1) * pg)
            win = jnp.dot(band, upad_ref[r * POOL_SUB:r * POOL_SUB + span, cols],
                          preferred_element_type=F32)
            mean = win / cnt.astype(F32)
            dlt = mean - u_ref[r * POOL_SUB:(r + 1) * POOL_SUB, cols].astype(F32)
            y = jnp.dot(dlt.astype(BF16), pw_ref[gi], preferred_element_type=F32) * ps_ref[:, cols]
            lhs_ref[r * POOL_SUB:(r + 1) * POOL_SUB, cols] = y.astype(BF16)


def _branches_kernel(u_ref, up_ref, un_ref, cb_ref, ccx_ref, ccxp_ref, ccxn_ref, ya_ref, g_ref,
                     w_ref, pw_ref, ps_ref, cw_ref, cbias_ref, o_ref,
                     acc_ref, lhs_ref, upad_ref, ppad_ref, *, seq):
    i = pl.program_id(0)
    s = pl.program_id(1)
    tm = u_ref.shape[0]
    halo = BF16_SUBLANES
    cwid = cb_ref.shape[1]
    row0 = i * tm

    def gated(lhs):
        gate = jax.nn.sigmoid(g_ref[...].astype(F32))
        return gate * jnp.dot(lhs, w_ref[...], preferred_element_type=F32)

    @pl.when(s == 0)
    def _():
        upad_ref[0:halo, :] = up_ref[...]
        upad_ref[halo:halo + tm, :] = u_ref[...]
        upad_ref[halo + tm:, :] = un_ref[...]
        _pool_mixer(u_ref, upad_ref, pw_ref, ps_ref, lhs_ref, row0, seq)
        acc_ref[...] = gated(lhs_ref[...])

    @pl.when((s == 1) | (s == 2))
    def _():
        acc_ref[...] += gated(ya_ref[...])

    @pl.when(s == 3)
    def _():
        def prod(ref):
            v = ref[...].astype(F32)
            return v[:, :cwid] * v[:, cwid:]

        ppad_ref[0:halo, :] = prod(ccxp_ref)
        ppad_ref[halo:halo + tm, :] = prod(ccx_ref)
        ppad_ref[halo + tm:, :] = prod(ccxn_ref)
        m_prev, m_next = _seq_masks(row0, tm, seq)
        p_all = ppad_ref[...]
        p_prev = pltpu.roll(p_all, 1, axis=0)[halo:halo + tm] * m_prev
        p_next = pltpu.roll(p_all, tm + 2 * halo - 1, axis=0)[halo:halo + tm] * m_next
        conv = (p_prev * cw_ref[0:1, :] + p_all[halo:halo + tm] * cw_ref[1:2, :]
                + p_next * cw_ref[2:3, :] + cbias_ref[...])
        y = cb_ref[...].astype(F32) * conv
        o_ref[...] = (acc_ref[...] + gated(y.astype(BF16))).astype(BF16)


def _branches(z, y_attn, w_br, pool_w, pool_scale, conv_w, conv_b, layer, *, seq, d, pool_w_cols,
              conv_cols, col0):
    t = z.shape[0]
    tm = TM_MIX
    halo = BF16_SUBLANES
    hb = tm // halo
    last_hb = t // halo - 1
    kc = w_br.shape[1] // 4
    prev = lambda i, s: jnp.maximum(i * hb - 1, 0)
    nxt = lambda i, s: jnp.minimum((i + 1) * hb, last_hb)
    pool_blk = col0["pool"] // pool_w_cols
    cb_blk = col0["cb"] // conv_cols
    ccx_blk = col0["ccx"] // (2 * conv_cols)
    g_blk = col0["g"] // d
    kern = functools.partial(_branches_kernel, seq=seq)
    return pl.pallas_call(
        kern,
        grid=(t // tm, 4),
        in_specs=[
            pl.BlockSpec((tm, pool_w_cols), lambda i, s: (i, pool_blk)),
            pl.BlockSpec((halo, pool_w_cols), lambda i, s: (prev(i, s), pool_blk)),
            pl.BlockSpec((halo, pool_w_cols), lambda i, s: (nxt(i, s), pool_blk)),
            pl.BlockSpec((tm, conv_cols), lambda i, s: (i, cb_blk)),
            pl.BlockSpec((tm, 2 * conv_cols), lambda i, s: (i, ccx_blk)),
            pl.BlockSpec((halo, 2 * conv_cols), lambda i, s: (prev(i, s), ccx_blk)),
            pl.BlockSpec((halo, 2 * conv_cols), lambda i, s: (nxt(i, s), ccx_blk)),
            pl.BlockSpec((tm, kc), lambda i, s: (i, jnp.clip(s - 1, 0, 1))),
            pl.BlockSpec((tm, d), lambda i, s: (i, g_blk + (s + 1) // 2)),
            pl.BlockSpec((None, kc, d), lambda i, s: (layer, s, 0)),
            pl.BlockSpec((None,) + pool_w.shape[1:], lambda i, s: (layer, 0, 0, 0)),
            pl.BlockSpec((None, 1, pool_w_cols), lambda i, s: (layer, 0, 0)),
            pl.BlockSpec((None, 3, conv_cols), lambda i, s: (layer, 0, 0)),
            pl.BlockSpec((None, 1, conv_cols), lambda i, s: (layer, 0, 0)),
        ],
        out_specs=pl.BlockSpec((tm, d), lambda i, s: (i, 0)),
        out_shape=jax.ShapeDtypeStruct((t, d), BF16),
        scratch_shapes=[
            pltpu.VMEM((tm, d), F32),
            pltpu.VMEM((tm, pool_w_cols), BF16),
            pltpu.VMEM((tm + 2 * halo, pool_w_cols), BF16),
            pltpu.VMEM((tm + 2 * halo, conv_cols), F32),
        ],
        compiler_params=_cparams(2),
        name="branches",
    )(z, z, z, z, z, z, z, y_attn, z, w_br, pool_w, pool_scale, conv_w, conv_b)


def _out_proj_kernel(m_ref, w_ref, x_ref, gt_ref, g_ref, o_ref):
    m = jnp.dot(m_ref[...], w_ref[...], preferred_element_type=F32)
    o_ref[...] = x_ref[...] + gt_ref[0] * (_rms(m) * g_ref[...])


def _out_proj(merged, w_out, x, mod3, g_post, layer, mod_row):
    t, d = x.shape
    tm = TM_MIX
    return pl.pallas_call(
        _out_proj_kernel,
        grid=(t // tm,),
        in_specs=[
            pl.BlockSpec((tm, d), lambda i: (i, 0)),
            pl.BlockSpec((None, d, d), lambda i: (layer, 0, 0)),
            pl.BlockSpec((tm, d), lambda i: (i, 0)),
            pl.BlockSpec((1, 1, d), lambda i: (mod_row(i * tm) * N_MOD + 2, 0, 0)),
            pl.BlockSpec((1, d), lambda i: (0, 0)),
        ],
        out_specs=pl.BlockSpec((tm, d), lambda i: (i, 0)),
        out_shape=jax.ShapeDtypeStruct((t, d), F32),
        compiler_params=_cparams(1),
        name="out_proj",
    )(merged, w_out, x, mod3, g_post)


def _ffn_kernel(x_ref, xp_ref, xn_ref, sc_ref, sh_ref, gt_ref, gpre_ref, gpost_ref,
                wg_ref, wv_ref, cwg_ref, cwv_ref, cbg_ref, cbv_ref, wd_ref, o_ref,
                h_ref, acc_ref, *, seq):
    i = pl.program_id(0)
    f = pl.program_id(1)
    tm = x_ref.shape[0]
    halo = BF16_SUBLANES
    rows_all = tm + 2 * halo

    @pl.when(f == 0)
    def _():
        scale = 1.0 + sc_ref[0]
        shift = sh_ref[0]
        gain = gpre_ref[...]

        def modulated(x):
            return (_rms(x) * gain * scale + shift).astype(BF16)

        h_ref[0:halo, :] = modulated(xp_ref[...])
        h_ref[halo + tm:, :] = modulated(xn_ref[...])
        chunk = 256

        def body(r, carry):
            src = pl.ds(pl.multiple_of(r * chunk, chunk), chunk)
            dst = pl.ds(pl.multiple_of(r * chunk + halo, halo), chunk)
            h_ref[dst, :] = modulated(x_ref[src, :])
            return carry

        lax.fori_loop(0, tm // chunk, body, 0)
        acc_ref[...] = jnp.zeros_like(acc_ref)

    m_prev, m_next = _seq_masks(i * tm, tm, seq)
    h = h_ref[...]

    def conv_half(w_ref, cw_ref, cb_ref):
        u = jnp.dot(h, w_ref[...], preferred_element_type=F32)
        u_prev = pltpu.roll(u, 1, axis=0)[halo:halo + tm] * m_prev
        u_next = pltpu.roll(u, rows_all - 1, axis=0)[halo:halo + tm] * m_next
        return (u_prev * cw_ref[0:1, :] + u[halo:halo + tm] * cw_ref[1:2, :]
                + u_next * cw_ref[2:3, :] + cb_ref[...])

    gate = conv_half(wg_ref, cwg_ref, cbg_ref)
    val = conv_half(wv_ref, cwv_ref, cbv_ref)
    a = (jax.nn.gelu(gate, approximate=True) * val).astype(BF16)
    acc_ref[...] += jnp.dot(a, wd_ref[...], preferred_element_type=F32)

    @pl.when(f == pl.num_programs(1) - 1)
    def _():
        o_ref[...] = x_ref[...] + gt_ref[0] * (_rms(acc_ref[...]) * gpost_ref[...])


def _ffn(x, mod3, g_pre, g_post, w_up, conv_w, conv_b, w_down, layer, mod_row, *, seq):
    t, d = x.shape
    tm, tf = TM_MIX, TF_FFN
    halo = BF16_SUBLANES
    hb = tm // halo
    last_hb = t // halo - 1
    nf = w_down.shape[1] // tf
    prev = lambda i, f: (jnp.maximum(i * hb - 1, 0), 0)
    nxt = lambda i, f: (jnp.minimum((i + 1) * hb, last_hb), 0)

    def mod_idx(m):
        return lambda i, f: (mod_row(i * tm) * N_MOD + m, 0, 0)

    kern = functools.partial(_ffn_kernel, seq=seq)
    return pl.pallas_call(
        kern,
        grid=(t // tm, nf),
        in_specs=[
            pl.BlockSpec((tm, d), lambda i, f: (i, 0)),
            pl.BlockSpec((halo, d), prev),
            pl.BlockSpec((halo, d), nxt),
            pl.BlockSpec((1, 1, d), mod_idx(4)),
            pl.BlockSpec((1, 1, d), mod_idx(3)),
            pl.BlockSpec((1, 1, d), mod_idx(5)),
            pl.BlockSpec((1, d), lambda i, f: (0, 0)),
            pl.BlockSpec((1, d), lambda i, f: (0, 0)),
            pl.BlockSpec((None, d, tf), lambda i, f: (layer, 0, f)),
            pl.BlockSpec((None, d, tf), lambda i, f: (layer, 0, nf + f)),
            pl.BlockSpec((None, 3, tf), lambda i, f: (layer, 0, f)),
            pl.BlockSpec((None, 3, tf), lambda i, f: (layer, 0, nf + f)),
            pl.BlockSpec((None, 1, tf), lambda i, f: (layer, 0, f)),
            pl.BlockSpec((None, 1, tf), lambda i, f: (layer, 0, nf + f)),
            pl.BlockSpec((None, tf, d), lambda i, f: (layer, f, 0)),
        ],
        out_specs=pl.BlockSpec((tm, d), lambda i, f: (i, 0)),
        out_shape=jax.ShapeDtypeStruct((t, d), F32),
        scratch_shapes=[pltpu.VMEM((tm + 2 * halo, d), BF16), pltpu.VMEM((tm, d), F32)],
        compiler_params=_cparams(2),
        name="ffn",
    )(x, x, x, mod3, mod3, mod3, g_pre, g_post, w_up, w_up, conv_w, conv_w, conv_b, conv_b, w_down)


def _rope_tables(seq):
    rows = seq // GRID_W
    row = jnp.repeat(jnp.arange(rows), GRID_W)
    col = jnp.tile(jnp.arange(GRID_W), rows)
    inv = ROPE_BASE ** (-jnp.arange(ROPE_FREQS, dtype=F32) / ROPE_FREQS)
    ang = jnp.stack([row, col], axis=-1).astype(F32)[:, :, None] * inv
    cos, sin = jnp.cos(ang), jnp.sin(ang)
    cos_t = jnp.concatenate([cos[:, 0], cos[:, 0], cos[:, 1], cos[:, 1]], axis=-1)
    sin_t = jnp.concatenate([-sin[:, 0], sin[:, 0], -sin[:, 1], sin[:, 1]], axis=-1)
    return cos_t, sin_t


def _pad_axis(a, axis, size):
    pad = [(0, 0)] * a.ndim
    pad[axis] = (0, size - a.shape[axis])
    return jnp.pad(a, pad)


def kernel(x_prompt, x_sample, cache_k, cache_v, c, c_ctx, w_mod, b_mod, g_pre_mix, g_post_mix,
           g_pre_ffn, g_post_ffn, w_in, pool_w, pool_scale, lambda_q1, lambda_k1, lambda_q2, lambda_k2,
           g_subln, conv_w, conv_b, w_br_pool, w_br_attn, w_br_conv, w_out, ffn_w_up, ffn_conv_w,
           ffn_conv_b, ffn_w_down):
    batch, seq, d = x_prompt.shape
    dec_batch, dec_seq, _ = x_sample.shape
    depth = w_in.shape[0]
    past = cache_k.shape[2]
    qk_w = N_HEADS * 2 * HEAD_DIM
    v_w = N_HEADS * V_DIM
    pool_cols = pool_scale.shape[1]
    conv_cols = conv_b.shape[1]
    d_ff = ffn_w_down.shape[1]
    d_ffp = -(-d_ff // TF_FFN) * TF_FFN
    assert qk_w == v_w and dec_batch + 1 <= MOD_ROWS

    o_pool, o_q, o_k, o_v = 0, pool_cols, pool_cols + qk_w, pool_cols + 2 * qk_w
    o_conv = o_v + v_w
    w_in_r = jnp.concatenate([w_in[:, :, o_q:o_conv], w_in[:, :, o_pool:o_q], w_in[:, :, o_conv:]],
                             axis=2).astype(BF16)
    col0 = {"pool": 3 * qk_w, "cb": 3 * qk_w + pool_cols, "ccx": 3 * qk_w + pool_cols + conv_cols,
            "g": 3 * qk_w + pool_cols + 3 * conv_cols}
    w_br = jnp.concatenate([w_br_pool, w_br_attn, w_br_conv], axis=1).astype(BF16)
    w_out_b = w_out.astype(BF16)
    pool_w_b = pool_w.astype(BF16)
    w_up = jnp.concatenate([_pad_axis(ffn_w_up[:, :, :d_ff], 2, d_ffp),
                            _pad_axis(ffn_w_up[:, :, d_ff:], 2, d_ffp)], axis=2).astype(BF16)
    fcw = jnp.concatenate([_pad_axis(ffn_conv_w[:, :, :d_ff], 2, d_ffp),
                           _pad_axis(ffn_conv_w[:, :, d_ff:], 2, d_ffp)], axis=2)
    fcb = jnp.concatenate([_pad_axis(ffn_conv_b[:, :d_ff], 1, d_ffp),
                           _pad_axis(ffn_conv_b[:, d_ff:], 1, d_ffp)], axis=1)[:, None, :]
    w_down = _pad_axis(ffn_w_down, 1, d_ffp).astype(BF16)

    cvec = _pad_axis(jnp.concatenate([c_ctx[None, :], c], axis=0), 0, MOD_ROWS)
    mod = _modulation(cvec, w_mod, b_mod)
    mod3 = mod.reshape(depth * MOD_ROWS * N_MOD, 1, d)

    rope_tabs = _rope_tables(dec_seq)
    ck = cache_k.reshape(dec_batch, depth, past, qk_w)
    cv = cache_v.reshape(dec_batch, depth, past, v_w)
    row1 = lambda a: a[:, None, :]
    lam_args = (row1(lambda_q1), row1(lambda_k1), row1(lambda_q2), row1(lambda_k2), row1(g_subln))
    pool_scale3, conv_b3 = row1(pool_scale), row1(conv_b)

    xc = x_prompt.reshape(batch * seq, d)
    xl = x_sample.reshape(dec_batch * dec_seq, d)
    new_k, new_v = [], []
    for l in range(depth):
        lam_init = 0.8 - 0.6 * math.exp(-0.3 * l)
        ctx_row = lambda r, l=l: l * MOD_ROWS
        lat_row = lambda r, l=l: l * MOD_ROWS + 1 + r // dec_seq
        gl = lambda a: a[l][None, :]
        mix = functools.partial(_branches, w_br=w_br, pool_w=pool_w_b, pool_scale=pool_scale3,
                                conv_w=conv_w, conv_b=conv_b3, layer=l, d=d, pool_w_cols=pool_cols,
                                conv_cols=conv_cols, col0=col0)
        ffn = functools.partial(_ffn, w_up=w_up, conv_w=fcw, conv_b=fcb, w_down=w_down, layer=l)

        z, k32, v32 = _in_proj(xc, mod3, gl(g_pre_mix), w_in_r, l, ctx_row, qk_w=qk_w, kv_out=True)
        new_k.append(k32.reshape(batch, seq, N_HEADS, 2, HEAD_DIM))
        new_v.append(v32.reshape(batch, seq, N_HEADS, V_DIM))
        ya = _attention_ctx(z, lam_args, l, lam_init, seq=seq, width=v_w)
        merged = mix(z, ya, seq=seq)
        xc = _out_proj(merged, w_out_b, xc, mod3, gl(g_post_mix), l, ctx_row)
        xc = ffn(xc, mod3, gl(g_pre_ffn), gl(g_post_ffn), mod_row=ctx_row, seq=seq)

        z = _in_proj(xl, mod3, gl(g_pre_mix), w_in_r, l, lat_row, qk_w=qk_w, rope_tabs=rope_tabs)[0]
        ya = _attention_lat(z, ck, cv, lam_args, l, lam_init, seq=dec_seq, width=v_w)
        merged = mix(z, ya, seq=dec_seq)
        xl = _out_proj(merged, w_out_b, xl, mod3, gl(g_post_mix), l, lat_row)
        xl = ffn(xl, mod3, gl(g_pre_ffn), gl(g_post_ffn), mod_row=lat_row, seq=dec_seq)

    return (xc.reshape(batch, seq, d), xl.reshape(dec_batch, dec_seq, d),
            jnp.stack(new_k, axis=1), jnp.stack(new_v, axis=1))
```

```python
import functools
import math

import jax
import jax.numpy as jnp
from jax import lax
from jax.experimental import pallas as pl
from jax.experimental.pallas import tpu as pltpu

F32 = jnp.float32
BF16 = jnp.bfloat16

GRID_W = 64
N_HEADS = 8
HEAD_DIM = 128
V_DIM = 2 * HEAD_DIM
POOL_WINDOWS = (2, 4, 8, 16)
ROPE_FREQS = HEAD_DIM // 4
ROPE_BASE = 10000.0
EPS = 1e-6
Q_SCALE = HEAD_DIM ** -0.5 * math.log2(math.e)
N_MOD = 6
N_BRANCH = 3

LANES = 128
BF16_SUBLANES = 16
VMEM_LIMIT_BYTES = 56 * 1024 * 1024
MOD_ROWS = 8

TM_IN = 1024
TN_IN = 512
TM_MIX = 512
POOL_SUB = 256
TQ_LAT = 256
TF_FFN = 512
TN_MOD = 1024


def _cparams(n_axes):
    return pltpu.CompilerParams(dimension_semantics=("arbitrary",) * n_axes,
                                vmem_limit_bytes=VMEM_LIMIT_BYTES)


def _rms(x):
    return x * lax.rsqrt(jnp.mean(x * x, axis=-1, keepdims=True) + EPS)


def _mod_kernel(c_ref, w_ref, b_ref, o_ref):
    c = c_ref[...]
    s = (c * jax.nn.sigmoid(c)).astype(BF16)
    o_ref[...] = jnp.dot(s, w_ref[...].astype(BF16), preferred_element_type=F32) + b_ref[...]


def _modulation(cvec, w_mod, b_mod):
    depth, d, n = w_mod.shape
    return pl.pallas_call(
        _mod_kernel,
        grid=(depth, n // TN_MOD),
        in_specs=[
            pl.BlockSpec((MOD_ROWS, d), lambda l, j: (0, 0)),
            pl.BlockSpec((None, d, TN_MOD), lambda l, j: (l, 0, j)),
            pl.BlockSpec((None, 1, TN_MOD), lambda l, j: (l, 0, j)),
        ],
        out_specs=pl.BlockSpec((None, MOD_ROWS, TN_MOD), lambda l, j: (l, 0, j)),
        out_shape=jax.ShapeDtypeStruct((depth, MOD_ROWS, n), F32),
        compiler_params=_cparams(2),
        name="modulation",
    )(cvec, w_mod, b_mod.reshape(depth, 1, n))


def _rope_tile(acc, c, s):
    lane = lax.broadcasted_iota(jnp.int32, c.shape, 1)
    upper = (lane & ROPE_FREQS) != 0
    outs = []
    for g in range(acc.shape[1] // LANES):
        xg = acc[:, g * LANES:(g + 1) * LANES]
        partner = jnp.where(upper, pltpu.roll(xg, ROPE_FREQS, axis=1),
                            pltpu.roll(xg, LANES - ROPE_FREQS, axis=1))
        outs.append(xg * c + partner * s)
    return jnp.concatenate(outs, axis=1)


def _in_proj_kernel(*refs, rope, kv_out, q_tiles, qk_tiles, k_tiles, v_tiles, norm_rows):
    x_ref, sc_ref, sh_ref, g_ref, w_ref = refs[:5]
    pos = 5
    if rope:
        cos_ref, sin_ref = refs[pos:pos + 2]
        pos += 2
    z_ref = refs[pos]
    pos += 1
    if kv_out:
        k_ref, v_ref = refs[pos:pos + 2]
        pos += 2
    h_ref = refs[pos]
    j = pl.program_id(1)

    @pl.when(j == 0)
    def _():
        scale = 1.0 + sc_ref[0]
        shift = sh_ref[0]
        gain = g_ref[...]

        def body(r, carry):
            rows = pl.ds(pl.multiple_of(r * norm_rows, norm_rows), norm_rows)
            h = _rms(x_ref[rows, :]) * gain * scale + shift
            h_ref[rows, :] = h.astype(BF16)
            return carry

        lax.fori_loop(0, x_ref.shape[0] // norm_rows, body, 0)

    acc = jnp.dot(h_ref[...], w_ref[...].astype(BF16), preferred_element_type=F32)

    def rotated(a):
        return _rope_tile(a, cos_ref[...], sin_ref[...]) if rope else a

    @pl.when(j < q_tiles)
    def _():
        z_ref[...] = rotated(acc * Q_SCALE).astype(BF16)

    @pl.when((j >= q_tiles) & (j < qk_tiles))
    def _():
        z_ref[...] = rotated(acc).astype(BF16)

    @pl.when(j >= qk_tiles)
    def _():
        z_ref[...] = acc.astype(BF16)

    if kv_out:
        @pl.when((j >= k_tiles[0]) & (j < k_tiles[1]))
        def _():
            k_ref[...] = acc

        @pl.when((j >= v_tiles[0]) & (j < v_tiles[1]))
        def _():
            v_ref[...] = acc


def _in_proj(x, mod3, g_pre, w_in, layer, mod_row, *, qk_w, pool_cols, rope_tabs=None, kv_out=False):
    t, d = x.shape
    n = w_in.shape[2]
    tm, tn = TM_IN, TN_IN
    nq = qk_w // tn
    n_pool = pool_cols // tn
    rope = rope_tabs is not None

    def w_tile(j):
        return jnp.where(j < 3 * nq, j + n_pool, jnp.where(j < 3 * nq + n_pool, j - 3 * nq, j))

    def mod_idx(m):
        return lambda i, j: (mod_row(i * tm) * N_MOD + m, 0, 0)

    in_specs = [
        pl.BlockSpec((tm, d), lambda i, j: (i, 0)),
        pl.BlockSpec((1, 1, d), mod_idx(1)),
        pl.BlockSpec((1, 1, d), mod_idx(0)),
        pl.BlockSpec((1, d), lambda i, j: (0, 0)),
        pl.BlockSpec((None, d, tn), lambda i, j: (layer, 0, w_tile(j))),
    ]
    args = [x, mod3, mod3, g_pre, w_in]
    if rope:
        tab_blocks = rope_tabs[0].shape[0] // tm
        tab_spec = pl.BlockSpec((tm, LANES), lambda i, j: (i % tab_blocks, 0))
        in_specs += [tab_spec, tab_spec]
        args += list(rope_tabs)
    out_specs = [pl.BlockSpec((tm, tn), lambda i, j: (i, j))]
    out_shape = [jax.ShapeDtypeStruct((t, n), BF16)]
    if kv_out:
        out_specs += [
            pl.BlockSpec((tm, tn), lambda i, j: (i, jnp.clip(j - nq, 0, nq - 1))),
            pl.BlockSpec((tm, tn), lambda i, j: (i, jnp.clip(j - 2 * nq, 0, nq - 1))),
        ]
        out_shape += [jax.ShapeDtypeStruct((t, qk_w), F32)] * 2
    kern = functools.partial(_in_proj_kernel, rope=rope, kv_out=kv_out, q_tiles=nq, qk_tiles=2 * nq,
                             k_tiles=(nq, 2 * nq), v_tiles=(2 * nq, 3 * nq), norm_rows=256)
    return pl.pallas_call(
        kern,
        grid=(t // tm, n // tn),
        in_specs=in_specs,
        out_specs=out_specs,
        out_shape=out_shape,
        scratch_shapes=[pltpu.VMEM((tm, d), BF16)],
        compiler_params=_cparams(2),
        name="in_proj_rope" if rope else "in_proj_kv",
    )(*args)


def _lambda(lq1, lk1, lq2, lk2, lam_init):
    return (jnp.exp(jnp.sum(lq1 * lk1, axis=-1, keepdims=True))
            - jnp.exp(jnp.sum(lq2 * lk2, axis=-1, keepdims=True)) + lam_init)


def _diff_attn_head(q, ks, vs, lam, gain, lam_init):
    def softmax_parts(m):
        cols = slice(m * HEAD_DIM, (m + 1) * HEAD_DIM)
        ss = [lax.dot_general(q[:, cols], k[:, cols], (((1,), (1,)), ((), ())),
                              preferred_element_type=F32) for k in ks]
        mx = functools.reduce(jnp.maximum, [jnp.max(s, axis=-1, keepdims=True) for s in ss])
        es = [jnp.exp2(s - mx) for s in ss]
        den = functools.reduce(jnp.add, [jnp.sum(e, axis=-1, keepdims=True) for e in es])
        return es, 1.0 / den

    e1, inv1 = softmax_parts(0)
    e2, inv2 = softmax_parts(1)
    w2 = lam * inv2
    o = functools.reduce(jnp.add, [
        jnp.dot((a * inv1 - b * w2).astype(BF16), v, preferred_element_type=F32)
        for a, b, v in zip(e1, e2, vs)])
    return _rms(o) * gain * (1.0 - lam_init)


def _attn_ctx_kernel(q_ref, k_ref, v_ref, lq1, lk1, lq2, lk2, g_ref, o_ref, *, lam_init):
    lam = _lambda(lq1[...], lk1[...], lq2[...], lk2[...], lam_init)
    gain = g_ref[...]
    for h in range(N_HEADS):
        cols = slice(h * V_DIM, (h + 1) * V_DIM)
        o = _diff_attn_head(q_ref[:, cols], [k_ref[:, cols]], [v_ref[:, cols]], lam, gain, lam_init)
        o_ref[:, cols] = o.astype(BF16)


def _attn_lat_kernel(q_ref, k_ref, v_ref, ck_ref, cv_ref, lq1, lk1, lq2, lk2, g_ref, o_ref, *,
                     lam_init):
    lam = _lambda(lq1[...], lk1[...], lq2[...], lk2[...], lam_init)
    ks = [ck_ref[...].astype(BF16), k_ref[...]]
    vs = [cv_ref[...].astype(BF16), v_ref[...]]
    o = _diff_attn_head(q_ref[...], ks, vs, lam, g_ref[...], lam_init)
    o_ref[...] = o.astype(BF16)


def _lam_specs(layer, n_axes):
    idx = (lambda b: (layer, 0, 0)) if n_axes == 1 else (lambda b, h, qi: (layer, 0, 0))
    return [pl.BlockSpec((None, 1, HEAD_DIM), idx)] * 4 + [pl.BlockSpec((None, 1, V_DIM), idx)]


def _attention_ctx(z, lam_args, layer, lam_init, *, seq, width):
    t = z.shape[0]
    kern = functools.partial(_attn_ctx_kernel, lam_init=lam_init)
    return pl.pallas_call(
        kern,
        grid=(t // seq,),
        in_specs=[pl.BlockSpec((seq, width), lambda b: (b, 0)),
                  pl.BlockSpec((seq, width), lambda b: (b, 1)),
                  pl.BlockSpec((seq, width), lambda b: (b, 2))] + _lam_specs(layer, 1),
        out_specs=pl.BlockSpec((seq, width), lambda b: (b, 0)),
        out_shape=jax.ShapeDtypeStruct((t, width), BF16),
        compiler_params=_cparams(1),
        name="attention_ctx",
    )(z, z, z, *lam_args)


def _attention_lat(z, cache_k, cache_v, lam_args, layer, lam_init, *, seq, width):
    t = z.shape[0]
    nb = t // seq
    nq = seq // TQ_LAT
    past = cache_k.shape[2]
    kern = functools.partial(_attn_lat_kernel, lam_init=lam_init)
    return pl.pallas_call(
        kern,
        grid=(nb, N_HEADS, nq),
        in_specs=[
            pl.BlockSpec((TQ_LAT, V_DIM), lambda b, h, qi: (b * nq + qi, h)),
            pl.BlockSpec((seq, V_DIM), lambda b, h, qi: (b, N_HEADS + h)),
            pl.BlockSpec((seq, V_DIM), lambda b, h, qi: (b, 2 * N_HEADS + h)),
            pl.BlockSpec((None, None, past, V_DIM), lambda b, h, qi: (b, layer, 0, h)),
            pl.BlockSpec((None, None, past, V_DIM), lambda b, h, qi: (b, layer, 0, h)),
        ] + _lam_specs(layer, 3),
        out_specs=pl.BlockSpec((TQ_LAT, V_DIM), lambda b, h, qi: (b * nq + qi, h)),
        out_shape=jax.ShapeDtypeStruct((t, width), BF16),
        compiler_params=_cparams(3),
        name="attention_lat",
    )(z, z, z, cache_k, cache_v, *lam_args)


def _seq_masks(row0, rows, seq):
    p = (row0 + lax.broadcasted_iota(jnp.int32, (rows, 1), 0)) % seq
    return (p != 0).astype(F32), (p != seq - 1).astype(F32)


def _pool_mixer(u_ref, upad_ref, pw_ref, ps_ref, lhs_ref, row0, seq):
    halo = BF16_SUBLANES
    tm = u_ref.shape[0]
    pg = pw_ref.shape[-1]
    span = POOL_SUB + 2 * halo
    t_idx = lax.broadcasted_iota(jnp.int32, (POOL_SUB, span), 0)
    c_idx = lax.broadcasted_iota(jnp.int32, (POOL_SUB, span), 1)
    off = c_idx - halo - t_idx
    t_col = lax.broadcasted_iota(jnp.int32, (POOL_SUB, 1), 0)
    for r in range(tm // POOL_SUB):
        pos0 = (row0 + r * POOL_SUB) % seq
        pos_c = pos0 + c_idx - halo
        in_seq = (pos_c >= 0) & (pos_c < seq)
        pos_t = pos0 + t_col
        for gi, w in enumerate(POOL_WINDOWS):
            half = w // 2
            band = ((off >= -half) & (off < w - half) & in_seq).astype(F32).astype(BF16)
            cnt = jnp.minimum(pos_t + (w - half), seq) - jnp.maximum(pos_t - half, 0)
            cols = slice(gi * pg, (gi + 1) * pg)
            win = jnp.dot(band, upad_ref[r * POOL_SUB:r * POOL_SUB + span, cols],
                          preferred_element_type=F32)
            mean = win / cnt.astype(F32)
            dlt = mean - u_ref[r * POOL_SUB:(r + 1) * POOL_SUB, cols].astype(F32)
            y = jnp.dot(dlt.astype(BF16), pw_ref[gi], preferred_element_type=F32) * ps_ref[:, cols]
            lhs_ref[r * POOL_SUB:(r + 1) * POOL_SUB, cols] = y.astype(BF16)


def _branches_kernel(u_ref, up_ref, un_ref, cb_ref, ccx_ref, ccxp_ref, ccxn_ref, ya_ref, g_ref,
                     w_ref, pw_ref, ps_ref, cw_ref, cbias_ref, o_ref,
                     acc_ref, lhs_ref, upad_ref, ppad_ref, *, seq):
    i = pl.program_id(0)
    s = pl.program_id(1)
    tm = u_ref.shape[0]
    halo = BF16_SUBLANES
    cwid = cb_ref.shape[1]
    row0 = i * tm

    def gated(lhs):
        gate = jax.nn.sigmoid(g_ref[...].astype(F32))
        return gate * jnp.dot(lhs, w_ref[...], preferred_element_type=F32)

    @pl.when(s == 0)
    def _():
        upad_ref[0:halo, :] = up_ref[...]
        upad_ref[halo:halo + tm, :] = u_ref[...]
        upad_ref[halo + tm:, :] = un_ref[...]
        _pool_mixer(u_ref, upad_ref, pw_ref, ps_ref, lhs_ref, row0, seq)
        acc_ref[...] = gated(lhs_ref[...])

    @pl.when((s == 1) | (s == 2))
    def _():
        acc_ref[...] += gated(ya_ref[...])

    @pl.when(s == 3)
    def _():
        def prod(ref):
            v = ref[...].astype(F32)
            return v[:, :cwid] * v[:, cwid:]

        ppad_ref[0:halo, :] = prod(ccxp_ref)
        ppad_ref[halo:halo + tm, :] = prod(ccx_ref)
        ppad_ref[halo + tm:, :] = prod(ccxn_ref)
        m_prev, m_next = _seq_masks(row0, tm, seq)
        p_all = ppad_ref[...]
        p_prev = pltpu.roll(p_all, 1, axis=0)[halo:halo + tm] * m_prev
        p_next = pltpu.roll(p_all, tm + 2 * halo - 1, axis=0)[halo:halo + tm] * m_next
        conv = (p_prev * cw_ref[0:1, :] + p_all[halo:halo + tm] * cw_ref[1:2, :]
                + p_next * cw_ref[2:3, :] + cbias_ref[...])
        y = cb_ref[...].astype(F32) * conv
        o_ref[...] = (acc_ref[...] + gated(y.astype(BF16))).astype(BF16)


def _branches(z, y_attn, w_br, pool_w, pool_scale, conv_w, conv_b, layer, *, seq, d, pool_w_cols,
              conv_cols, col0):
    t = z.shape[0]
    tm = TM_MIX
    halo = BF16_SUBLANES
    hb = tm // halo
    last_hb = t // halo - 1
    kc = w_br.shape[1] // 4
    prev = lambda i, s: jnp.maximum(i * hb - 1, 0)
    nxt = lambda i, s: jnp.minimum((i + 1) * hb, last_hb)
    pool_blk = col0["pool"] // pool_w_cols
    cb_blk = col0["cb"] // conv_cols
    ccx_blk = col0["ccx"] // (2 * conv_cols)
    g_blk = col0["g"] // d
    kern = functools.partial(_branches_kernel, seq=seq)
    return pl.pallas_call(
        kern,
        grid=(t // tm, 4),
        in_specs=[
            pl.BlockSpec((tm, pool_w_cols), lambda i, s: (i, pool_blk)),
            pl.BlockSpec((halo, pool_w_cols), lambda i, s: (prev(i, s), pool_blk)),
            pl.BlockSpec((halo, pool_w_cols), lambda i, s: (nxt(i, s), pool_blk)),
            pl.BlockSpec((tm, conv_cols), lambda i, s: (i, cb_blk)),
            pl.BlockSpec((tm, 2 * conv_cols), lambda i, s: (i, ccx_blk)),
            pl.BlockSpec((halo, 2 * conv_cols), lambda i, s: (prev(i, s), ccx_blk)),
            pl.BlockSpec((halo, 2 * conv_cols), lambda i, s: (nxt(i, s), ccx_blk)),
            pl.BlockSpec((tm, kc), lambda i, s: (i, jnp.clip(s - 1, 0, 1))),
            pl.BlockSpec((tm, d), lambda i, s: (i, g_blk + (s + 1) // 2)),
            pl.BlockSpec((None, kc, d), lambda i, s: (layer, s, 0)),
            pl.BlockSpec((None,) + pool_w.shape[1:], lambda i, s: (layer, 0, 0, 0)),
            pl.BlockSpec((None, 1, pool_w_cols), lambda i, s: (layer, 0, 0)),
            pl.BlockSpec((None, 3, conv_cols), lambda i, s: (layer, 0, 0)),
            pl.BlockSpec((None, 1, conv_cols), lambda i, s: (layer, 0, 0)),
        ],
        out_specs=pl.BlockSpec((tm, d), lambda i, s: (i, 0)),
        out_shape=jax.ShapeDtypeStruct((t, d), BF16),
        scratch_shapes=[
            pltpu.VMEM((tm, d), F32),
            pltpu.VMEM((tm, pool_w_cols), BF16),
            pltpu.VMEM((tm + 2 * halo, pool_w_cols), BF16),
            pltpu.VMEM((tm + 2 * halo, conv_cols), F32),
        ],
        compiler_params=_cparams(2),
        name="branches",
    )(z, z, z, z, z, z, z, y_attn, z, w_br, pool_w, pool_scale, conv_w, conv_b)


def _out_proj_kernel(m_ref, w_ref, x_ref, gt_ref, g_ref, o_ref):
    m = jnp.dot(m_ref[...], w_ref[...], preferred_element_type=F32)
    o_ref[...] = x_ref[...] + gt_ref[0] * (_rms(m) * g_ref[...])


def _out_proj(merged, w_out, x, mod3, g_post, layer, mod_row):
    t, d = x.shape
    tm = TM_MIX
    return pl.pallas_call(
        _out_proj_kernel,
        grid=(t // tm,),
        in_specs=[
            pl.BlockSpec((tm, d), lambda i: (i, 0)),
            pl.BlockSpec((None, d, d), lambda i: (layer, 0, 0)),
            pl.BlockSpec((tm, d), lambda i: (i, 0)),
            pl.BlockSpec((1, 1, d), lambda i: (mod_row(i * tm) * N_MOD + 2, 0, 0)),
            pl.BlockSpec((1, d), lambda i: (0, 0)),
        ],
        out_specs=pl.BlockSpec((tm, d), lambda i: (i, 0)),
        out_shape=jax.ShapeDtypeStruct((t, d), F32),
        compiler_params=_cparams(1),
        name="out_proj",
    )(merged, w_out, x, mod3, g_post)


def _ffn_kernel(x_ref, xp_ref, xn_ref, sc_ref, sh_ref, gt_ref, gpre_ref, gpost_ref,
                wg_ref, wv_ref, cwg_ref, cwv_ref, cbg_ref, cbv_ref, wd_ref, o_ref,
                h_ref, acc_ref, *, seq):
    i = pl.program_id(0)
    f = pl.program_id(1)
    tm = x_ref.shape[0]
    halo = BF16_SUBLANES
    rows_all = tm + 2 * halo

    @pl.when(f == 0)
    def _():
        scale = 1.0 + sc_ref[0]
        shift = sh_ref[0]
        gain = gpre_ref[...]

        def modulated(x):
            return (_rms(x) * gain * scale + shift).astype(BF16)

        h_ref[0:halo, :] = modulated(xp_ref[...])
        h_ref[halo + tm:, :] = modulated(xn_ref[...])
        chunk = 256

        def body(r, carry):
            src = pl.ds(pl.multiple_of(r * chunk, chunk), chunk)
            dst = pl.ds(pl.multiple_of(r * chunk + halo, halo), chunk)
            h_ref[dst, :] = modulated(x_ref[src, :])
            return carry

        lax.fori_loop(0, tm // chunk, body, 0)
        acc_ref[...] = jnp.zeros_like(acc_ref)

    m_prev, m_next = _seq_masks(i * tm, tm, seq)
    h = h_ref[...]

    def conv_half(w_ref, cw_ref, cb_ref):
        u = jnp.dot(h, w_ref[...], preferred_element_type=F32)
        u_prev = pltpu.roll(u, 1, axis=0)[halo:halo + tm] * m_prev
        u_next = pltpu.roll(u, rows_all - 1, axis=0)[halo:halo + tm] * m_next
        return (u_prev * cw_ref[0:1, :] + u[halo:halo + tm] * cw_ref[1:2, :]
                + u_next * cw_ref[2:3, :] + cb_ref[...])

    gate = conv_half(wg_ref, cwg_ref, cbg_ref)
    val = conv_half(wv_ref, cwv_ref, cbv_ref)
    a = (jax.nn.gelu(gate, approximate=True) * val).astype(BF16)
    acc_ref[...] += jnp.dot(a, wd_ref[...], preferred_element_type=F32)

    @pl.when(f == pl.num_programs(1) - 1)
    def _():
        o_ref[...] = x_ref[...] + gt_ref[0] * (_rms(acc_ref[...]) * gpost_ref[...])


def _ffn(x, mod3, g_pre, g_post, w_up, conv_w, conv_b, w_down, layer, mod_row, *, seq):
    t, d = x.shape
    tm, tf = TM_MIX, TF_FFN
    halo = BF16_SUBLANES
    hb = tm // halo
    last_hb = t // halo - 1
    nf = w_down.shape[1] // tf
    prev = lambda i, f: (jnp.maximum(i * hb - 1, 0), 0)
    nxt = lambda i, f: (jnp.minimum((i + 1) * hb, last_hb), 0)

    def mod_idx(m):
        return lambda i, f: (mod_row(i * tm) * N_MOD + m, 0, 0)

    kern = functools.partial(_ffn_kernel, seq=seq)
    return pl.pallas_call(
        kern,
        grid=(t // tm, nf),
        in_specs=[
            pl.BlockSpec((tm, d), lambda i, f: (i, 0)),
            pl.BlockSpec((halo, d), prev),
            pl.BlockSpec((halo, d), nxt),
            pl.BlockSpec((1, 1, d), mod_idx(4)),
            pl.BlockSpec((1, 1, d), mod_idx(3)),
            pl.BlockSpec((1, 1, d), mod_idx(5)),
            pl.BlockSpec((1, d), lambda i, f: (0, 0)),
            pl.BlockSpec((1, d), lambda i, f: (0, 0)),
            pl.BlockSpec((None, d, tf), lambda i, f: (layer, 0, f)),
            pl.BlockSpec((None, d, tf), lambda i, f: (layer, 0, f)),
            pl.BlockSpec((None, 3, tf), lambda i, f: (layer, 0, f)),
            pl.BlockSpec((None, 3, tf), lambda i, f: (layer, 0, f)),
            pl.BlockSpec((None, 1, tf), lambda i, f: (layer, 0, f)),
            pl.BlockSpec((None, 1, tf), lambda i, f: (layer, 0, f)),
            pl.BlockSpec((None, tf, d), lambda i, f: (layer, f, 0)),
        ],
        out_specs=pl.BlockSpec((tm, d), lambda i, f: (i, 0)),
        out_shape=jax.ShapeDtypeStruct((t, d), F32),
        scratch_shapes=[pltpu.VMEM((tm + 2 * halo, d), BF16), pltpu.VMEM((tm, d), F32)],
        compiler_params=_cparams(2),
        name="ffn",
    )(x, x, x, mod3, mod3, mod3, g_pre, g_post, *w_up, *conv_w, *conv_b, w_down)


def _rope_tables(seq):
    rows = seq // GRID_W
    row = jnp.repeat(jnp.arange(rows), GRID_W)
    col = jnp.tile(jnp.arange(GRID_W), rows)
    inv = ROPE_BASE ** (-jnp.arange(ROPE_FREQS, dtype=F32) / ROPE_FREQS)
    ang = jnp.stack([row, col], axis=-1).astype(F32)[:, :, None] * inv
    cos, sin = jnp.cos(ang), jnp.sin(ang)
    cos_t = jnp.concatenate([cos[:, 0], cos[:, 0], cos[:, 1], cos[:, 1]], axis=-1)
    sin_t = jnp.concatenate([-sin[:, 0], sin[:, 0], -sin[:, 1], sin[:, 1]], axis=-1)
    return cos_t, sin_t


def _pad_axis(a, axis, size):
    pad = [(0, 0)] * a.ndim
    pad[axis] = (0, size - a.shape[axis])
    return jnp.pad(a, pad)


def kernel(x_prompt, x_sample, cache_k, cache_v, c, c_ctx, w_mod, b_mod, g_pre_mix, g_post_mix,
           g_pre_ffn, g_post_ffn, w_in, pool_w, pool_scale, lambda_q1, lambda_k1, lambda_q2, lambda_k2,
           g_subln, conv_w, conv_b, w_br_pool, w_br_attn, w_br_conv, w_out, ffn_w_up, ffn_conv_w,
           ffn_conv_b, ffn_w_down):
    batch, seq, d = x_prompt.shape
    dec_batch, dec_seq, _ = x_sample.shape
    depth = w_in.shape[0]
    past = cache_k.shape[2]
    qk_w = N_HEADS * 2 * HEAD_DIM
    v_w = N_HEADS * V_DIM
    pool_cols = pool_scale.shape[1]
    conv_cols = conv_b.shape[1]
    d_ff = ffn_w_down.shape[1]
    d_ffp = -(-d_ff // TF_FFN) * TF_FFN
    assert qk_w == v_w and dec_batch + 1 <= MOD_ROWS

    col0 = {"pool": 3 * qk_w, "cb": 3 * qk_w + pool_cols, "ccx": 3 * qk_w + pool_cols + conv_cols,
            "g": 3 * qk_w + pool_cols + 3 * conv_cols}
    w_br = jnp.concatenate([w_br_pool, w_br_attn, w_br_conv], axis=1).astype(BF16)
    w_out_b = w_out.astype(BF16)
    pool_w_b = pool_w.astype(BF16)

    def halves(a, axis):
        return (lax.slice_in_dim(a, 0, d_ff, axis=axis), lax.slice_in_dim(a, d_ff, 2 * d_ff, axis=axis))

    w_up = tuple(_pad_axis(h.astype(BF16), 2, d_ffp) for h in halves(ffn_w_up, 2))
    fcw = tuple(_pad_axis(h, 2, d_ffp) for h in halves(ffn_conv_w, 2))
    fcb = tuple(_pad_axis(h, 1, d_ffp)[:, None, :] for h in halves(ffn_conv_b, 1))
    w_down = _pad_axis(ffn_w_down.astype(BF16), 1, d_ffp)

    cvec = _pad_axis(jnp.concatenate([c_ctx[None, :], c], axis=0), 0, MOD_ROWS)
    mod = _modulation(cvec, w_mod, b_mod)
    mod3 = mod.reshape(depth * MOD_ROWS * N_MOD, 1, d)

    rope_tabs = _rope_tables(dec_seq)
    ck = cache_k.reshape(dec_batch, depth, past, qk_w)
    cv = cache_v.reshape(dec_batch, depth, past, v_w)
    row1 = lambda a: a[:, None, :]
    lam_args = (row1(lambda_q1), row1(lambda_k1), row1(lambda_q2), row1(lambda_k2), row1(g_subln))
    pool_scale3, conv_b3 = row1(pool_scale), row1(conv_b)

    xc = x_prompt.reshape(batch * seq, d)
    xl = x_sample.reshape(dec_batch * dec_seq, d)
    new_k, new_v = [], []
    for l in range(depth):
        lam_init = 0.8 - 0.6 * math.exp(-0.3 * l)
        ctx_row = lambda r, l=l: l * MOD_ROWS
        lat_row = lambda r, l=l: l * MOD_ROWS + 1 + r // dec_seq
        gl = lambda a: a[l][None, :]
        mix = functools.partial(_branches, w_br=w_br, pool_w=pool_w_b, pool_scale=pool_scale3,
                                conv_w=conv_w, conv_b=conv_b3, layer=l, d=d, pool_w_cols=pool_cols,
                                conv_cols=conv_cols, col0=col0)
        ffn = functools.partial(_ffn, w_up=w_up, conv_w=fcw, conv_b=fcb, w_down=w_down, layer=l)

        z, k32, v32 = _in_proj(xc, mod3, gl(g_pre_mix), w_in, l, ctx_row, qk_w=qk_w, pool_cols=pool_cols,
                               kv_out=True)
        new_k.append(k32.reshape(batch, seq, N_HEADS, 2, HEAD_DIM))
        new_v.append(v32.reshape(batch, seq, N_HEADS, V_DIM))
        ya = _attention_ctx(z, lam_args, l, lam_init, seq=seq, width=v_w)
        merged = mix(z, ya, seq=seq)
        xc = _out_proj(merged, w_out_b, xc, mod3, gl(g_post_mix), l, ctx_row)
        xc = ffn(xc, mod3, gl(g_pre_ffn), gl(g_post_ffn), mod_row=ctx_row, seq=seq)

        z = _in_proj(xl, mod3, gl(g_pre_mix), w_in, l, lat_row, qk_w=qk_w, pool_cols=pool_cols,
                     rope_tabs=rope_tabs)[0]
        ya = _attention_lat(z, ck, cv, lam_args, l, lam_init, seq=dec_seq, width=v_w)
        merged = mix(z, ya, seq=dec_seq)
        xl = _out_proj(merged, w_out_b, xl, mod3, gl(g_post_mix), l, lat_row)
        xl = ffn(xl, mod3, gl(g_pre_ffn), gl(g_post_ffn), mod_row=lat_row, seq=dec_seq)

    return (xc.reshape(batch, seq, d), xl.reshape(dec_batch, dec_seq, d),
            jnp.stack(new_k, axis=1), jnp.stack(new_v, axis=1))
```

```python
import functools
import math

import jax
import jax.numpy as jnp
from jax import lax
from jax.experimental import pallas as pl
from jax.experimental.pallas import tpu as pltpu

F32 = jnp.float32
BF16 = jnp.bfloat16

GRID_W = 64
N_HEADS = 8
HEAD_DIM = 128
V_DIM = 2 * HEAD_DIM
POOL_WINDOWS = (2, 4, 8, 16)
ROPE_FREQS = HEAD_DIM // 4
ROPE_BASE = 10000.0
EPS = 1e-6
Q_SCALE = HEAD_DIM ** -0.5 * math.log2(math.e)
N_MOD = 6
N_BRANCH = 3

LANES = 128
BF16_SUBLANES = 16
VMEM_LIMIT_BYTES = 56 * 1024 * 1024
MOD_ROWS = 8

TM_IN = 1024
TN_IN = 1024
TM_MIX = 512
POOL_SUB = 256
TQ_LAT = 512
TF_FFN = 512
KEY_CHUNK = 512
TN_MOD = 1024
ZERO_ROWS = 128


def _cparams(n_axes):
    return pltpu.CompilerParams(dimension_semantics=("arbitrary",) * n_axes,
                                vmem_limit_bytes=VMEM_LIMIT_BYTES)


def _rms(x):
    return x * lax.rsqrt(jnp.mean(x * x, axis=-1, keepdims=True) + EPS)


def _modulated_norm(x, gain_ref, sc_ref, sh_ref):
    return (_rms(x) * (gain_ref[...] * (1.0 + sc_ref[0])) + sh_ref[0]).astype(BF16)


def _mod_kernel(c_ref, w_ref, b_ref, o_ref):
    c = c_ref[...]
    s = (c * jax.nn.sigmoid(c)).astype(BF16)
    o_ref[...] = jnp.dot(s, w_ref[...].astype(BF16), preferred_element_type=F32) + b_ref[...]


def _modulation(cvec, w_mod, b_mod):
    depth, d, n = w_mod.shape
    return pl.pallas_call(
        _mod_kernel,
        grid=(depth, n // TN_MOD),
        in_specs=[
            pl.BlockSpec((MOD_ROWS, d), lambda l, j: (0, 0)),
            pl.BlockSpec((None, d, TN_MOD), lambda l, j: (l, 0, j)),
            pl.BlockSpec((None, 1, TN_MOD), lambda l, j: (l, 0, j)),
        ],
        out_specs=pl.BlockSpec((None, MOD_ROWS, TN_MOD), lambda l, j: (l, 0, j)),
        out_shape=jax.ShapeDtypeStruct((depth, MOD_ROWS, n), F32),
        compiler_params=_cparams(2),
        name="modulation",
    )(cvec, w_mod, b_mod.reshape(depth, 1, n))


def _rope_tile(acc, c, s):
    lane = lax.broadcasted_iota(jnp.int32, c.shape, 1)
    upper = (lane & ROPE_FREQS) != 0
    outs = []
    for g in range(acc.shape[1] // LANES):
        xg = acc[:, g * LANES:(g + 1) * LANES]
        partner = jnp.where(upper, pltpu.roll(xg, ROPE_FREQS, axis=1),
                            pltpu.roll(xg, LANES - ROPE_FREQS, axis=1))
        outs.append(xg * c + partner * s)
    return jnp.concatenate(outs, axis=1)


def _prenorm_kernel(x_ref, g_ref, sc_ref, sh_ref, h_ref):
    h_ref[...] = _modulated_norm(x_ref[...], g_ref, sc_ref, sh_ref)


def _prenorm(x, mod3, g_pre, mod_row):
    t, d = x.shape
    tm = TM_MIX
    return pl.pallas_call(
        _prenorm_kernel,
        grid=(t // tm,),
        in_specs=[
            pl.BlockSpec((tm, d), lambda i: (i, 0)),
            pl.BlockSpec((1, d), lambda i: (0, 0)),
            pl.BlockSpec((1, 1, d), lambda i: (mod_row(i * tm) * N_MOD + 1, 0, 0)),
            pl.BlockSpec((1, 1, d), lambda i: (mod_row(i * tm) * N_MOD + 0, 0, 0)),
        ],
        out_specs=pl.BlockSpec((tm, d), lambda i: (i, 0)),
        out_shape=jax.ShapeDtypeStruct((t, d), BF16),
        compiler_params=_cparams(1),
        name="prenorm",
    )(x, g_pre, mod3, mod3)


def _in_proj_kernel(*refs, rope, kv_out, q_tiles, qk_tiles, k_tiles, v_tiles):
    h_ref, w_ref = refs[:2]
    pos = 2
    if rope:
        cos_ref, sin_ref = refs[pos:pos + 2]
        pos += 2
    z_ref = refs[pos]
    pos += 1
    if kv_out:
        k_ref, v_ref = refs[pos:pos + 2]
    j = pl.program_id(1)

    acc = jnp.dot(h_ref[...], w_ref[...].astype(BF16), preferred_element_type=F32)

    def rotated(a):
        return _rope_tile(a, cos_ref[...], sin_ref[...]) if rope else a

    @pl.when(j < q_tiles)
    def _():
        z_ref[...] = rotated(acc * Q_SCALE).astype(BF16)

    @pl.when((j >= q_tiles) & (j < qk_tiles))
    def _():
        z_ref[...] = rotated(acc).astype(BF16)

    @pl.when(j >= qk_tiles)
    def _():
        z_ref[...] = acc.astype(BF16)

    if kv_out:
        @pl.when((j >= k_tiles[0]) & (j < k_tiles[1]))
        def _():
            k_ref[...] = acc

        @pl.when((j >= v_tiles[0]) & (j < v_tiles[1]))
        def _():
            v_ref[...] = acc


def _in_proj(h, w_in, layer, *, qk_w, pool_cols, rope_tabs=None, kv_out=False):
    t, d = h.shape
    n = w_in.shape[2]
    tm, tn = TM_IN, TN_IN
    nq = qk_w // tn
    n_pool = pool_cols // tn
    rope = rope_tabs is not None

    def w_tile(j):
        return jnp.where(j < 3 * nq, j + n_pool, jnp.where(j < 3 * nq + n_pool, j - 3 * nq, j))

    in_specs = [
        pl.BlockSpec((tm, d), lambda i, j: (i, 0)),
        pl.BlockSpec((None, d, tn), lambda i, j: (layer, 0, w_tile(j))),
    ]
    args = [h, w_in]
    if rope:
        tab_blocks = rope_tabs[0].shape[0] // tm
        tab_spec = pl.BlockSpec((tm, LANES), lambda i, j: (i % tab_blocks, 0))
        in_specs += [tab_spec, tab_spec]
        args += list(rope_tabs)
    out_specs = [pl.BlockSpec((tm, tn), lambda i, j: (i, j))]
    out_shape = [jax.ShapeDtypeStruct((t, n), BF16)]
    if kv_out:
        out_specs += [
            pl.BlockSpec((tm, tn), lambda i, j: (i, jnp.clip(j - nq, 0, nq - 1))),
            pl.BlockSpec((tm, tn), lambda i, j: (i, jnp.clip(j - 2 * nq, 0, nq - 1))),
        ]
        out_shape += [jax.ShapeDtypeStruct((t, qk_w), F32)] * 2
    kern = functools.partial(_in_proj_kernel, rope=rope, kv_out=kv_out, q_tiles=nq, qk_tiles=2 * nq,
                             k_tiles=(nq, 2 * nq), v_tiles=(2 * nq, 3 * nq))
    return pl.pallas_call(
        kern,
        grid=(t // tm, n // tn),
        in_specs=in_specs,
        out_specs=out_specs,
        out_shape=out_shape,
        compiler_params=_cparams(2),
        name="in_proj_rope" if rope else "in_proj_kv",
    )(*args)


def _lambda(lq1, lk1, lq2, lk2, lam_init):
    return (jnp.exp(jnp.sum(lq1 * lk1, axis=-1, keepdims=True))
            - jnp.exp(jnp.sum(lq2 * lk2, axis=-1, keepdims=True)) + lam_init)


def _diff_attn_head(q, ks, vs, lam, gain, lam_init):
    def softmax_parts(m):
        cols = slice(m * HEAD_DIM, (m + 1) * HEAD_DIM)
        ss = [lax.dot_general(q[:, cols], k[:, cols], (((1,), (1,)), ((), ())),
                              preferred_element_type=F32) for k in ks]
        mx = functools.reduce(jnp.maximum, [jnp.max(s, axis=-1, keepdims=True) for s in ss])
        es = [jnp.exp2(s - mx) for s in ss]
        den = functools.reduce(jnp.add, [jnp.sum(e, axis=-1, keepdims=True) for e in es])
        return es, 1.0 / den

    e1, inv1 = softmax_parts(0)
    e2, inv2 = softmax_parts(1)
    w2 = lam * inv2
    o = functools.reduce(jnp.add, [
        jnp.dot((a * inv1 - b * w2).astype(BF16), v, preferred_element_type=F32)
        for a, b, v in zip(e1, e2, vs)])
    return _rms(o) * gain * (1.0 - lam_init)


def _attn_ctx_kernel(q_ref, k_ref, v_ref, lq1, lk1, lq2, lk2, g_ref, o_ref, *, lam_init):
    lam = _lambda(lq1[...], lk1[...], lq2[...], lk2[...], lam_init)
    gain = g_ref[...]
    for h in range(N_HEADS):
        cols = slice(h * V_DIM, (h + 1) * V_DIM)
        o = _diff_attn_head(q_ref[:, cols], [k_ref[:, cols]], [v_ref[:, cols]], lam, gain, lam_init)
        o_ref[:, cols] = o.astype(BF16)


def _attn_lat_kernel(q_ref, k_ref, v_ref, ck_ref, cv_ref, lq1, lk1, lq2, lk2, g_ref, o_ref, *,
                     lam_init):
    lam = _lambda(lq1[...], lk1[...], lq2[...], lk2[...], lam_init)
    q = q_ref[...]
    chunks = [(ck_ref[...].astype(BF16), cv_ref[...].astype(BF16))]
    for c in range(k_ref.shape[0] // KEY_CHUNK):
        rows = slice(c * KEY_CHUNK, (c + 1) * KEY_CHUNK)
        chunks.append((k_ref[rows, :], v_ref[rows, :]))
    outs = []
    for m in range(2):
        cols = slice(m * HEAD_DIM, (m + 1) * HEAD_DIM)
        mx = den = acc = None
        for kc, vc in chunks:
            s = lax.dot_general(q[:, cols], kc[:, cols], (((1,), (1,)), ((), ())),
                                preferred_element_type=F32)
            s_max = jnp.max(s, axis=-1, keepdims=True)
            if mx is None:
                mx = s_max
                e = jnp.exp2(s - mx)
                den = jnp.sum(e, axis=-1, keepdims=True)
                acc = jnp.dot(e.astype(BF16), vc, preferred_element_type=F32)
            else:
                mx_new = jnp.maximum(mx, s_max)
                alpha = jnp.exp2(mx - mx_new)
                e = jnp.exp2(s - mx_new)
                den = alpha * den + jnp.sum(e, axis=-1, keepdims=True)
                acc = alpha * acc + jnp.dot(e.astype(BF16), vc, preferred_element_type=F32)
                mx = mx_new
        outs.append(acc * (1.0 / den))
    o = outs[0] - lam * outs[1]
    o_ref[...] = (_rms(o) * g_ref[...] * (1.0 - lam_init)).astype(BF16)


def _lam_specs(layer, n_axes):
    idx = (lambda b: (layer, 0, 0)) if n_axes == 1 else (lambda b, h, qi: (layer, 0, 0))
    return [pl.BlockSpec((None, 1, HEAD_DIM), idx)] * 4 + [pl.BlockSpec((None, 1, V_DIM), idx)]


def _attention_ctx(z, lam_args, layer, lam_init, *, seq, width):
    t = z.shape[0]
    kern = functools.partial(_attn_ctx_kernel, lam_init=lam_init)
    return pl.pallas_call(
        kern,
        grid=(t // seq,),
        in_specs=[pl.BlockSpec((seq, width), lambda b: (b, 0)),
                  pl.BlockSpec((seq, width), lambda b: (b, 1)),
                  pl.BlockSpec((seq, width), lambda b: (b, 2))] + _lam_specs(layer, 1),
        out_specs=pl.BlockSpec((seq, width), lambda b: (b, 0)),
        out_shape=jax.ShapeDtypeStruct((t, width), BF16),
        compiler_params=_cparams(1),
        name="attention_ctx",
    )(z, z, z, *lam_args)


def _attention_lat(z, cache_k, cache_v, lam_args, layer, lam_init, *, seq, width):
    t = z.shape[0]
    nb = t // seq
    nq = seq // TQ_LAT
    past = cache_k.shape[2]
    kern = functools.partial(_attn_lat_kernel, lam_init=lam_init)
    return pl.pallas_call(
        kern,
        grid=(nb, N_HEADS, nq),
        in_specs=[
            pl.BlockSpec((TQ_LAT, V_DIM), lambda b, h, qi: (b * nq + qi, h)),
            pl.BlockSpec((seq, V_DIM), lambda b, h, qi: (b, N_HEADS + h)),
            pl.BlockSpec((seq, V_DIM), lambda b, h, qi: (b, 2 * N_HEADS + h)),
            pl.BlockSpec((None, None, past, V_DIM), lambda b, h, qi: (b, layer, 0, h)),
            pl.BlockSpec((None, None, past, V_DIM), lambda b, h, qi: (b, layer, 0, h)),
        ] + _lam_specs(layer, 3),
        out_specs=pl.BlockSpec((TQ_LAT, V_DIM), lambda b, h, qi: (b * nq + qi, h)),
        out_shape=jax.ShapeDtypeStruct((t, width), BF16),
        compiler_params=_cparams(3),
        name="attention_lat",
    )(z, z, z, cache_k, cache_v, *lam_args)


def _seq_masks(row0, rows, seq):
    p = (row0 + lax.broadcasted_iota(jnp.int32, (rows, 1), 0)) % seq
    return (p != 0).astype(F32), (p != seq - 1).astype(F32)


def _pool_mixer(u_ref, upad_ref, pw_ref, ps_ref, lhs_ref, row0, seq):
    halo = BF16_SUBLANES
    tm = u_ref.shape[0]
    pg = pw_ref.shape[-1]
    span = POOL_SUB + 2 * halo
    t_idx = lax.broadcasted_iota(jnp.int32, (POOL_SUB, span), 0)
    c_idx = lax.broadcasted_iota(jnp.int32, (POOL_SUB, span), 1)
    off = c_idx - halo - t_idx
    t_col = lax.broadcasted_iota(jnp.int32, (POOL_SUB, 1), 0)
    for r in range(tm // POOL_SUB):
        pos0 = (row0 + r * POOL_SUB) % seq
        pos_c = pos0 + c_idx - halo
        in_seq = (pos_c >= 0) & (pos_c < seq)
        pos_t = pos0 + t_col
        for gi, w in enumerate(POOL_WINDOWS):
            half = w // 2
            band = ((off >= -half) & (off < w - half) & in_seq).astype(F32).astype(BF16)
            cnt = jnp.minimum(pos_t + (w - half), seq) - jnp.maximum(pos_t - half, 0)
            cols = slice(gi * pg, (gi + 1) * pg)
            win = jnp.dot(band, upad_ref[r * POOL_SUB:r * POOL_SUB + span, cols],
                          preferred_element_type=F32)
            mean = win / cnt.astype(F32)
            dlt = mean - u_ref[r * POOL_SUB:(r + 1) * POOL_SUB, cols].astype(F32)
            y = jnp.dot(dlt.astype(BF16), pw_ref[gi], preferred_element_type=F32) * ps_ref[:, cols]
            lhs_ref[r * POOL_SUB:(r + 1) * POOL_SUB, cols] = y.astype(BF16)


def _branches_kernel(u_ref, up_ref, un_ref, cb_ref, ccx_ref, ccxp_ref, ccxn_ref, ya_ref, g_ref,
                     w_ref, pw_ref, ps_ref, cw_ref, cbias_ref, o_ref,
                     acc_ref, lhs_ref, upad_ref, ppad_ref, *, seq):
    i = pl.program_id(0)
    s = pl.program_id(1)
    tm = u_ref.shape[0]
    halo = BF16_SUBLANES
    cwid = cb_ref.shape[1]
    row0 = i * tm

    def gated(lhs):
        gate = jax.nn.sigmoid(g_ref[...].astype(F32))
        return gate * jnp.dot(lhs, w_ref[...], preferred_element_type=F32)

    @pl.when(s == 0)
    def _():
        upad_ref[0:halo, :] = up_ref[...]
        upad_ref[halo:halo + tm, :] = u_ref[...]
        upad_ref[halo + tm:, :] = un_ref[...]
        _pool_mixer(u_ref, upad_ref, pw_ref, ps_ref, lhs_ref, row0, seq)
        acc_ref[...] = gated(lhs_ref[...])

    @pl.when((s == 1) | (s == 2))
    def _():
        acc_ref[...] += gated(ya_ref[...])

    @pl.when(s == 3)
    def _():
        def prod(ref):
            v = ref[...].astype(F32)
            return v[:, :cwid] * v[:, cwid:]

        ppad_ref[0:halo, :] = prod(ccxp_ref)
        ppad_ref[halo:halo + tm, :] = prod(ccx_ref)
        ppad_ref[halo + tm:, :] = prod(ccxn_ref)
        m_prev, m_next = _seq_masks(row0, tm, seq)
        p_prev = ppad_ref[halo - 1:halo - 1 + tm, :] * m_prev
        p_next = ppad_ref[halo + 1:halo + 1 + tm, :] * m_next
        conv = (p_prev * cw_ref[0:1, :] + ppad_ref[halo:halo + tm, :] * cw_ref[1:2, :]
                + p_next * cw_ref[2:3, :] + cbias_ref[...])
        y = cb_ref[...].astype(F32) * conv
        o_ref[...] = (acc_ref[...] + gated(y.astype(BF16))).astype(BF16)


def _branches(z, y_attn, w_br, pool_w, pool_scale, conv_w, conv_b, layer, *, seq, d, pool_w_cols,
              conv_cols, col0):
    t = z.shape[0]
    tm = TM_MIX
    halo = BF16_SUBLANES
    hb = tm // halo
    last_hb = t // halo - 1
    kc = w_br.shape[1] // 4
    prev = lambda i, s: jnp.maximum(i * hb - 1, 0)
    nxt = lambda i, s: jnp.minimum((i + 1) * hb, last_hb)
    pool_blk = col0["pool"] // pool_w_cols
    cb_blk = col0["cb"] // conv_cols
    ccx_blk = col0["ccx"] // (2 * conv_cols)
    g_blk = col0["g"] // d
    kern = functools.partial(_branches_kernel, seq=seq)
    return pl.pallas_call(
        kern,
        grid=(t // tm, 4),
        in_specs=[
            pl.BlockSpec((tm, pool_w_cols), lambda i, s: (i, pool_blk)),
            pl.BlockSpec((halo, pool_w_cols), lambda i, s: (prev(i, s), pool_blk)),
            pl.BlockSpec((halo, pool_w_cols), lambda i, s: (nxt(i, s), pool_blk)),
            pl.BlockSpec((tm, conv_cols), lambda i, s: (i, cb_blk)),
            pl.BlockSpec((tm, 2 * conv_cols), lambda i, s: (i, ccx_blk)),
            pl.BlockSpec((halo, 2 * conv_cols), lambda i, s: (prev(i, s), ccx_blk)),
            pl.BlockSpec((halo, 2 * conv_cols), lambda i, s: (nxt(i, s), ccx_blk)),
            pl.BlockSpec((tm, kc), lambda i, s: (i, jnp.clip(s - 1, 0, 1))),
            pl.BlockSpec((tm, d), lambda i, s: (i, g_blk + (s + 1) // 2)),
            pl.BlockSpec((None, kc, d), lambda i, s: (layer, s, 0)),
            pl.BlockSpec((None,) + pool_w.shape[1:], lambda i, s: (layer, 0, 0, 0)),
            pl.BlockSpec((None, 1, pool_w_cols), lambda i, s: (layer, 0, 0)),
            pl.BlockSpec((None, 3, conv_cols), lambda i, s: (layer, 0, 0)),
            pl.BlockSpec((None, 1, conv_cols), lambda i, s: (layer, 0, 0)),
        ],
        out_specs=pl.BlockSpec((tm, d), lambda i, s: (i, 0)),
        out_shape=jax.ShapeDtypeStruct((t, d), BF16),
        scratch_shapes=[
            pltpu.VMEM((tm, d), F32),
            pltpu.VMEM((tm, pool_w_cols), BF16),
            pltpu.VMEM((tm + 2 * halo, pool_w_cols), BF16),
            pltpu.VMEM((tm + 2 * halo, conv_cols), F32),
        ],
        compiler_params=_cparams(2),
        name="branches",
    )(z, z, z, z, z, z, z, y_attn, z, w_br, pool_w, pool_scale, conv_w, conv_b)


def _out_proj_kernel(m_ref, w_ref, x_ref, gt_ref, g_ref, gn_ref, scn_ref, shn_ref, o_ref, h_ref):
    m = jnp.dot(m_ref[...], w_ref[...], preferred_element_type=F32)
    x1 = x_ref[...] + _rms(m) * (gt_ref[0] * g_ref[...])
    o_ref[...] = x1
    h_ref[...] = _modulated_norm(x1, gn_ref, scn_ref, shn_ref)


def _out_proj(merged, w_out, x, mod3, g_post, g_pre_ffn, layer, mod_row):
    t, d = x.shape
    tm = TM_MIX
    mod_idx = lambda m: (lambda i: (mod_row(i * tm) * N_MOD + m, 0, 0))
    row = pl.BlockSpec((tm, d), lambda i: (i, 0))
    vec = pl.BlockSpec((1, d), lambda i: (0, 0))
    return pl.pallas_call(
        _out_proj_kernel,
        grid=(t // tm,),
        in_specs=[row, pl.BlockSpec((None, d, d), lambda i: (layer, 0, 0)), row,
                  pl.BlockSpec((1, 1, d), mod_idx(2)), vec, vec,
                  pl.BlockSpec((1, 1, d), mod_idx(4)), pl.BlockSpec((1, 1, d), mod_idx(3))],
        out_specs=[row, row],
        out_shape=[jax.ShapeDtypeStruct((t, d), F32), jax.ShapeDtypeStruct((t, d), BF16)],
        compiler_params=_cparams(1),
        name="out_proj",
    )(merged, w_out, x, mod3, g_post, g_pre_ffn, mod3, mod3)


def _gelu_tanh(x):
    c = math.sqrt(2.0 / math.pi)
    hx = 0.5 * x
    return hx * jnp.tanh(x * (c + (c * 0.044715) * (x * x))) + hx


def _ffn_layout(tm, seq):
    halo = BF16_SUBLANES
    pad = 8
    if seq > tm:
        assert seq % tm == 0 and (tm + 2 * halo) % (2 * halo) == 0
        rows = tm + 2 * halo
        return rows, rows, [(0, rows // 2, 0), (rows // 2, rows // 2, rows // 2)], [(halo, tm)], []
    assert tm % seq == 0
    nseg = tm // seq
    pieces = [(s * seq, seq, pad + s * (seq + pad)) for s in range(nseg)]
    outs = [(pad + s * (seq + pad), seq) for s in range(nseg)]
    zero_rows = [(s * (seq + pad), pad) for s in range(nseg + 1)]
    return tm, pad + nseg * (seq + pad), pieces, outs, zero_rows


def _ffn_kernel(*refs, seq, with_halo, emit_next):
    refs = list(refs)
    h_ref = refs.pop(0)
    hp_ref, hn_ref = (refs.pop(0), refs.pop(0)) if with_halo else (None, None)
    x_ref, gt_ref, gpost_ref = refs.pop(0), refs.pop(0), refs.pop(0)
    next_refs = (refs.pop(0), refs.pop(0), refs.pop(0)) if emit_next else None
    wg_ref, wv_ref, cwg_ref, cwv_ref, cbg_ref, cbv_ref, wd_ref = refs[:7]
    refs = refs[7:]
    o_ref = refs.pop(0)
    hnext_ref = refs.pop(0) if emit_next else None
    lhs_ref = refs.pop(0) if with_halo else h_ref
    acc_ref, u_ref = refs
    i = pl.program_id(0)
    f = pl.program_id(1)
    tm = x_ref.shape[0]
    halo = BF16_SUBLANES
    _, _, pieces, outs, zero_rows = _ffn_layout(tm, seq)

    @pl.when(f == 0)
    def _():
        if with_halo:
            has_prev = (i * tm) % seq != 0
            has_next = ((i + 1) * tm) % seq != 0
            lhs_ref[0:halo, :] = jnp.where(has_prev, hp_ref[...], jnp.zeros_like(hp_ref))
            lhs_ref[halo:halo + tm, :] = h_ref[...]
            lhs_ref[halo + tm:, :] = jnp.where(has_next, hn_ref[...], jnp.zeros_like(hn_ref))

        def zero_rows_of_acc(r, carry):
            acc_ref[pl.ds(pl.multiple_of(r * ZERO_ROWS, ZERO_ROWS), ZERO_ROWS), :] = jnp.zeros(
                (ZERO_ROWS, acc_ref.shape[1]), F32)
            return carry

        lax.fori_loop(0, tm // ZERO_ROWS, zero_rows_of_acc, 0)
        for start, rows in zero_rows:
            u_ref[:, start:start + rows, :] = jnp.zeros((2, rows, u_ref.shape[2]), F32)

    def conv_half(slot, w_ref, cw_ref, cb_ref):
        for lhs0, rows, u0 in pieces:
            u_ref[slot, u0:u0 + rows, :] = jnp.dot(lhs_ref[lhs0:lhs0 + rows, :], w_ref[...],
                                                   preferred_element_type=F32)
        return jnp.concatenate([
            u_ref[slot, u0 - 1:u0 - 1 + rows, :] * cw_ref[0:1, :] + u_ref[slot, u0:u0 + rows, :] * cw_ref[1:2, :]
            + u_ref[slot, u0 + 1:u0 + 1 + rows, :] * cw_ref[2:3, :] + cb_ref[...]
            for u0, rows in outs], axis=0)

    gate = conv_half(0, wg_ref, cwg_ref, cbg_ref)
    val = conv_half(1, wv_ref, cwv_ref, cbv_ref)
    a = (_gelu_tanh(gate) * val).astype(BF16)
    acc_ref[...] += jnp.dot(a, wd_ref[...], preferred_element_type=F32)

    @pl.when(f == pl.num_programs(1) - 1)
    def _():
        x2 = x_ref[...] + _rms(acc_ref[...]) * (gt_ref[0] * gpost_ref[...])
        o_ref[...] = x2
        if emit_next:
            hnext_ref[...] = _modulated_norm(x2, *next_refs)


def _ffn(h, x, mod3, g_post, w_up, conv_w, conv_b, w_down, layer, mod_row, *, seq, next_norm=None):
    t, d = x.shape
    tm, tf = TM_MIX, TF_FFN
    halo = BF16_SUBLANES
    hb = tm // halo
    last_hb = t // halo - 1
    nf = w_down.shape[1] // tf
    with_halo = seq > tm
    emit_next = next_norm is not None
    lhs_rows, u_rows, _, _, _ = _ffn_layout(tm, seq)
    row = pl.BlockSpec((tm, d), lambda i, f: (i, 0))
    vec = pl.BlockSpec((1, d), lambda i, f: (0, 0))

    def mod_spec(row_fn, m):
        return pl.BlockSpec((1, 1, d), lambda i, f: (row_fn(i * tm) * N_MOD + m, 0, 0))

    in_specs, args = [row], [h]
    if with_halo:
        in_specs += [pl.BlockSpec((halo, d), lambda i, f: (jnp.maximum(i * hb - 1, 0), 0)),
                     pl.BlockSpec((halo, d), lambda i, f: (jnp.minimum((i + 1) * hb, last_hb), 0))]
        args += [h, h]
    in_specs += [row, mod_spec(mod_row, 5), vec]
    args += [x, mod3, g_post]
    if emit_next:
        in_specs += [vec, mod_spec(next_norm[1], 1), mod_spec(next_norm[1], 0)]
        args += [next_norm[0], mod3, mod3]
    tile = lambda rows: pl.BlockSpec((None, rows, tf), lambda i, f: (layer, 0, f))
    in_specs += [tile(d), tile(d), tile(3), tile(3), tile(1), tile(1),
                 pl.BlockSpec((None, tf, d), lambda i, f: (layer, f, 0))]
    args += [*w_up, *conv_w, *conv_b, w_down]
    scratch = [pltpu.VMEM((lhs_rows, d), BF16)] if with_halo else []
    scratch += [pltpu.VMEM((tm, d), F32), pltpu.VMEM((2, u_rows, tf), F32)]
    kern = functools.partial(_ffn_kernel, seq=seq, with_halo=with_halo, emit_next=emit_next)
    return pl.pallas_call(
        kern,
        grid=(t // tm, nf),
        in_specs=in_specs,
        out_specs=[row, row] if emit_next else [row],
        out_shape=[jax.ShapeDtypeStruct((t, d), F32)] + ([jax.ShapeDtypeStruct((t, d), BF16)] if emit_next else []),
        scratch_shapes=scratch,
        compiler_params=_cparams(2),
        name="ffn_halo" if with_halo else "ffn",
    )(*args)


def _rope_tables(seq):
    rows = seq // GRID_W
    row = jnp.repeat(jnp.arange(rows), GRID_W)
    col = jnp.tile(jnp.arange(GRID_W), rows)
    inv = ROPE_BASE ** (-jnp.arange(ROPE_FREQS, dtype=F32) / ROPE_FREQS)
    ang = jnp.stack([row, col], axis=-1).astype(F32)[:, :, None] * inv
    cos, sin = jnp.cos(ang), jnp.sin(ang)
    cos_t = jnp.concatenate([cos[:, 0], cos[:, 0], cos[:, 1], cos[:, 1]], axis=-1)
    sin_t = jnp.concatenate([-sin[:, 0], sin[:, 0], -sin[:, 1], sin[:, 1]], axis=-1)
    return cos_t, sin_t


def _pad_axis(a, axis, size):
    pad = [(0, 0)] * a.ndim
    pad[axis] = (0, size - a.shape[axis])
    return jnp.pad(a, pad)


def kernel(x_prompt, x_sample, cache_k, cache_v, c, c_ctx, w_mod, b_mod, g_pre_mix, g_post_mix,
           g_pre_ffn, g_post_ffn, w_in, pool_w, pool_scale, lambda_q1, lambda_k1, lambda_q2, lambda_k2,
           g_subln, conv_w, conv_b, w_br_pool, w_br_attn, w_br_conv, w_out, ffn_w_up, ffn_conv_w,
           ffn_conv_b, ffn_w_down):
    batch, seq, d = x_prompt.shape
    dec_batch, dec_seq, _ = x_sample.shape
    depth = w_in.shape[0]
    past = cache_k.shape[2]
    qk_w = N_HEADS * 2 * HEAD_DIM
    v_w = N_HEADS * V_DIM
    pool_cols = pool_scale.shape[1]
    conv_cols = conv_b.shape[1]
    d_ff = ffn_w_down.shape[1]
    d_ffp = -(-d_ff // TF_FFN) * TF_FFN
    assert qk_w == v_w and dec_batch + 1 <= MOD_ROWS

    col0 = {"pool": 3 * qk_w, "cb": 3 * qk_w + pool_cols, "ccx": 3 * qk_w + pool_cols + conv_cols,
            "g": 3 * qk_w + pool_cols + 3 * conv_cols}
    w_br = jnp.concatenate([w_br_pool, w_br_attn, w_br_conv], axis=1).astype(BF16)
    w_out_b = w_out.astype(BF16)
    pool_w_b = pool_w.astype(BF16)

    def halves(a, axis):
        return (lax.slice_in_dim(a, 0, d_ff, axis=axis), lax.slice_in_dim(a, d_ff, 2 * d_ff, axis=axis))

    w_up = tuple(_pad_axis(h.astype(BF16), 2, d_ffp) for h in halves(ffn_w_up, 2))
    fcw = tuple(_pad_axis(h, 2, d_ffp) for h in halves(ffn_conv_w, 2))
    fcb = tuple(_pad_axis(h, 1, d_ffp)[:, None, :] for h in halves(ffn_conv_b, 1))
    w_down = _pad_axis(ffn_w_down.astype(BF16), 1, d_ffp)

    cvec = _pad_axis(jnp.concatenate([c_ctx[None, :], c], axis=0), 0, MOD_ROWS)
    mod = _modulation(cvec, w_mod, b_mod)
    mod3 = mod.reshape(depth * MOD_ROWS * N_MOD, 1, d)

    rope_tabs = _rope_tables(dec_seq)
    ck = cache_k.reshape(dec_batch, depth, past, qk_w)
    cv = cache_v.reshape(dec_batch, depth, past, v_w)
    row1 = lambda a: a[:, None, :]
    lam_args = (row1(lambda_q1), row1(lambda_k1), row1(lambda_q2), row1(lambda_k2), row1(g_subln))
    pool_scale3, conv_b3 = row1(pool_scale), row1(conv_b)

    xc = x_prompt.reshape(batch * seq, d)
    xl = x_sample.reshape(dec_batch * dec_seq, d)
    ctx_row = lambda l: (lambda r: l * MOD_ROWS)
    lat_row = lambda l: (lambda r: l * MOD_ROWS + 1 + r // dec_seq)
    gl = lambda a, l: a[l][None, :]
    hc = _prenorm(xc, mod3, gl(g_pre_mix, 0), ctx_row(0))
    hl = _prenorm(xl, mod3, gl(g_pre_mix, 0), lat_row(0))
    new_k, new_v = [], []
    for l in range(depth):
        lam_init = 0.8 - 0.6 * math.exp(-0.3 * l)
        last = l == depth - 1
        mix = functools.partial(_branches, w_br=w_br, pool_w=pool_w_b, pool_scale=pool_scale3,
                                conv_w=conv_w, conv_b=conv_b3, layer=l, d=d, pool_w_cols=pool_cols,
                                conv_cols=conv_cols, col0=col0)
        ffn = functools.partial(_ffn, mod3=mod3, g_post=gl(g_post_ffn, l), w_up=w_up, conv_w=fcw, conv_b=fcb,
                                w_down=w_down, layer=l)
        out_proj = functools.partial(_out_proj, w_out=w_out_b, mod3=mod3, g_post=gl(g_post_mix, l),
                                     g_pre_ffn=gl(g_pre_ffn, l), layer=l)
        next_norm = lambda row: None if last else (gl(g_pre_mix, l + 1), row(l + 1))

        z, k32, v32 = _in_proj(hc, w_in, l, qk_w=qk_w, pool_cols=pool_cols, kv_out=True)
        new_k.append(k32.reshape(batch, seq, N_HEADS, 2, HEAD_DIM))
        new_v.append(v32.reshape(batch, seq, N_HEADS, V_DIM))
        ya = _attention_ctx(z, lam_args, l, lam_init, seq=seq, width=v_w)
        merged = mix(z, ya, seq=seq)
        xc, h2 = out_proj(merged, x=xc, mod_row=ctx_row(l))
        res = ffn(h2, xc, mod_row=ctx_row(l), seq=seq, next_norm=next_norm(ctx_row))
        xc, hc = res if not last else (res[0], None)

        z = _in_proj(hl, w_in, l, qk_w=qk_w, pool_cols=pool_cols, rope_tabs=rope_tabs)[0]
        ya = _attention_lat(z, ck, cv, lam_args, l, lam_init, seq=dec_seq, width=v_w)
        merged = mix(z, ya, seq=dec_seq)
        xl, h2 = out_proj(merged, x=xl, mod_row=lat_row(l))
        res = ffn(h2, xl, mod_row=lat_row(l), seq=dec_seq, next_norm=next_norm(lat_row))
        xl, hl = res if not last else (res[0], None)

    return (xc.reshape(batch, seq, d), xl.reshape(dec_batch, dec_seq, d),
            jnp.stack(new_k, axis=1), jnp.stack(new_v, axis=1))
```

```python
import functools
import math

import jax
import jax.numpy as jnp
from jax import lax
from jax.experimental import pallas as pl
from jax.experimental.pallas import tpu as pltpu

F32 = jnp.float32
BF16 = jnp.bfloat16

GRID_W = 64
N_HEADS = 8
HEAD_DIM = 128
V_DIM = 2 * HEAD_DIM
POOL_WINDOWS = (2, 4, 8, 16)
ROPE_FREQS = HEAD_DIM // 4
ROPE_BASE = 10000.0
EPS = 1e-6
Q_SCALE = HEAD_DIM ** -0.5 * math.log2(math.e)
N_MOD = 6
N_BRANCH = 3

LANES = 128
BF16_SUBLANES = 16
VMEM_LIMIT_BYTES = 56 * 1024 * 1024
MOD_ROWS = 8

TM_IN = 1024
TN_IN = 1024
TM_MIX = 512
POOL_SUB = 256
TQ_LAT = 512
TF_FFN = 512
KEY_CHUNK = 512
TN_MOD = 1024
ZERO_ROWS = 128
CONV_PARAM_ROWS = 4
PREP_BLOCK = 256


def _cparams(n_axes):
    return pltpu.CompilerParams(dimension_semantics=("arbitrary",) * n_axes,
                                vmem_limit_bytes=VMEM_LIMIT_BYTES)


def _rms(x):
    return x * lax.rsqrt(jnp.mean(x * x, axis=-1, keepdims=True) + EPS)


def _modulated_norm(x, gain_ref, sc_ref, sh_ref):
    return (_rms(x) * (gain_ref[...] * (1.0 + sc_ref[0])) + sh_ref[0]).astype(BF16)


def _mod_kernel(c_ref, w_ref, b_ref, o_ref):
    c = c_ref[...]
    s = (c * jax.nn.sigmoid(c)).astype(BF16)
    o_ref[...] = jnp.dot(s, w_ref[...].astype(BF16), preferred_element_type=F32) + b_ref[...]


def _modulation(cvec, w_mod, b_mod):
    depth, d, n = w_mod.shape
    return pl.pallas_call(
        _mod_kernel,
        grid=(depth, n // TN_MOD),
        in_specs=[
            pl.BlockSpec((MOD_ROWS, d), lambda l, j: (0, 0)),
            pl.BlockSpec((None, d, TN_MOD), lambda l, j: (l, 0, j)),
            pl.BlockSpec((None, 1, TN_MOD), lambda l, j: (l, 0, j)),
        ],
        out_specs=pl.BlockSpec((None, MOD_ROWS, TN_MOD), lambda l, j: (l, 0, j)),
        out_shape=jax.ShapeDtypeStruct((depth, MOD_ROWS, n), F32),
        compiler_params=_cparams(2),
        name="modulation",
    )(cvec, w_mod, b_mod.reshape(depth, 1, n))


def _rope_tile(acc, c, s):
    lane = lax.broadcasted_iota(jnp.int32, c.shape, 1)
    upper = (lane & ROPE_FREQS) != 0
    outs = []
    for g in range(acc.shape[1] // LANES):
        xg = acc[:, g * LANES:(g + 1) * LANES]
        partner = jnp.where(upper, pltpu.roll(xg, ROPE_FREQS, axis=1),
                            pltpu.roll(xg, LANES - ROPE_FREQS, axis=1))
        outs.append(xg * c + partner * s)
    return jnp.concatenate(outs, axis=1)


def _prenorm_kernel(x_ref, g_ref, sc_ref, sh_ref, h_ref):
    h_ref[...] = _modulated_norm(x_ref[...], g_ref, sc_ref, sh_ref)


def _prenorm(x, mod3, g_pre, mod_row):
    t, d = x.shape
    tm = TM_MIX
    return pl.pallas_call(
        _prenorm_kernel,
        grid=(t // tm,),
        in_specs=[
            pl.BlockSpec((tm, d), lambda i: (i, 0)),
            pl.BlockSpec((1, d), lambda i: (0, 0)),
            pl.BlockSpec((1, 1, d), lambda i: (mod_row(i * tm) * N_MOD + 1, 0, 0)),
            pl.BlockSpec((1, 1, d), lambda i: (mod_row(i * tm) * N_MOD + 0, 0, 0)),
        ],
        out_specs=pl.BlockSpec((tm, d), lambda i: (i, 0)),
        out_shape=jax.ShapeDtypeStruct((t, d), BF16),
        compiler_params=_cparams(1),
        name="prenorm",
    )(x, g_pre, mod3, mod3)


def _in_proj_kernel(*refs, rope, kv_out, q_tiles, qk_tiles, k_tiles, v_tiles):
    h_ref, w_ref = refs[:2]
    pos = 2
    if rope:
        cos_ref, sin_ref = refs[pos:pos + 2]
        pos += 2
    z_ref = refs[pos]
    pos += 1
    if kv_out:
        k_ref, v_ref = refs[pos:pos + 2]
    j = pl.program_id(1)

    acc = jnp.dot(h_ref[...], w_ref[...].astype(BF16), preferred_element_type=F32)

    if rope:
        z_ref[...] = acc.astype(BF16)

        @pl.when(j < q_tiles)
        def _():
            z_ref[...] = _rope_tile(acc * Q_SCALE, cos_ref[...], sin_ref[...]).astype(BF16)

        @pl.when((j >= q_tiles) & (j < qk_tiles))
        def _():
            z_ref[...] = _rope_tile(acc, cos_ref[...], sin_ref[...]).astype(BF16)
    else:
        z_ref[...] = (acc * jnp.where(j < q_tiles, Q_SCALE, 1.0)).astype(BF16)

    if kv_out:
        @pl.when((j >= k_tiles[0]) & (j < k_tiles[1]))
        def _():
            k_ref[...] = acc

        @pl.when((j >= v_tiles[0]) & (j < v_tiles[1]))
        def _():
            v_ref[...] = acc


def _in_proj(h, w_in, layer, *, qk_w, pool_cols, rope_tabs=None, kv_out=False):
    t, d = h.shape
    n = w_in.shape[2]
    tm, tn = TM_IN, TN_IN
    nq = qk_w // tn
    n_pool = pool_cols // tn
    rope = rope_tabs is not None

    def w_tile(j):
        return jnp.where(j < 3 * nq, j + n_pool, jnp.where(j < 3 * nq + n_pool, j - 3 * nq, j))

    in_specs = [
        pl.BlockSpec((tm, d), lambda i, j: (i, 0)),
        pl.BlockSpec((None, d, tn), lambda i, j: (layer, 0, w_tile(j))),
    ]
    args = [h, w_in]
    if rope:
        tab_blocks = rope_tabs[0].shape[0] // tm
        tab_spec = pl.BlockSpec((tm, LANES), lambda i, j: (i % tab_blocks, 0))
        in_specs += [tab_spec, tab_spec]
        args += list(rope_tabs)
    out_specs = [pl.BlockSpec((tm, tn), lambda i, j: (i, j))]
    out_shape = [jax.ShapeDtypeStruct((t, n), BF16)]
    if kv_out:
        out_specs += [
            pl.BlockSpec((tm, tn), lambda i, j: (i, jnp.clip(j - nq, 0, nq - 1))),
            pl.BlockSpec((tm, tn), lambda i, j: (i, jnp.clip(j - 2 * nq, 0, nq - 1))),
        ]
        out_shape += [jax.ShapeDtypeStruct((t, qk_w), F32)] * 2
    kern = functools.partial(_in_proj_kernel, rope=rope, kv_out=kv_out, q_tiles=nq, qk_tiles=2 * nq,
                             k_tiles=(nq, 2 * nq), v_tiles=(2 * nq, 3 * nq))
    return pl.pallas_call(
        kern,
        grid=(t // tm, n // tn),
        in_specs=in_specs,
        out_specs=out_specs,
        out_shape=out_shape,
        compiler_params=_cparams(2),
        name="in_proj_rope" if rope else "in_proj_kv",
    )(*args)


def _lambda(lq1, lk1, lq2, lk2, lam_init):
    return (jnp.exp(jnp.sum(lq1 * lk1, axis=-1, keepdims=True))
            - jnp.exp(jnp.sum(lq2 * lk2, axis=-1, keepdims=True)) + lam_init)


def _diff_attn_head(q, ks, vs, lam, gain, lam_init):
    def softmax_parts(m):
        cols = slice(m * HEAD_DIM, (m + 1) * HEAD_DIM)
        ss = [lax.dot_general(q[:, cols], k[:, cols], (((1,), (1,)), ((), ())),
                              preferred_element_type=F32) for k in ks]
        mx = functools.reduce(jnp.maximum, [jnp.max(s, axis=-1, keepdims=True) for s in ss])
        es = [jnp.exp2(s - mx) for s in ss]
        den = functools.reduce(jnp.add, [jnp.sum(e, axis=-1, keepdims=True) for e in es])
        return es, 1.0 / den

    e1, inv1 = softmax_parts(0)
    e2, inv2 = softmax_parts(1)
    w2 = lam * inv2
    o = functools.reduce(jnp.add, [
        jnp.dot((a * inv1 - b * w2).astype(BF16), v, preferred_element_type=F32)
        for a, b, v in zip(e1, e2, vs)])
    return _rms(o) * gain * (1.0 - lam_init)


def _attn_ctx_kernel(q_ref, k_ref, v_ref, lq1, lk1, lq2, lk2, g_ref, o_ref, *, lam_init):
    lam = _lambda(lq1[...], lk1[...], lq2[...], lk2[...], lam_init)
    gain = g_ref[...]
    for h in range(N_HEADS):
        cols = slice(h * V_DIM, (h + 1) * V_DIM)
        o = _diff_attn_head(q_ref[:, cols], [k_ref[:, cols]], [v_ref[:, cols]], lam, gain, lam_init)
        o_ref[:, cols] = o.astype(BF16)


def _attn_lat_kernel(q_ref, k_ref, v_ref, ck_ref, cv_ref, lq1, lk1, lq2, lk2, g_ref, o_ref, *,
                     lam_init):
    lam = _lambda(lq1[...], lk1[...], lq2[...], lk2[...], lam_init)
    q = q_ref[...]
    chunks = [(ck_ref[...].astype(BF16), cv_ref[...].astype(BF16))]
    for c in range(k_ref.shape[0] // KEY_CHUNK):
        rows = slice(c * KEY_CHUNK, (c + 1) * KEY_CHUNK)
        chunks.append((k_ref[rows, :], v_ref[rows, :]))
    outs = []
    for m in range(2):
        cols = slice(m * HEAD_DIM, (m + 1) * HEAD_DIM)
        mx = den = acc = None
        for kc, vc in chunks:
            s = lax.dot_general(q[:, cols], kc[:, cols], (((1,), (1,)), ((), ())),
                                preferred_element_type=F32)
            s_max = jnp.max(s, axis=-1, keepdims=True)
            if mx is None:
                mx = s_max
                e = jnp.exp2(s - mx)
                den = jnp.sum(e, axis=-1, keepdims=True)
                acc = jnp.dot(e.astype(BF16), vc, preferred_element_type=F32)
            else:
                mx_new = jnp.maximum(mx, s_max)
                alpha = jnp.exp2(mx - mx_new)
                e = jnp.exp2(s - mx_new)
                den = alpha * den + jnp.sum(e, axis=-1, keepdims=True)
                acc = alpha * acc + jnp.dot(e.astype(BF16), vc, preferred_element_type=F32)
                mx = mx_new
        outs.append(acc * (1.0 / den))
    o = outs[0] - lam * outs[1]
    o_ref[...] = (_rms(o) * g_ref[...] * (1.0 - lam_init)).astype(BF16)


def _lam_specs(layer, n_axes):
    idx = (lambda b: (layer, 0, 0)) if n_axes == 1 else (lambda b, h, qi: (layer, 0, 0))
    return [pl.BlockSpec((None, 1, HEAD_DIM), idx)] * 4 + [pl.BlockSpec((None, 1, V_DIM), idx)]


def _attention_ctx(z, lam_args, layer, lam_init, *, seq, width):
    t = z.shape[0]
    kern = functools.partial(_attn_ctx_kernel, lam_init=lam_init)
    return pl.pallas_call(
        kern,
        grid=(t // seq,),
        in_specs=[pl.BlockSpec((seq, width), lambda b: (b, 0)),
                  pl.BlockSpec((seq, width), lambda b: (b, 1)),
                  pl.BlockSpec((seq, width), lambda b: (b, 2))] + _lam_specs(layer, 1),
        out_specs=pl.BlockSpec((seq, width), lambda b: (b, 0)),
        out_shape=jax.ShapeDtypeStruct((t, width), BF16),
        compiler_params=_cparams(1),
        name="attention_ctx",
    )(z, z, z, *lam_args)


def _attention_lat(z, cache_k, cache_v, lam_args, layer, lam_init, *, seq, width):
    t = z.shape[0]
    nb = t // seq
    nq = seq // TQ_LAT
    past = cache_k.shape[2]
    kern = functools.partial(_attn_lat_kernel, lam_init=lam_init)
    return pl.pallas_call(
        kern,
        grid=(nb, N_HEADS, nq),
        in_specs=[
            pl.BlockSpec((TQ_LAT, V_DIM), lambda b, h, qi: (b * nq + qi, h)),
            pl.BlockSpec((seq, V_DIM), lambda b, h, qi: (b, N_HEADS + h)),
            pl.BlockSpec((seq, V_DIM), lambda b, h, qi: (b, 2 * N_HEADS + h)),
            pl.BlockSpec((None, None, past, V_DIM), lambda b, h, qi: (b, layer, 0, h)),
            pl.BlockSpec((None, None, past, V_DIM), lambda b, h, qi: (b, layer, 0, h)),
        ] + _lam_specs(layer, 3),
        out_specs=pl.BlockSpec((TQ_LAT, V_DIM), lambda b, h, qi: (b * nq + qi, h)),
        out_shape=jax.ShapeDtypeStruct((t, width), BF16),
        compiler_params=_cparams(3),
        name="attention_lat",
    )(z, z, z, cache_k, cache_v, *lam_args)


def _seq_masks(row0, rows, seq):
    p = (row0 + lax.broadcasted_iota(jnp.int32, (rows, 1), 0)) % seq
    return (p != 0).astype(F32), (p != seq - 1).astype(F32)


def _pool_mixer(u_ref, upad_ref, pw_ref, ps_ref, lhs_ref, row0, seq):
    halo = BF16_SUBLANES
    tm = u_ref.shape[0]
    pg = pw_ref.shape[-1]
    span = POOL_SUB + 2 * halo
    t_idx = lax.broadcasted_iota(jnp.int32, (POOL_SUB, span), 0)
    c_idx = lax.broadcasted_iota(jnp.int32, (POOL_SUB, span), 1)
    off = c_idx - halo - t_idx
    t_col = lax.broadcasted_iota(jnp.int32, (POOL_SUB, 1), 0)
    for r in range(tm // POOL_SUB):
        pos0 = (row0 + r * POOL_SUB) % seq
        pos_c = pos0 + c_idx - halo
        in_seq = (pos_c >= 0) & (pos_c < seq)
        pos_t = pos0 + t_col
        for gi, w in enumerate(POOL_WINDOWS):
            half = w // 2
            band = ((off >= -half) & (off < w - half) & in_seq).astype(F32).astype(BF16)
            cnt = jnp.minimum(pos_t + (w - half), seq) - jnp.maximum(pos_t - half, 0)
            cols = slice(gi * pg, (gi + 1) * pg)
            win = jnp.dot(band, upad_ref[r * POOL_SUB:r * POOL_SUB + span, cols],
                          preferred_element_type=F32)
            mean = win / cnt.astype(F32)
            dlt = mean - u_ref[r * POOL_SUB:(r + 1) * POOL_SUB, cols].astype(F32)
            y = jnp.dot(dlt.astype(BF16), pw_ref[gi], preferred_element_type=F32) * ps_ref[:, cols]
            lhs_ref[r * POOL_SUB:(r + 1) * POOL_SUB, cols] = y.astype(BF16)


def _branches_kernel(u_ref, up_ref, un_ref, cb_ref, ccx_ref, ccxp_ref, ccxn_ref, ya_ref, g_ref,
                     w_ref, pw_ref, ps_ref, cw_ref, cbias_ref, o_ref,
                     acc_ref, lhs_ref, upad_ref, ppad_ref, *, seq):
    i = pl.program_id(0)
    s = pl.program_id(1)
    tm = u_ref.shape[0]
    halo = BF16_SUBLANES
    cwid = cb_ref.shape[1]
    row0 = i * tm

    def gated(lhs):
        gate = jax.nn.sigmoid(g_ref[...].astype(F32))
        return gate * jnp.dot(lhs, w_ref[...], preferred_element_type=F32)

    @pl.when(s == 0)
    def _():
        upad_ref[0:halo, :] = up_ref[...]
        upad_ref[halo:halo + tm, :] = u_ref[...]
        upad_ref[halo + tm:, :] = un_ref[...]
        _pool_mixer(u_ref, upad_ref, pw_ref, ps_ref, lhs_ref, row0, seq)
        acc_ref[...] = gated(lhs_ref[...])

    @pl.when((s == 1) | (s == 2))
    def _():
        acc_ref[...] += gated(ya_ref[...])

    @pl.when(s == 3)
    def _():
        def prod(ref):
            v = ref[...].astype(F32)
            return v[:, :cwid] * v[:, cwid:]

        ppad_ref[0:halo, :] = prod(ccxp_ref)
        ppad_ref[halo:halo + tm, :] = prod(ccx_ref)
        ppad_ref[halo + tm:, :] = prod(ccxn_ref)
        m_prev, m_next = _seq_masks(row0, tm, seq)
        p_prev = ppad_ref[halo - 1:halo - 1 + tm, :] * m_prev
        p_next = ppad_ref[halo + 1:halo + 1 + tm, :] * m_next
        conv = (p_prev * cw_ref[0:1, :] + ppad_ref[halo:halo + tm, :] * cw_ref[1:2, :]
                + p_next * cw_ref[2:3, :] + cbias_ref[...])
        y = cb_ref[...].astype(F32) * conv
        o_ref[...] = (acc_ref[...] + gated(y.astype(BF16))).astype(BF16)


def _branches(z, y_attn, w_br, pool_w, pool_scale, conv_w, conv_b, layer, *, seq, d, pool_w_cols,
              conv_cols, col0):
    t = z.shape[0]
    tm = TM_MIX
    halo = BF16_SUBLANES
    hb = tm // halo
    last_hb = t // halo - 1
    kc = w_br.shape[1] // 4
    prev = lambda i, s: jnp.maximum(i * hb - 1, 0)
    nxt = lambda i, s: jnp.minimum((i + 1) * hb, last_hb)
    pool_blk = col0["pool"] // pool_w_cols
    cb_blk = col0["cb"] // conv_cols
    ccx_blk = col0["ccx"] // (2 * conv_cols)
    g_blk = col0["g"] // d
    kern = functools.partial(_branches_kernel, seq=seq)
    return pl.pallas_call(
        kern,
        grid=(t // tm, 4),
        in_specs=[
            pl.BlockSpec((tm, pool_w_cols), lambda i, s: (i, pool_blk)),
            pl.BlockSpec((halo, pool_w_cols), lambda i, s: (prev(i, s), pool_blk)),
            pl.BlockSpec((halo, pool_w_cols), lambda i, s: (nxt(i, s), pool_blk)),
            pl.BlockSpec((tm, conv_cols), lambda i, s: (i, cb_blk)),
            pl.BlockSpec((tm, 2 * conv_cols), lambda i, s: (i, ccx_blk)),
            pl.BlockSpec((halo, 2 * conv_cols), lambda i, s: (prev(i, s), ccx_blk)),
            pl.BlockSpec((halo, 2 * conv_cols), lambda i, s: (nxt(i, s), ccx_blk)),
            pl.BlockSpec((tm, kc), lambda i, s: (i, jnp.clip(s - 1, 0, 1))),
            pl.BlockSpec((tm, d), lambda i, s: (i, g_blk + (s + 1) // 2)),
            pl.BlockSpec((None, kc, d), lambda i, s: (layer, s, 0)),
            pl.BlockSpec((None,) + pool_w.shape[1:], lambda i, s: (layer, 0, 0, 0)),
            pl.BlockSpec((None, 1, pool_w_cols), lambda i, s: (layer, 0, 0)),
            pl.BlockSpec((None, 3, conv_cols), lambda i, s: (layer, 0, 0)),
            pl.BlockSpec((None, 1, conv_cols), lambda i, s: (layer, 0, 0)),
        ],
        out_specs=pl.BlockSpec((tm, d), lambda i, s: (i, 0)),
        out_shape=jax.ShapeDtypeStruct((t, d), BF16),
        scratch_shapes=[
            pltpu.VMEM((tm, d), F32),
            pltpu.VMEM((tm, pool_w_cols), BF16),
            pltpu.VMEM((tm + 2 * halo, pool_w_cols), BF16),
            pltpu.VMEM((tm + 2 * halo, conv_cols), F32),
        ],
        compiler_params=_cparams(2),
        name="branches",
    )(z, z, z, z, z, z, z, y_attn, z, w_br, pool_w, pool_scale, conv_w, conv_b)


def _out_proj_kernel(m_ref, w_ref, x_ref, gt_ref, g_ref, gn_ref, scn_ref, shn_ref, o_ref, h_ref):
    m = jnp.dot(m_ref[...], w_ref[...], preferred_element_type=F32)
    x1 = x_ref[...] + _rms(m) * (gt_ref[0] * g_ref[...])
    o_ref[...] = x1
    h_ref[...] = _modulated_norm(x1, gn_ref, scn_ref, shn_ref)


def _out_proj(merged, w_out, x, mod3, g_post, g_pre_ffn, layer, mod_row):
    t, d = x.shape
    tm = TM_MIX
    mod_idx = lambda m: (lambda i: (mod_row(i * tm) * N_MOD + m, 0, 0))
    row = pl.BlockSpec((tm, d), lambda i: (i, 0))
    vec = pl.BlockSpec((1, d), lambda i: (0, 0))
    return pl.pallas_call(
        _out_proj_kernel,
        grid=(t // tm,),
        in_specs=[row, pl.BlockSpec((None, d, d), lambda i: (layer, 0, 0)), row,
                  pl.BlockSpec((1, 1, d), mod_idx(2)), vec, vec,
                  pl.BlockSpec((1, 1, d), mod_idx(4)), pl.BlockSpec((1, 1, d), mod_idx(3))],
        out_specs=[row, row],
        out_shape=[jax.ShapeDtypeStruct((t, d), F32), jax.ShapeDtypeStruct((t, d), BF16)],
        compiler_params=_cparams(1),
        name="out_proj",
    )(merged, w_out, x, mod3, g_post, g_pre_ffn, mod3, mod3)


def _gelu_tanh(x):
    c = math.sqrt(2.0 / math.pi)
    hx = 0.5 * x
    return hx * jnp.tanh(x * (c + (c * 0.044715) * (x * x))) + hx


def _ffn_layout(tm, seq):
    halo = BF16_SUBLANES
    pad = 8
    if seq > tm:
        assert seq % tm == 0 and (tm + 2 * halo) % (2 * halo) == 0
        rows = tm + 2 * halo
        return rows, rows, [(0, rows // 2, 0), (rows // 2, rows // 2, rows // 2)], [(halo, tm)], []
    assert tm % seq == 0
    nseg = tm // seq
    pieces = [(s * seq, seq, pad + s * (seq + pad)) for s in range(nseg)]
    outs = [(pad + s * (seq + pad), seq) for s in range(nseg)]
    zero_rows = [(s * (seq + pad), pad) for s in range(nseg + 1)]
    return tm, pad + nseg * (seq + pad), pieces, outs, zero_rows


def _ffn_kernel(*refs, seq, with_halo, emit_next):
    refs = list(refs)
    h_ref = refs.pop(0)
    hp_ref, hn_ref = (refs.pop(0), refs.pop(0)) if with_halo else (None, None)
    x_ref, gt_ref, gpost_ref = refs.pop(0), refs.pop(0), refs.pop(0)
    next_refs = (refs.pop(0), refs.pop(0), refs.pop(0)) if emit_next else None
    wg_ref, wv_ref, cp_ref, wd_ref = refs[:4]
    refs = refs[4:]
    o_ref = refs.pop(0)
    hnext_ref = refs.pop(0) if emit_next else None
    lhs_ref = refs.pop(0) if with_halo else h_ref
    acc_ref, u_ref = refs
    i = pl.program_id(0)
    f = pl.program_id(1)
    tm = x_ref.shape[0]
    halo = BF16_SUBLANES
    _, _, pieces, outs, zero_rows = _ffn_layout(tm, seq)

    @pl.when(f == 0)
    def _():
        if with_halo:
            has_prev = (i * tm) % seq != 0
            has_next = ((i + 1) * tm) % seq != 0
            lhs_ref[0:halo, :] = jnp.where(has_prev, hp_ref[...], jnp.zeros_like(hp_ref))
            lhs_ref[halo:halo + tm, :] = h_ref[...]
            lhs_ref[halo + tm:, :] = jnp.where(has_next, hn_ref[...], jnp.zeros_like(hn_ref))

        def zero_rows_of_acc(r, carry):
            acc_ref[pl.ds(pl.multiple_of(r * ZERO_ROWS, ZERO_ROWS), ZERO_ROWS), :] = jnp.zeros(
                (ZERO_ROWS, acc_ref.shape[1]), F32)
            return carry

        lax.fori_loop(0, tm // ZERO_ROWS, zero_rows_of_acc, 0)
        for start, rows in zero_rows:
            u_ref[:, start:start + rows, :] = jnp.zeros((2, rows, u_ref.shape[2]), F32)

    def conv_half(slot, w_ref):
        for lhs0, rows, u0 in pieces:
            u_ref[slot, u0:u0 + rows, :] = jnp.dot(lhs_ref[lhs0:lhs0 + rows, :], w_ref[...],
                                                   preferred_element_type=F32)
        p = slot * CONV_PARAM_ROWS
        return jnp.concatenate([
            u_ref[slot, u0 - 1:u0 - 1 + rows, :] * cp_ref[p:p + 1, :]
            + u_ref[slot, u0:u0 + rows, :] * cp_ref[p + 1:p + 2, :]
            + u_ref[slot, u0 + 1:u0 + 1 + rows, :] * cp_ref[p + 2:p + 3, :] + cp_ref[p + 3:p + 4, :]
            for u0, rows in outs], axis=0)

    gate = conv_half(0, wg_ref)
    val = conv_half(1, wv_ref)
    a = (_gelu_tanh(gate) * val).astype(BF16)
    acc_ref[...] += jnp.dot(a, wd_ref[...], preferred_element_type=F32)

    @pl.when(f == pl.num_programs(1) - 1)
    def _():
        x2 = x_ref[...] + _rms(acc_ref[...]) * (gt_ref[0] * gpost_ref[...])
        o_ref[...] = x2
        if emit_next:
            hnext_ref[...] = _modulated_norm(x2, *next_refs)


def _ffn(h, x, mod3, g_post, w_up, conv_params, w_down, layer, mod_row, *, seq, next_norm=None):
    t, d = x.shape
    tm, tf = TM_MIX, TF_FFN
    halo = BF16_SUBLANES
    hb = tm // halo
    last_hb = t // halo - 1
    nf = w_down.shape[1] // tf
    with_halo = seq > tm
    emit_next = next_norm is not None
    lhs_rows, u_rows, _, _, _ = _ffn_layout(tm, seq)
    row = pl.BlockSpec((tm, d), lambda i, f: (i, 0))
    vec = pl.BlockSpec((1, d), lambda i, f: (0, 0))

    def mod_spec(row_fn, m):
        return pl.BlockSpec((1, 1, d), lambda i, f: (row_fn(i * tm) * N_MOD + m, 0, 0))

    in_specs, args = [row], [h]
    if with_halo:
        in_specs += [pl.BlockSpec((halo, d), lambda i, f: (jnp.maximum(i * hb - 1, 0), 0)),
                     pl.BlockSpec((halo, d), lambda i, f: (jnp.minimum((i + 1) * hb, last_hb), 0))]
        args += [h, h]
    in_specs += [row, mod_spec(mod_row, 5), vec]
    args += [x, mod3, g_post]
    if emit_next:
        in_specs += [vec, mod_spec(next_norm[1], 1), mod_spec(next_norm[1], 0)]
        args += [next_norm[0], mod3, mod3]
    tile = lambda rows: pl.BlockSpec((None, rows, tf), lambda i, f: (layer, 0, f))
    in_specs += [tile(d), tile(d), tile(2 * CONV_PARAM_ROWS),
                 pl.BlockSpec((None, tf, d), lambda i, f: (layer, f, 0))]
    args += [*w_up, conv_params, w_down]
    scratch = [pltpu.VMEM((lhs_rows, d), BF16)] if with_halo else []
    scratch += [pltpu.VMEM((tm, d), F32), pltpu.VMEM((2, u_rows, tf), F32)]
    kern = functools.partial(_ffn_kernel, seq=seq, with_halo=with_halo, emit_next=emit_next)
    return pl.pallas_call(
        kern,
        grid=(t // tm, nf),
        in_specs=in_specs,
        out_specs=[row, row] if emit_next else [row],
        out_shape=[jax.ShapeDtypeStruct((t, d), F32)] + ([jax.ShapeDtypeStruct((t, d), BF16)] if emit_next else []),
        scratch_shapes=scratch,
        compiler_params=_cparams(2),
        name="ffn_halo" if with_halo else "ffn",
    )(*args)


def _prep_up_kernel(w_ref, g_ref, v_ref, *, d_ff):
    for half, o_ref in enumerate((g_ref, v_ref)):
        o_ref[:, 0:d_ff] = w_ref[:, half * d_ff:(half + 1) * d_ff].astype(BF16)
        o_ref[:, d_ff:] = jnp.zeros((o_ref.shape[0], o_ref.shape[1] - d_ff), BF16)


def _prep_down_kernel(w_ref, o_ref, *, d_ff):
    o_ref[0:d_ff, :] = w_ref[...].astype(BF16)
    o_ref[d_ff:, :] = jnp.zeros((o_ref.shape[0] - d_ff, o_ref.shape[1]), BF16)


def _prep_ffn_weights(w_up, w_down, d_ffp):
    depth, d, _ = w_up.shape
    d_ff = w_down.shape[1]
    half = jax.ShapeDtypeStruct((depth, d, d_ffp), BF16)
    up = pl.pallas_call(
        functools.partial(_prep_up_kernel, d_ff=d_ff),
        grid=(depth, d // PREP_BLOCK),
        in_specs=[pl.BlockSpec((None, PREP_BLOCK, 2 * d_ff), lambda l, r: (l, r, 0))],
        out_specs=[pl.BlockSpec((None, PREP_BLOCK, d_ffp), lambda l, r: (l, r, 0))] * 2,
        out_shape=[half, half],
        compiler_params=_cparams(2),
        name="prep_w_up",
    )(w_up)
    down = pl.pallas_call(
        functools.partial(_prep_down_kernel, d_ff=d_ff),
        grid=(depth, d // PREP_BLOCK),
        in_specs=[pl.BlockSpec((None, d_ff, PREP_BLOCK), lambda l, c: (l, 0, c))],
        out_specs=pl.BlockSpec((None, d_ffp, PREP_BLOCK), lambda l, c: (l, 0, c)),
        out_shape=jax.ShapeDtypeStruct((depth, d_ffp, d), BF16),
        compiler_params=_cparams(2),
        name="prep_w_down",
    )(w_down)
    return tuple(up), down


def _rope_tables(seq):
    rows = seq // GRID_W
    row = jnp.repeat(jnp.arange(rows), GRID_W)
    col = jnp.tile(jnp.arange(GRID_W), rows)
    inv = ROPE_BASE ** (-jnp.arange(ROPE_FREQS, dtype=F32) / ROPE_FREQS)
    ang = jnp.stack([row, col], axis=-1).astype(F32)[:, :, None] * inv
    cos, sin = jnp.cos(ang), jnp.sin(ang)
    cos_t = jnp.concatenate([cos[:, 0], cos[:, 0], cos[:, 1], cos[:, 1]], axis=-1)
    sin_t = jnp.concatenate([-sin[:, 0], sin[:, 0], -sin[:, 1], sin[:, 1]], axis=-1)
    return cos_t, sin_t


def _pad_axis(a, axis, size):
    pad = [(0, 0)] * a.ndim
    pad[axis] = (0, size - a.shape[axis])
    return jnp.pad(a, pad)


def kernel(x_prompt, x_sample, cache_k, cache_v, c, c_ctx, w_mod, b_mod, g_pre_mix, g_post_mix,
           g_pre_ffn, g_post_ffn, w_in, pool_w, pool_scale, lambda_q1, lambda_k1, lambda_q2, lambda_k2,
           g_subln, conv_w, conv_b, w_br_pool, w_br_attn, w_br_conv, w_out, ffn_w_up, ffn_conv_w,
           ffn_conv_b, ffn_w_down):
    batch, seq, d = x_prompt.shape
    dec_batch, dec_seq, _ = x_sample.shape
    depth = w_in.shape[0]
    past = cache_k.shape[2]
    qk_w = N_HEADS * 2 * HEAD_DIM
    v_w = N_HEADS * V_DIM
    pool_cols = pool_scale.shape[1]
    conv_cols = conv_b.shape[1]
    d_ff = ffn_w_down.shape[1]
    d_ffp = -(-d_ff // TF_FFN) * TF_FFN
    assert qk_w == v_w and dec_batch + 1 <= MOD_ROWS

    col0 = {"pool": 3 * qk_w, "cb": 3 * qk_w + pool_cols, "ccx": 3 * qk_w + pool_cols + conv_cols,
            "g": 3 * qk_w + pool_cols + 3 * conv_cols}
    w_br = jnp.concatenate([w_br_pool, w_br_attn, w_br_conv], axis=1).astype(BF16)
    w_out_b = w_out.astype(BF16)
    pool_w_b = pool_w.astype(BF16)

    def padded_half(a, start, axis, dtype):
        part = lax.slice_in_dim(a, start, start + d_ff, axis=axis).astype(dtype)
        zeros = jnp.zeros(part.shape[:axis] + (d_ffp - d_ff,) + part.shape[axis + 1:], dtype)
        return jnp.concatenate([part, zeros], axis=axis)

    w_up, w_down = _prep_ffn_weights(ffn_w_up, ffn_w_down, d_ffp)
    fcp = jnp.concatenate([blk for start in (0, d_ff) for blk in (
        padded_half(ffn_conv_w, start, 2, F32), padded_half(ffn_conv_b[:, None, :], start, 2, F32))], axis=1)

    cvec = _pad_axis(jnp.concatenate([c_ctx[None, :], c], axis=0), 0, MOD_ROWS)
    mod = _modulation(cvec, w_mod, b_mod)
    mod3 = mod.reshape(depth * MOD_ROWS * N_MOD, 1, d)

    rope_tabs = _rope_tables(dec_seq)
    ck = cache_k.reshape(dec_batch, depth, past, qk_w)
    cv = cache_v.reshape(dec_batch, depth, past, v_w)
    row1 = lambda a: a[:, None, :]
    lam_args = (row1(lambda_q1), row1(lambda_k1), row1(lambda_q2), row1(lambda_k2), row1(g_subln))
    pool_scale3, conv_b3 = row1(pool_scale), row1(conv_b)

    xc = x_prompt.reshape(batch * seq, d)
    xl = x_sample.reshape(dec_batch * dec_seq, d)
    ctx_row = lambda l: (lambda r: l * MOD_ROWS)
    lat_row = lambda l: (lambda r: l * MOD_ROWS + 1 + r // dec_seq)
    gl = lambda a, l: a[l][None, :]
    hc = _prenorm(xc, mod3, gl(g_pre_mix, 0), ctx_row(0))
    hl = _prenorm(xl, mod3, gl(g_pre_mix, 0), lat_row(0))
    new_k, new_v = [], []
    for l in range(depth):
        lam_init = 0.8 - 0.6 * math.exp(-0.3 * l)
        last = l == depth - 1
        mix = functools.partial(_branches, w_br=w_br, pool_w=pool_w_b, pool_scale=pool_scale3,
                                conv_w=conv_w, conv_b=conv_b3, layer=l, d=d, pool_w_cols=pool_cols,
                                conv_cols=conv_cols, col0=col0)
        ffn = functools.partial(_ffn, mod3=mod3, g_post=gl(g_post_ffn, l), w_up=w_up, conv_params=fcp,
                                w_down=w_down, layer=l)
        out_proj = functools.partial(_out_proj, w_out=w_out_b, mod3=mod3, g_post=gl(g_post_mix, l),
                                     g_pre_ffn=gl(g_pre_ffn, l), layer=l)
        next_norm = lambda row: None if last else (gl(g_pre_mix, l + 1), row(l + 1))

        z, k32, v32 = _in_proj(hc, w_in, l, qk_w=qk_w, pool_cols=pool_cols, kv_out=True)
        new_k.append(k32.reshape(batch, seq, N_HEADS, 2, HEAD_DIM))
        new_v.append(v32.reshape(batch, seq, N_HEADS, V_DIM))
        ya = _attention_ctx(z, lam_args, l, lam_init, seq=seq, width=v_w)
        merged = mix(z, ya, seq=seq)
        xc, h2 = out_proj(merged, x=xc, mod_row=ctx_row(l))
        res = ffn(h2, xc, mod_row=ctx_row(l), seq=seq, next_norm=next_norm(ctx_row))
        xc, hc = res if not last else (res[0], None)

        z = _in_proj(hl, w_in, l, qk_w=qk_w, pool_cols=pool_cols, rope_tabs=rope_tabs)[0]
        ya = _attention_lat(z, ck, cv, lam_args, l, lam_init, seq=dec_seq, width=v_w)
        merged = mix(z, ya, seq=dec_seq)
        xl, h2 = out_proj(merged, x=xl, mod_row=lat_row(l))
        res = ffn(h2, xl, mod_row=lat_row(l), seq=dec_seq, next_norm=next_norm(lat_row))
        xl, hl = res if not last else (res[0], None)

    return (xc.reshape(batch, seq, d), xl.reshape(dec_batch, dec_seq, d),
            jnp.stack(new_k, axis=1), jnp.stack(new_v, axis=1))
```

```python
import functools
import math

import jax
import jax.numpy as jnp
from jax import lax
from jax.experimental import pallas as pl
from jax.experimental.pallas import tpu as pltpu

F32 = jnp.float32
BF16 = jnp.bfloat16

GRID_W = 64
N_HEADS = 8
HEAD_DIM = 128
V_DIM = 2 * HEAD_DIM
POOL_WINDOWS = (2, 4, 8, 16)
ROPE_FREQS = HEAD_DIM // 4
ROPE_BASE = 10000.0
EPS = 1e-6
Q_SCALE = HEAD_DIM ** -0.5 * math.log2(math.e)
N_MOD = 6
N_BRANCH = 3

LANES = 128
BF16_SUBLANES = 16
VMEM_LIMIT_BYTES = 56 * 1024 * 1024
MOD_ROWS = 8

TM_IN = 2048
TN_IN = 512
TM_MIX = 512
POOL_SUB = 256
TQ_LAT = 512
TF_FFN = 512
KEY_CHUNK = 512
TN_MOD = 1024
ZERO_ROWS = 128
CONV_PARAM_ROWS = 4
PREP_BLOCK = 256


def _cparams(n_axes):
    return pltpu.CompilerParams(dimension_semantics=("arbitrary",) * n_axes,
                                vmem_limit_bytes=VMEM_LIMIT_BYTES)


def _rms(x):
    return x * lax.rsqrt(jnp.mean(x * x, axis=-1, keepdims=True) + EPS)


def _modulated_norm(x, gain_ref, sc_ref, sh_ref):
    return (_rms(x) * (gain_ref[...] * (1.0 + sc_ref[0])) + sh_ref[0]).astype(BF16)


def _mod_kernel(c_ref, w_ref, b_ref, o_ref):
    c = c_ref[...]
    s = (c * jax.nn.sigmoid(c)).astype(BF16)
    o_ref[...] = jnp.dot(s, w_ref[...].astype(BF16), preferred_element_type=F32) + b_ref[...]


def _modulation(cvec, w_mod, b_mod):
    depth, d, n = w_mod.shape
    return pl.pallas_call(
        _mod_kernel,
        grid=(depth, n // TN_MOD),
        in_specs=[
            pl.BlockSpec((MOD_ROWS, d), lambda l, j: (0, 0)),
            pl.BlockSpec((None, d, TN_MOD), lambda l, j: (l, 0, j)),
            pl.BlockSpec((None, 1, TN_MOD), lambda l, j: (l, 0, j)),
        ],
        out_specs=pl.BlockSpec((None, MOD_ROWS, TN_MOD), lambda l, j: (l, 0, j)),
        out_shape=jax.ShapeDtypeStruct((depth, MOD_ROWS, n), F32),
        compiler_params=_cparams(2),
        name="modulation",
    )(cvec, w_mod, b_mod.reshape(depth, 1, n))


def _rope_tile(acc, c, s):
    lane = lax.broadcasted_iota(jnp.int32, c.shape, 1)
    upper = (lane & ROPE_FREQS) != 0
    outs = []
    for g in range(acc.shape[1] // LANES):
        xg = acc[:, g * LANES:(g + 1) * LANES]
        partner = jnp.where(upper, pltpu.roll(xg, ROPE_FREQS, axis=1),
                            pltpu.roll(xg, LANES - ROPE_FREQS, axis=1))
        outs.append(xg * c + partner * s)
    return jnp.concatenate(outs, axis=1)


def _prenorm_kernel(x_ref, g_ref, sc_ref, sh_ref, h_ref):
    h_ref[...] = _modulated_norm(x_ref[...], g_ref, sc_ref, sh_ref)


def _prenorm(x, mod3, g_pre, mod_row):
    t, d = x.shape
    tm = TM_MIX
    return pl.pallas_call(
        _prenorm_kernel,
        grid=(t // tm,),
        in_specs=[
            pl.BlockSpec((tm, d), lambda i: (i, 0)),
            pl.BlockSpec((1, d), lambda i: (0, 0)),
            pl.BlockSpec((1, 1, d), lambda i: (mod_row(i * tm) * N_MOD + 1, 0, 0)),
            pl.BlockSpec((1, 1, d), lambda i: (mod_row(i * tm) * N_MOD + 0, 0, 0)),
        ],
        out_specs=pl.BlockSpec((tm, d), lambda i: (i, 0)),
        out_shape=jax.ShapeDtypeStruct((t, d), BF16),
        compiler_params=_cparams(1),
        name="prenorm",
    )(x, g_pre, mod3, mod3)


def _in_proj_kernel(*refs, rope, kv_out, q_tiles, qk_tiles, k_tiles, v_tiles):
    h_ref, w_ref = refs[:2]
    pos = 2
    if rope:
        cos_ref, sin_ref = refs[pos:pos + 2]
        pos += 2
    z_ref = refs[pos]
    pos += 1
    if kv_out:
        k_ref, v_ref = refs[pos:pos + 2]
    j = pl.program_id(1)

    acc = jnp.dot(h_ref[...], w_ref[...].astype(BF16), preferred_element_type=F32)

    if rope:
        z_ref[...] = acc.astype(BF16)

        @pl.when(j < q_tiles)
        def _():
            z_ref[...] = _rope_tile(acc * Q_SCALE, cos_ref[...], sin_ref[...]).astype(BF16)

        @pl.when((j >= q_tiles) & (j < qk_tiles))
        def _():
            z_ref[...] = _rope_tile(acc, cos_ref[...], sin_ref[...]).astype(BF16)
    else:
        z_ref[...] = (acc * jnp.where(j < q_tiles, Q_SCALE, 1.0)).astype(BF16)

    if kv_out:
        @pl.when((j >= k_tiles[0]) & (j < k_tiles[1]))
        def _():
            k_ref[...] = acc

        @pl.when((j >= v_tiles[0]) & (j < v_tiles[1]))
        def _():
            v_ref[...] = acc


def _in_proj(h, w_in, layer, *, qk_w, pool_cols, rope_tabs=None, kv_out=False):
    t, d = h.shape
    n = w_in.shape[2]
    tm, tn = TM_IN, TN_IN
    nq = qk_w // tn
    n_pool = pool_cols // tn
    rope = rope_tabs is not None

    def w_tile(j):
        return jnp.where(j < 3 * nq, j + n_pool, jnp.where(j < 3 * nq + n_pool, j - 3 * nq, j))

    in_specs = [
        pl.BlockSpec((tm, d), lambda i, j: (i, 0)),
        pl.BlockSpec((None, d, tn), lambda i, j: (layer, 0, w_tile(j))),
    ]
    args = [h, w_in]
    if rope:
        tab_blocks = rope_tabs[0].shape[0] // tm
        tab_spec = pl.BlockSpec((tm, LANES), lambda i, j: (i % tab_blocks, 0))
        in_specs += [tab_spec, tab_spec]
        args += list(rope_tabs)
    out_specs = [pl.BlockSpec((tm, tn), lambda i, j: (i, j))]
    out_shape = [jax.ShapeDtypeStruct((t, n), BF16)]
    if kv_out:
        out_specs += [
            pl.BlockSpec((tm, tn), lambda i, j: (i, jnp.clip(j - nq, 0, nq - 1))),
            pl.BlockSpec((tm, tn), lambda i, j: (i, jnp.clip(j - 2 * nq, 0, nq - 1))),
        ]
        out_shape += [jax.ShapeDtypeStruct((t, qk_w), F32)] * 2
    kern = functools.partial(_in_proj_kernel, rope=rope, kv_out=kv_out, q_tiles=nq, qk_tiles=2 * nq,
                             k_tiles=(nq, 2 * nq), v_tiles=(2 * nq, 3 * nq))
    return pl.pallas_call(
        kern,
        grid=(t // tm, n // tn),
        in_specs=in_specs,
        out_specs=out_specs,
        out_shape=out_shape,
        compiler_params=_cparams(2),
        name="in_proj_rope" if rope else "in_proj_kv",
    )(*args)


def _lambda(lq1, lk1, lq2, lk2, lam_init):
    return (jnp.exp(jnp.sum(lq1 * lk1, axis=-1, keepdims=True))
            - jnp.exp(jnp.sum(lq2 * lk2, axis=-1, keepdims=True)) + lam_init)


def _diff_attn_head(q, ks, vs, lam, gain, lam_init):
    def softmax_parts(m):
        cols = slice(m * HEAD_DIM, (m + 1) * HEAD_DIM)
        ss = [lax.dot_general(q[:, cols], k[:, cols], (((1,), (1,)), ((), ())),
                              preferred_element_type=F32) for k in ks]
        mx = functools.reduce(jnp.maximum, [jnp.max(s, axis=-1, keepdims=True) for s in ss])
        es = [jnp.exp2(s - mx) for s in ss]
        den = functools.reduce(jnp.add, [jnp.sum(e, axis=-1, keepdims=True) for e in es])
        return es, 1.0 / den

    e1, inv1 = softmax_parts(0)
    e2, inv2 = softmax_parts(1)
    w2 = lam * inv2
    o = functools.reduce(jnp.add, [
        jnp.dot((a * inv1 - b * w2).astype(BF16), v, preferred_element_type=F32)
        for a, b, v in zip(e1, e2, vs)])
    return _rms(o) * gain * (1.0 - lam_init)


def _attn_ctx_kernel(q_ref, k_ref, v_ref, lq1, lk1, lq2, lk2, g_ref, o_ref, *, lam_init):
    lam = _lambda(lq1[...], lk1[...], lq2[...], lk2[...], lam_init)
    gain = g_ref[...]
    for h in range(N_HEADS):
        cols = slice(h * V_DIM, (h + 1) * V_DIM)
        o = _diff_attn_head(q_ref[:, cols], [k_ref[:, cols]], [v_ref[:, cols]], lam, gain, lam_init)
        o_ref[:, cols] = o.astype(BF16)


def _attn_lat_kernel(q_ref, k_ref, v_ref, ck_ref, cv_ref, lq1, lk1, lq2, lk2, g_ref, o_ref, *,
                     lam_init):
    lam = _lambda(lq1[...], lk1[...], lq2[...], lk2[...], lam_init)
    q = q_ref[...]
    chunks = [(ck_ref[...].astype(BF16), cv_ref[...].astype(BF16))]
    for c in range(k_ref.shape[0] // KEY_CHUNK):
        rows = slice(c * KEY_CHUNK, (c + 1) * KEY_CHUNK)
        chunks.append((k_ref[rows, :], v_ref[rows, :]))
    outs = []
    for m in range(2):
        cols = slice(m * HEAD_DIM, (m + 1) * HEAD_DIM)
        mx = den = acc = None
        for kc, vc in chunks:
            s = lax.dot_general(q[:, cols], kc[:, cols], (((1,), (1,)), ((), ())),
                                preferred_element_type=F32)
            s_max = jnp.max(s, axis=-1, keepdims=True)
            if mx is None:
                mx = s_max
                e = jnp.exp2(s - mx)
                den = jnp.sum(e, axis=-1, keepdims=True)
                acc = jnp.dot(e.astype(BF16), vc, preferred_element_type=F32)
            else:
                mx_new = jnp.maximum(mx, s_max)
                alpha = jnp.exp2(mx - mx_new)
                e = jnp.exp2(s - mx_new)
                den = alpha * den + jnp.sum(e, axis=-1, keepdims=True)
                acc = alpha * acc + jnp.dot(e.astype(BF16), vc, preferred_element_type=F32)
                mx = mx_new
        outs.append(acc * (1.0 / den))
    o = outs[0] - lam * outs[1]
    o_ref[...] = (_rms(o) * g_ref[...] * (1.0 - lam_init)).astype(BF16)


def _lam_specs(layer, n_axes):
    idx = (lambda b: (layer, 0, 0)) if n_axes == 1 else (lambda b, h, qi: (layer, 0, 0))
    return [pl.BlockSpec((None, 1, HEAD_DIM), idx)] * 4 + [pl.BlockSpec((None, 1, V_DIM), idx)]


def _attention_ctx(z, lam_args, layer, lam_init, *, seq, width):
    t = z.shape[0]
    kern = functools.partial(_attn_ctx_kernel, lam_init=lam_init)
    return pl.pallas_call(
        kern,
        grid=(t // seq,),
        in_specs=[pl.BlockSpec((seq, width), lambda b: (b, 0)),
                  pl.BlockSpec((seq, width), lambda b: (b, 1)),
                  pl.BlockSpec((seq, width), lambda b: (b, 2))] + _lam_specs(layer, 1),
        out_specs=pl.BlockSpec((seq, width), lambda b: (b, 0)),
        out_shape=jax.ShapeDtypeStruct((t, width), BF16),
        compiler_params=_cparams(1),
        name="attention_ctx",
    )(z, z, z, *lam_args)


def _attention_lat(z, cache_k, cache_v, lam_args, layer, lam_init, *, seq, width):
    t = z.shape[0]
    nb = t // seq
    nq = seq // TQ_LAT
    past = cache_k.shape[2]
    kern = functools.partial(_attn_lat_kernel, lam_init=lam_init)
    return pl.pallas_call(
        kern,
        grid=(nb, N_HEADS, nq),
        in_specs=[
            pl.BlockSpec((TQ_LAT, V_DIM), lambda b, h, qi: (b * nq + qi, h)),
            pl.BlockSpec((seq, V_DIM), lambda b, h, qi: (b, N_HEADS + h)),
            pl.BlockSpec((seq, V_DIM), lambda b, h, qi: (b, 2 * N_HEADS + h)),
            pl.BlockSpec((None, None, past, V_DIM), lambda b, h, qi: (b, layer, 0, h)),
            pl.BlockSpec((None, None, past, V_DIM), lambda b, h, qi: (b, layer, 0, h)),
        ] + _lam_specs(layer, 3),
        out_specs=pl.BlockSpec((TQ_LAT, V_DIM), lambda b, h, qi: (b * nq + qi, h)),
        out_shape=jax.ShapeDtypeStruct((t, width), BF16),
        compiler_params=_cparams(3),
        name="attention_lat",
    )(z, z, z, cache_k, cache_v, *lam_args)


def _seq_masks(row0, rows, seq):
    p = (row0 + lax.broadcasted_iota(jnp.int32, (rows, 1), 0)) % seq
    return (p != 0).astype(F32), (p != seq - 1).astype(F32)


def _pool_mixer(u_ref, upad_ref, pw_ref, ps_ref, lhs_ref, row0, seq):
    halo = BF16_SUBLANES
    tm = u_ref.shape[0]
    pg = pw_ref.shape[-1]
    span = POOL_SUB + 2 * halo
    t_idx = lax.broadcasted_iota(jnp.int32, (POOL_SUB, span), 0)
    c_idx = lax.broadcasted_iota(jnp.int32, (POOL_SUB, span), 1)
    off = c_idx - halo - t_idx
    t_col = lax.broadcasted_iota(jnp.int32, (POOL_SUB, 1), 0)
    for r in range(tm // POOL_SUB):
        pos0 = (row0 + r * POOL_SUB) % seq
        pos_c = pos0 + c_idx - halo
        in_seq = (pos_c >= 0) & (pos_c < seq)
        pos_t = pos0 + t_col
        for gi, w in enumerate(POOL_WINDOWS):
            half = w // 2
            band = ((off >= -half) & (off < w - half) & in_seq).astype(F32).astype(BF16)
            cnt = jnp.minimum(pos_t + (w - half), seq) - jnp.maximum(pos_t - half, 0)
            cols = slice(gi * pg, (gi + 1) * pg)
            win = jnp.dot(band, upad_ref[r * POOL_SUB:r * POOL_SUB + span, cols],
                          preferred_element_type=F32)
            mean = win / cnt.astype(F32)
            dlt = mean - u_ref[r * POOL_SUB:(r + 1) * POOL_SUB, cols].astype(F32)
            y = jnp.dot(dlt.astype(BF16), pw_ref[gi], preferred_element_type=F32) * ps_ref[:, cols]
            lhs_ref[r * POOL_SUB:(r + 1) * POOL_SUB, cols] = y.astype(BF16)


def _branches_kernel(u_ref, up_ref, un_ref, cb_ref, ccx_ref, ccxp_ref, ccxn_ref, ya_ref, g_ref,
                     w_ref, pw_ref, ps_ref, cw_ref, cbias_ref, o_ref,
                     acc_ref, lhs_ref, upad_ref, ppad_ref, *, seq):
    i = pl.program_id(0)
    s = pl.program_id(1)
    tm = u_ref.shape[0]
    halo = BF16_SUBLANES
    cwid = cb_ref.shape[1]
    row0 = i * tm

    def gated(lhs):
        gate = jax.nn.sigmoid(g_ref[...].astype(F32))
        kc = lhs.shape[1]
        w = w_ref[pl.ds(pl.multiple_of(s * kc, kc), kc), :]
        return gate * jnp.dot(lhs, w, preferred_element_type=F32)

    @pl.when(s == 0)
    def _():
        upad_ref[0:halo, :] = up_ref[...]
        upad_ref[halo:halo + tm, :] = u_ref[...]
        upad_ref[halo + tm:, :] = un_ref[...]
        _pool_mixer(u_ref, upad_ref, pw_ref, ps_ref, lhs_ref, row0, seq)
        acc_ref[...] = gated(lhs_ref[...])

    @pl.when((s == 1) | (s == 2))
    def _():
        acc_ref[...] += gated(ya_ref[...])

    @pl.when(s == 3)
    def _():
        def prod(ref):
            v = ref[...].astype(F32)
            return v[:, :cwid] * v[:, cwid:]

        ppad_ref[0:halo, :] = prod(ccxp_ref)
        ppad_ref[halo:halo + tm, :] = prod(ccx_ref)
        ppad_ref[halo + tm:, :] = prod(ccxn_ref)
        m_prev, m_next = _seq_masks(row0, tm, seq)
        p_prev = ppad_ref[halo - 1:halo - 1 + tm, :] * m_prev
        p_next = ppad_ref[halo + 1:halo + 1 + tm, :] * m_next
        conv = (p_prev * cw_ref[0:1, :] + ppad_ref[halo:halo + tm, :] * cw_ref[1:2, :]
                + p_next * cw_ref[2:3, :] + cbias_ref[...])
        y = cb_ref[...].astype(F32) * conv
        o_ref[...] = (acc_ref[...] + gated(y.astype(BF16))).astype(BF16)


def _branches(z, y_attn, w_br, pool_w, pool_scale, conv_w, conv_b, layer, *, seq, d, pool_w_cols,
              conv_cols, col0):
    t = z.shape[0]
    tm = TM_MIX
    halo = BF16_SUBLANES
    hb = tm // halo
    last_hb = t // halo - 1
    kc = w_br.shape[1] // 4
    prev = lambda i, s: jnp.maximum(i * hb - 1, 0)
    nxt = lambda i, s: jnp.minimum((i + 1) * hb, last_hb)
    pool_blk = col0["pool"] // pool_w_cols
    cb_blk = col0["cb"] // conv_cols
    ccx_blk = col0["ccx"] // (2 * conv_cols)
    g_blk = col0["g"] // d
    kern = functools.partial(_branches_kernel, seq=seq)
    return pl.pallas_call(
        kern,
        grid=(t // tm, 4),
        in_specs=[
            pl.BlockSpec((tm, pool_w_cols), lambda i, s: (i, pool_blk)),
            pl.BlockSpec((halo, pool_w_cols), lambda i, s: (prev(i, s), pool_blk)),
            pl.BlockSpec((halo, pool_w_cols), lambda i, s: (nxt(i, s), pool_blk)),
            pl.BlockSpec((tm, conv_cols), lambda i, s: (i, cb_blk)),
            pl.BlockSpec((tm, 2 * conv_cols), lambda i, s: (i, ccx_blk)),
            pl.BlockSpec((halo, 2 * conv_cols), lambda i, s: (prev(i, s), ccx_blk)),
            pl.BlockSpec((halo, 2 * conv_cols), lambda i, s: (nxt(i, s), ccx_blk)),
            pl.BlockSpec((tm, kc), lambda i, s: (i, jnp.clip(s - 1, 0, 1))),
            pl.BlockSpec((tm, d), lambda i, s: (i, g_blk + (s + 1) // 2)),
            pl.BlockSpec((None, 4 * kc, d), lambda i, s: (layer, 0, 0), pipeline_mode=pl.Buffered(1)),
            pl.BlockSpec((None,) + pool_w.shape[1:], lambda i, s: (layer, 0, 0, 0)),
            pl.BlockSpec((None, 1, pool_w_cols), lambda i, s: (layer, 0, 0)),
            pl.BlockSpec((None, 3, conv_cols), lambda i, s: (layer, 0, 0)),
            pl.BlockSpec((None, 1, conv_cols), lambda i, s: (layer, 0, 0)),
        ],
        out_specs=pl.BlockSpec((tm, d), lambda i, s: (i, 0)),
        out_shape=jax.ShapeDtypeStruct((t, d), BF16),
        scratch_shapes=[
            pltpu.VMEM((tm, d), F32),
            pltpu.VMEM((tm, pool_w_cols), BF16),
            pltpu.VMEM((tm + 2 * halo, pool_w_cols), BF16),
            pltpu.VMEM((tm + 2 * halo, conv_cols), F32),
        ],
        compiler_params=_cparams(2),
        name="branches",
    )(z, z, z, z, z, z, z, y_attn, z, w_br, pool_w, pool_scale, conv_w, conv_b)


def _out_proj_kernel(m_ref, w_ref, x_ref, gt_ref, g_ref, gn_ref, scn_ref, shn_ref, o_ref, h_ref):
    m = jnp.dot(m_ref[...], w_ref[...], preferred_element_type=F32)
    x1 = x_ref[...] + _rms(m) * (gt_ref[0] * g_ref[...])
    o_ref[...] = x1
    h_ref[...] = _modulated_norm(x1, gn_ref, scn_ref, shn_ref)


def _out_proj(merged, w_out, x, mod3, g_post, g_pre_ffn, layer, mod_row):
    t, d = x.shape
    tm = TM_MIX
    mod_idx = lambda m: (lambda i: (mod_row(i * tm) * N_MOD + m, 0, 0))
    row = pl.BlockSpec((tm, d), lambda i: (i, 0))
    vec = pl.BlockSpec((1, d), lambda i: (0, 0))
    return pl.pallas_call(
        _out_proj_kernel,
        grid=(t // tm,),
        in_specs=[row, pl.BlockSpec((None, d, d), lambda i: (layer, 0, 0)), row,
                  pl.BlockSpec((1, 1, d), mod_idx(2)), vec, vec,
                  pl.BlockSpec((1, 1, d), mod_idx(4)), pl.BlockSpec((1, 1, d), mod_idx(3))],
        out_specs=[row, row],
        out_shape=[jax.ShapeDtypeStruct((t, d), F32), jax.ShapeDtypeStruct((t, d), BF16)],
        compiler_params=_cparams(1),
        name="out_proj",
    )(merged, w_out, x, mod3, g_post, g_pre_ffn, mod3, mod3)


def _gelu_tanh(x):
    c = math.sqrt(2.0 / math.pi)
    hx = 0.5 * x
    return hx * jnp.tanh(x * (c + (c * 0.044715) * (x * x))) + hx


def _ffn_layout(tm, seq):
    halo = BF16_SUBLANES
    pad = 8
    if seq > tm:
        assert seq % tm == 0 and (tm + 2 * halo) % (2 * halo) == 0
        rows = tm + 2 * halo
        return rows, rows, [(0, rows // 2, 0), (rows // 2, rows // 2, rows // 2)], [(halo, tm)], []
    assert tm % seq == 0
    nseg = tm // seq
    pieces = [(s * seq, seq, pad + s * (seq + pad)) for s in range(nseg)]
    outs = [(pad + s * (seq + pad), seq) for s in range(nseg)]
    zero_rows = [(s * (seq + pad), pad) for s in range(nseg + 1)]
    return tm, pad + nseg * (seq + pad), pieces, outs, zero_rows


def _ffn_kernel(*refs, seq, with_halo, emit_next):
    refs = list(refs)
    h_ref = refs.pop(0)
    hp_ref, hn_ref = (refs.pop(0), refs.pop(0)) if with_halo else (None, None)
    x_ref, gt_ref, gpost_ref = refs.pop(0), refs.pop(0), refs.pop(0)
    next_refs = (refs.pop(0), refs.pop(0), refs.pop(0)) if emit_next else None
    wg_ref, wv_ref, cp_ref, wd_ref = refs[:4]
    refs = refs[4:]
    o_ref = refs.pop(0)
    hnext_ref = refs.pop(0) if emit_next else None
    lhs_ref = refs.pop(0) if with_halo else h_ref
    acc_ref, u_ref = refs
    i = pl.program_id(0)
    f = pl.program_id(1)
    tm = x_ref.shape[0]
    halo = BF16_SUBLANES
    _, _, pieces, outs, zero_rows = _ffn_layout(tm, seq)

    @pl.when(f == 0)
    def _():
        if with_halo:
            has_prev = (i * tm) % seq != 0
            has_next = ((i + 1) * tm) % seq != 0
            lhs_ref[0:halo, :] = jnp.where(has_prev, hp_ref[...], jnp.zeros_like(hp_ref))
            lhs_ref[halo:halo + tm, :] = h_ref[...]
            lhs_ref[halo + tm:, :] = jnp.where(has_next, hn_ref[...], jnp.zeros_like(hn_ref))

        def zero_rows_of_acc(r, carry):
            acc_ref[pl.ds(pl.multiple_of(r * ZERO_ROWS, ZERO_ROWS), ZERO_ROWS), :] = jnp.zeros(
                (ZERO_ROWS, acc_ref.shape[1]), F32)
            return carry

        lax.fori_loop(0, tm // ZERO_ROWS, zero_rows_of_acc, 0)
        for start, rows in zero_rows:
            u_ref[:, start:start + rows, :] = jnp.zeros((2, rows, u_ref.shape[2]), F32)

    def conv_half(slot, w_ref):
        for lhs0, rows, u0 in pieces:
            u_ref[slot, u0:u0 + rows, :] = jnp.dot(lhs_ref[lhs0:lhs0 + rows, :], w_ref[...],
                                                   preferred_element_type=F32)
        p = slot * CONV_PARAM_ROWS
        return jnp.concatenate([
            u_ref[slot, u0 - 1:u0 - 1 + rows, :] * cp_ref[p:p + 1, :]
            + u_ref[slot, u0:u0 + rows, :] * cp_ref[p + 1:p + 2, :]
            + u_ref[slot, u0 + 1:u0 + 1 + rows, :] * cp_ref[p + 2:p + 3, :] + cp_ref[p + 3:p + 4, :]
            for u0, rows in outs], axis=0)

    gate = conv_half(0, wg_ref)
    val = conv_half(1, wv_ref)
    a = (_gelu_tanh(gate) * val).astype(BF16)
    acc_ref[...] += jnp.dot(a, wd_ref[...], preferred_element_type=F32)

    @pl.when(f == pl.num_programs(1) - 1)
    def _():
        x2 = x_ref[...] + _rms(acc_ref[...]) * (gt_ref[0] * gpost_ref[...])
        o_ref[...] = x2
        if emit_next:
            hnext_ref[...] = _modulated_norm(x2, *next_refs)


def _ffn(h, x, mod3, g_post, w_up, conv_params, w_down, layer, mod_row, *, seq, next_norm=None):
    t, d = x.shape
    tm, tf = TM_MIX, TF_FFN
    halo = BF16_SUBLANES
    hb = tm // halo
    last_hb = t // halo - 1
    nf = w_down.shape[1] // tf
    with_halo = seq > tm
    emit_next = next_norm is not None
    lhs_rows, u_rows, _, _, _ = _ffn_layout(tm, seq)
    row = pl.BlockSpec((tm, d), lambda i, f: (i, 0))
    vec = pl.BlockSpec((1, d), lambda i, f: (0, 0))

    def mod_spec(row_fn, m):
        return pl.BlockSpec((1, 1, d), lambda i, f: (row_fn(i * tm) * N_MOD + m, 0, 0))

    in_specs, args = [row], [h]
    if with_halo:
        in_specs += [pl.BlockSpec((halo, d), lambda i, f: (jnp.maximum(i * hb - 1, 0), 0)),
                     pl.BlockSpec((halo, d), lambda i, f: (jnp.minimum((i + 1) * hb, last_hb), 0))]
        args += [h, h]
    in_specs += [row, mod_spec(mod_row, 5), vec]
    args += [x, mod3, g_post]
    if emit_next:
        in_specs += [vec, mod_spec(next_norm[1], 1), mod_spec(next_norm[1], 0)]
        args += [next_norm[0], mod3, mod3]
    up_tile = pl.BlockSpec((None, None, d, tf), lambda i, f: (layer, f, 0, 0))
    in_specs += [up_tile, up_tile, pl.BlockSpec((None, 2 * CONV_PARAM_ROWS, tf), lambda i, f: (layer, 0, f)),
                 pl.BlockSpec((None, tf, d), lambda i, f: (layer, f, 0))]
    args += [*w_up, conv_params, w_down]
    scratch = [pltpu.VMEM((lhs_rows, d), BF16)] if with_halo else []
    scratch += [pltpu.VMEM((tm, d), F32), pltpu.VMEM((2, u_rows, tf), F32)]
    kern = functools.partial(_ffn_kernel, seq=seq, with_halo=with_halo, emit_next=emit_next)
    return pl.pallas_call(
        kern,
        grid=(t // tm, nf),
        in_specs=in_specs,
        out_specs=[row, row] if emit_next else [row],
        out_shape=[jax.ShapeDtypeStruct((t, d), F32)] + ([jax.ShapeDtypeStruct((t, d), BF16)] if emit_next else []),
        scratch_shapes=scratch,
        compiler_params=_cparams(2),
        name="ffn_halo" if with_halo else "ffn",
    )(*args)


def _prep_up_kernel(w_ref, g_ref, v_ref, *, d_ff):
    n_tiles, rows, tf = g_ref.shape
    for half, o_ref in enumerate((g_ref, v_ref)):
        for f in range(n_tiles):
            lo, hi = f * tf, min((f + 1) * tf, d_ff)
            o_ref[f, :, 0:hi - lo] = w_ref[:, half * d_ff + lo:half * d_ff + hi].astype(BF16)
            if hi - lo < tf:
                o_ref[f, :, hi - lo:] = jnp.zeros((rows, tf - (hi - lo)), BF16)


def _prep_down_kernel(w_ref, o_ref, *, d_ff):
    o_ref[0:d_ff, :] = w_ref[...].astype(BF16)
    o_ref[d_ff:, :] = jnp.zeros((o_ref.shape[0] - d_ff, o_ref.shape[1]), BF16)


def _prep_ffn_weights(w_up, w_down, d_ffp):
    depth, d, _ = w_up.shape
    d_ff = w_down.shape[1]
    n_tiles = d_ffp // TF_FFN
    half = jax.ShapeDtypeStruct((depth, n_tiles, d, TF_FFN), BF16)
    up = pl.pallas_call(
        functools.partial(_prep_up_kernel, d_ff=d_ff),
        grid=(depth, d // PREP_BLOCK),
        in_specs=[pl.BlockSpec((None, PREP_BLOCK, 2 * d_ff), lambda l, r: (l, r, 0))],
        out_specs=[pl.BlockSpec((None, n_tiles, PREP_BLOCK, TF_FFN), lambda l, r: (l, 0, r, 0))] * 2,
        out_shape=[half, half],
        compiler_params=_cparams(2),
        name="prep_w_up",
    )(w_up)
    down = pl.pallas_call(
        functools.partial(_prep_down_kernel, d_ff=d_ff),
        grid=(depth, d // PREP_BLOCK),
        in_specs=[pl.BlockSpec((None, d_ff, PREP_BLOCK), lambda l, c: (l, 0, c))],
        out_specs=pl.BlockSpec((None, d_ffp, PREP_BLOCK), lambda l, c: (l, 0, c)),
        out_shape=jax.ShapeDtypeStruct((depth, d_ffp, d), BF16),
        compiler_params=_cparams(2),
        name="prep_w_down",
    )(w_down)
    return tuple(up), down


def _rope_tables(seq):
    rows = seq // GRID_W
    row = jnp.repeat(jnp.arange(rows), GRID_W)
    col = jnp.tile(jnp.arange(GRID_W), rows)
    inv = ROPE_BASE ** (-jnp.arange(ROPE_FREQS, dtype=F32) / ROPE_FREQS)
    ang = jnp.stack([row, col], axis=-1).astype(F32)[:, :, None] * inv
    cos, sin = jnp.cos(ang), jnp.sin(ang)
    cos_t = jnp.concatenate([cos[:, 0], cos[:, 0], cos[:, 1], cos[:, 1]], axis=-1)
    sin_t = jnp.concatenate([-sin[:, 0], sin[:, 0], -sin[:, 1], sin[:, 1]], axis=-1)
    return cos_t, sin_t


def _pad_axis(a, axis, size):
    pad = [(0, 0)] * a.ndim
    pad[axis] = (0, size - a.shape[axis])
    return jnp.pad(a, pad)


def kernel(x_prompt, x_sample, cache_k, cache_v, c, c_ctx, w_mod, b_mod, g_pre_mix, g_post_mix,
           g_pre_ffn, g_post_ffn, w_in, pool_w, pool_scale, lambda_q1, lambda_k1, lambda_q2, lambda_k2,
           g_subln, conv_w, conv_b, w_br_pool, w_br_attn, w_br_conv, w_out, ffn_w_up, ffn_conv_w,
           ffn_conv_b, ffn_w_down):
    batch, seq, d = x_prompt.shape
    dec_batch, dec_seq, _ = x_sample.shape
    depth = w_in.shape[0]
    past = cache_k.shape[2]
    qk_w = N_HEADS * 2 * HEAD_DIM
    v_w = N_HEADS * V_DIM
    pool_cols = pool_scale.shape[1]
    conv_cols = conv_b.shape[1]
    d_ff = ffn_w_down.shape[1]
    d_ffp = -(-d_ff // TF_FFN) * TF_FFN
    assert qk_w == v_w and dec_batch + 1 <= MOD_ROWS

    col0 = {"pool": 3 * qk_w, "cb": 3 * qk_w + pool_cols, "ccx": 3 * qk_w + pool_cols + conv_cols,
            "g": 3 * qk_w + pool_cols + 3 * conv_cols}
    w_br = jnp.concatenate([w_br_pool, w_br_attn, w_br_conv], axis=1).astype(BF16)
    w_out_b = w_out.astype(BF16)
    pool_w_b = pool_w.astype(BF16)

    def padded_half(a, start, axis, dtype):
        part = lax.slice_in_dim(a, start, start + d_ff, axis=axis).astype(dtype)
        zeros = jnp.zeros(part.shape[:axis] + (d_ffp - d_ff,) + part.shape[axis + 1:], dtype)
        return jnp.concatenate([part, zeros], axis=axis)

    w_up, w_down = _prep_ffn_weights(ffn_w_up, ffn_w_down, d_ffp)
    fcp = jnp.concatenate([blk for start in (0, d_ff) for blk in (
        padded_half(ffn_conv_w, start, 2, F32), padded_half(ffn_conv_b[:, None, :], start, 2, F32))], axis=1)

    cvec = _pad_axis(jnp.concatenate([c_ctx[None, :], c], axis=0), 0, MOD_ROWS)
    mod = _modulation(cvec, w_mod, b_mod)
    mod3 = mod.reshape(depth * MOD_ROWS * N_MOD, 1, d)

    rope_tabs = _rope_tables(dec_seq)
    ck = cache_k.reshape(dec_batch, depth, past, qk_w)
    cv = cache_v.reshape(dec_batch, depth, past, v_w)
    row1 = lambda a: a[:, None, :]
    lam_args = (row1(lambda_q1), row1(lambda_k1), row1(lambda_q2), row1(lambda_k2), row1(g_subln))
    pool_scale3, conv_b3 = row1(pool_scale), row1(conv_b)

    xc = x_prompt.reshape(batch * seq, d)
    xl = x_sample.reshape(dec_batch * dec_seq, d)
    ctx_row = lambda l: (lambda r: l * MOD_ROWS)
    lat_row = lambda l: (lambda r: l * MOD_ROWS + 1 + r // dec_seq)
    gl = lambda a, l: a[l][None, :]
    hc = _prenorm(xc, mod3, gl(g_pre_mix, 0), ctx_row(0))
    hl = _prenorm(xl, mod3, gl(g_pre_mix, 0), lat_row(0))
    new_k, new_v = [], []
    for l in range(depth):
        lam_init = 0.8 - 0.6 * math.exp(-0.3 * l)
        last = l == depth - 1
        mix = functools.partial(_branches, w_br=w_br, pool_w=pool_w_b, pool_scale=pool_scale3,
                                conv_w=conv_w, conv_b=conv_b3, layer=l, d=d, pool_w_cols=pool_cols,
                                conv_cols=conv_cols, col0=col0)
        ffn = functools.partial(_ffn, mod3=mod3, g_post=gl(g_post_ffn, l), w_up=w_up, conv_params=fcp,
                                w_down=w_down, layer=l)
        out_proj = functools.partial(_out_proj, w_out=w_out_b, mod3=mod3, g_post=gl(g_post_mix, l),
                                     g_pre_ffn=gl(g_pre_ffn, l), layer=l)
        next_norm = lambda row: None if last else (gl(g_pre_mix, l + 1), row(l + 1))

        z, k32, v32 = _in_proj(hc, w_in, l, qk_w=qk_w, pool_cols=pool_cols, kv_out=True)
        new_k.append(k32.reshape(batch, seq, N_HEADS, 2, HEAD_DIM))
        new_v.append(v32.reshape(batch, seq, N_HEADS, V_DIM))
        ya = _attention_ctx(z, lam_args, l, lam_init, seq=seq, width=v_w)
        merged = mix(z, ya, seq=seq)
        xc, h2 = out_proj(merged, x=xc, mod_row=ctx_row(l))
        res = ffn(h2, xc, mod_row=ctx_row(l), seq=seq, next_norm=next_norm(ctx_row))
        xc, hc = res if not last else (res[0], None)

        z = _in_proj(hl, w_in, l, qk_w=qk_w, pool_cols=pool_cols, rope_tabs=rope_tabs)[0]
        ya = _attention_lat(z, ck, cv, lam_args, l, lam_init, seq=dec_seq, width=v_w)
        merged = mix(z, ya, seq=dec_seq)
        xl, h2 = out_proj(merged, x=xl, mod_row=lat_row(l))
        res = ffn(h2, xl, mod_row=lat_row(l), seq=dec_seq, next_norm=next_norm(lat_row))
        xl, hl = res if not last else (res[0], None)

    return (xc.reshape(batch, seq, d), xl.reshape(dec_batch, dec_seq, d),
            jnp.stack(new_k, axis=1), jnp.stack(new_v, axis=1))
```

```python
import functools
import math

import jax
import jax.numpy as jnp
from jax import lax
from jax.experimental import pallas as pl
from jax.experimental.pallas import tpu as pltpu

F32 = jnp.float32
BF16 = jnp.bfloat16

GRID_W = 64
N_HEADS = 8
HEAD_DIM = 128
V_DIM = 2 * HEAD_DIM
POOL_WINDOWS = (2, 4, 8, 16)
ROPE_FREQS = HEAD_DIM // 4
ROPE_BASE = 10000.0
EPS = 1e-6
Q_SCALE = HEAD_DIM ** -0.5 * math.log2(math.e)
N_MOD = 6
N_BRANCH = 3

LANES = 128
BF16_SUBLANES = 16
VMEM_LIMIT_BYTES = 56 * 1024 * 1024
MOD_ROWS = 8

TM_IN = 2048
TN_IN = 512
TM_MIX = 512
POOL_SUB = 256
TQ_LAT = 512
TF_FFN = 512
KEY_CHUNK = 512
TN_MOD = 1024
ZERO_ROWS = 128
CONV_PARAM_ROWS = 4
PREP_BLOCK = 256


def _cparams(n_axes):
    return pltpu.CompilerParams(dimension_semantics=("arbitrary",) * n_axes,
                                vmem_limit_bytes=VMEM_LIMIT_BYTES)


def _rms(x):
    return x * lax.rsqrt(jnp.mean(x * x, axis=-1, keepdims=True) + EPS)


def _modulated_norm(x, gain_ref, sc_ref, sh_ref):
    return (_rms(x) * (gain_ref[...] * (1.0 + sc_ref[0])) + sh_ref[0]).astype(BF16)


def _mod_kernel(c_ref, w_ref, b_ref, o_ref):
    c = c_ref[...]
    s = (c * jax.nn.sigmoid(c)).astype(BF16)
    o_ref[...] = jnp.dot(s, w_ref[...].astype(BF16), preferred_element_type=F32) + b_ref[...]


def _modulation(cvec, w_mod, b_mod):
    depth, d, n = w_mod.shape
    return pl.pallas_call(
        _mod_kernel,
        grid=(depth, n // TN_MOD),
        in_specs=[
            pl.BlockSpec((MOD_ROWS, d), lambda l, j: (0, 0)),
            pl.BlockSpec((None, d, TN_MOD), lambda l, j: (l, 0, j)),
            pl.BlockSpec((None, 1, TN_MOD), lambda l, j: (l, 0, j)),
        ],
        out_specs=pl.BlockSpec((None, MOD_ROWS, TN_MOD), lambda l, j: (l, 0, j)),
        out_shape=jax.ShapeDtypeStruct((depth, MOD_ROWS, n), F32),
        compiler_params=_cparams(2),
        name="modulation",
    )(cvec, w_mod, b_mod.reshape(depth, 1, n))


def _rope_tile(acc, c, s):
    lane = lax.broadcasted_iota(jnp.int32, c.shape, 1)
    upper = (lane & ROPE_FREQS) != 0
    outs = []
    for g in range(acc.shape[1] // LANES):
        xg = acc[:, g * LANES:(g + 1) * LANES]
        partner = jnp.where(upper, pltpu.roll(xg, ROPE_FREQS, axis=1),
                            pltpu.roll(xg, LANES - ROPE_FREQS, axis=1))
        outs.append(xg * c + partner * s)
    return jnp.concatenate(outs, axis=1)


def _prenorm_kernel(x_ref, g_ref, sc_ref, sh_ref, h_ref):
    h_ref[...] = _modulated_norm(x_ref[...], g_ref, sc_ref, sh_ref)


def _prenorm(x, mod3, g_pre, mod_row):
    t, d = x.shape
    tm = TM_MIX
    return pl.pallas_call(
        _prenorm_kernel,
        grid=(t // tm,),
        in_specs=[
            pl.BlockSpec((tm, d), lambda i: (i, 0)),
            pl.BlockSpec((1, d), lambda i: (0, 0)),
            pl.BlockSpec((1, 1, d), lambda i: (mod_row(i * tm) * N_MOD + 1, 0, 0)),
            pl.BlockSpec((1, 1, d), lambda i: (mod_row(i * tm) * N_MOD + 0, 0, 0)),
        ],
        out_specs=pl.BlockSpec((tm, d), lambda i: (i, 0)),
        out_shape=jax.ShapeDtypeStruct((t, d), BF16),
        compiler_params=_cparams(1),
        name="prenorm",
    )(x, g_pre, mod3, mod3)


def _in_proj_kernel(*refs, rope, kv_out, q_tiles, qk_tiles, k_tiles, v_tiles):
    h_ref, w_ref = refs[:2]
    pos = 2
    if rope:
        cos_ref, sin_ref = refs[pos:pos + 2]
        pos += 2
    z_ref = refs[pos]
    pos += 1
    if kv_out:
        k_ref, v_ref = refs[pos:pos + 2]
    j = pl.program_id(1)

    acc = jnp.dot(h_ref[...], w_ref[...].astype(BF16), preferred_element_type=F32)

    if rope:
        z_ref[...] = acc.astype(BF16)

        @pl.when(j < q_tiles)
        def _():
            z_ref[...] = _rope_tile(acc * Q_SCALE, cos_ref[...], sin_ref[...]).astype(BF16)

        @pl.when((j >= q_tiles) & (j < qk_tiles))
        def _():
            z_ref[...] = _rope_tile(acc, cos_ref[...], sin_ref[...]).astype(BF16)
    else:
        z_ref[...] = (acc * jnp.where(j < q_tiles, Q_SCALE, 1.0)).astype(BF16)

    if kv_out:
        @pl.when((j >= k_tiles[0]) & (j < k_tiles[1]))
        def _():
            k_ref[...] = acc

        @pl.when((j >= v_tiles[0]) & (j < v_tiles[1]))
        def _():
            v_ref[...] = acc


def _in_proj(h, w_in, layer, *, qk_w, pool_cols, rope_tabs=None, kv_out=False):
    t, d = h.shape
    n = w_in.shape[2]
    tm, tn = TM_IN, TN_IN
    nq = qk_w // tn
    n_pool = pool_cols // tn
    rope = rope_tabs is not None

    def w_tile(j):
        return jnp.where(j < 3 * nq, j + n_pool, jnp.where(j < 3 * nq + n_pool, j - 3 * nq, j))

    in_specs = [
        pl.BlockSpec((tm, d), lambda i, j: (i, 0)),
        pl.BlockSpec((None, d, tn), lambda i, j: (layer, 0, w_tile(j))),
    ]
    args = [h, w_in]
    if rope:
        tab_blocks = rope_tabs[0].shape[0] // tm
        tab_spec = pl.BlockSpec((tm, LANES), lambda i, j: (i % tab_blocks, 0))
        in_specs += [tab_spec, tab_spec]
        args += list(rope_tabs)
    out_specs = [pl.BlockSpec((tm, tn), lambda i, j: (i, j))]
    out_shape = [jax.ShapeDtypeStruct((t, n), BF16)]
    if kv_out:
        out_specs += [
            pl.BlockSpec((tm, tn), lambda i, j: (i, jnp.clip(j - nq, 0, nq - 1))),
            pl.BlockSpec((tm, tn), lambda i, j: (i, jnp.clip(j - 2 * nq, 0, nq - 1))),
        ]
        out_shape += [jax.ShapeDtypeStruct((t, qk_w), F32)] * 2
    kern = functools.partial(_in_proj_kernel, rope=rope, kv_out=kv_out, q_tiles=nq, qk_tiles=2 * nq,
                             k_tiles=(nq, 2 * nq), v_tiles=(2 * nq, 3 * nq))
    return pl.pallas_call(
        kern,
        grid=(t // tm, n // tn),
        in_specs=in_specs,
        out_specs=out_specs,
        out_shape=out_shape,
        compiler_params=_cparams(2),
        name="in_proj_rope" if rope else "in_proj_kv",
    )(*args)


def _lambda(lq1, lk1, lq2, lk2, lam_init):
    return (jnp.exp(jnp.sum(lq1 * lk1, axis=-1, keepdims=True))
            - jnp.exp(jnp.sum(lq2 * lk2, axis=-1, keepdims=True)) + lam_init)


def _diff_attn_head(q, ks, vs, lam, gain, lam_init):
    def softmax_parts(m):
        cols = slice(m * HEAD_DIM, (m + 1) * HEAD_DIM)
        ss = [lax.dot_general(q[:, cols], k[:, cols], (((1,), (1,)), ((), ())),
                              preferred_element_type=F32) for k in ks]
        mx = functools.reduce(jnp.maximum, [jnp.max(s, axis=-1, keepdims=True) for s in ss])
        es = [jnp.exp2(s - mx) for s in ss]
        den = functools.reduce(jnp.add, [jnp.sum(e, axis=-1, keepdims=True) for e in es])
        return es, 1.0 / den

    e1, inv1 = softmax_parts(0)
    e2, inv2 = softmax_parts(1)
    w2 = lam * inv2
    o = functools.reduce(jnp.add, [
        jnp.dot((a * inv1 - b * w2).astype(BF16), v, preferred_element_type=F32)
        for a, b, v in zip(e1, e2, vs)])
    return _rms(o) * gain * (1.0 - lam_init)


def _attn_ctx_kernel(q_ref, k_ref, v_ref, lq1, lk1, lq2, lk2, g_ref, o_ref, *, lam_init):
    lam = _lambda(lq1[...], lk1[...], lq2[...], lk2[...], lam_init)
    gain = g_ref[...]
    for h in range(N_HEADS):
        cols = slice(h * V_DIM, (h + 1) * V_DIM)
        o = _diff_attn_head(q_ref[:, cols], [k_ref[:, cols]], [v_ref[:, cols]], lam, gain, lam_init)
        o_ref[:, cols] = o.astype(BF16)


def _attn_lat_kernel(q_ref, k_ref, v_ref, ck_ref, cv_ref, lq1, lk1, lq2, lk2, g_ref, o_ref, *,
                     lam_init):
    lam = _lambda(lq1[...], lk1[...], lq2[...], lk2[...], lam_init)
    q = q_ref[...]
    chunks = [(ck_ref[...].astype(BF16), cv_ref[...].astype(BF16))]
    for c in range(k_ref.shape[0] // KEY_CHUNK):
        rows = slice(c * KEY_CHUNK, (c + 1) * KEY_CHUNK)
        chunks.append((k_ref[rows, :], v_ref[rows, :]))
    outs = []
    for m in range(2):
        cols = slice(m * HEAD_DIM, (m + 1) * HEAD_DIM)
        mx = den = acc = None
        for kc, vc in chunks:
            s = lax.dot_general(q[:, cols], kc[:, cols], (((1,), (1,)), ((), ())),
                                preferred_element_type=F32)
            s_max = jnp.max(s, axis=-1, keepdims=True)
            if mx is None:
                mx = s_max
                e = jnp.exp2(s - mx)
                den = jnp.sum(e, axis=-1, keepdims=True)
                acc = jnp.dot(e.astype(BF16), vc, preferred_element_type=F32)
            else:
                mx_new = jnp.maximum(mx, s_max)
                alpha = jnp.exp2(mx - mx_new)
                e = jnp.exp2(s - mx_new)
                den = alpha * den + jnp.sum(e, axis=-1, keepdims=True)
                acc = alpha * acc + jnp.dot(e.astype(BF16), vc, preferred_element_type=F32)
                mx = mx_new
        outs.append(acc * (1.0 / den))
    o = outs[0] - lam * outs[1]
    o_ref[...] = (_rms(o) * g_ref[...] * (1.0 - lam_init)).astype(BF16)


def _lam_specs(layer, n_axes):
    idx = (lambda b: (layer, 0, 0)) if n_axes == 1 else (lambda b, h, qi: (layer, 0, 0))
    return [pl.BlockSpec((None, 1, HEAD_DIM), idx)] * 4 + [pl.BlockSpec((None, 1, V_DIM), idx)]


def _attention_ctx(z, lam_args, layer, lam_init, *, seq, width):
    t = z.shape[0]
    kern = functools.partial(_attn_ctx_kernel, lam_init=lam_init)
    return pl.pallas_call(
        kern,
        grid=(t // seq,),
        in_specs=[pl.BlockSpec((seq, width), lambda b: (b, 0)),
                  pl.BlockSpec((seq, width), lambda b: (b, 1)),
                  pl.BlockSpec((seq, width), lambda b: (b, 2))] + _lam_specs(layer, 1),
        out_specs=pl.BlockSpec((seq, width), lambda b: (b, 0)),
        out_shape=jax.ShapeDtypeStruct((t, width), BF16),
        compiler_params=_cparams(1),
        name="attention_ctx",
    )(z, z, z, *lam_args)


def _attention_lat(z, cache_k, cache_v, lam_args, layer, lam_init, *, seq, width):
    t = z.shape[0]
    nb = t // seq
    nq = seq // TQ_LAT
    past = cache_k.shape[2]
    kern = functools.partial(_attn_lat_kernel, lam_init=lam_init)
    return pl.pallas_call(
        kern,
        grid=(nb, N_HEADS, nq),
        in_specs=[
            pl.BlockSpec((TQ_LAT, V_DIM), lambda b, h, qi: (b * nq + qi, h)),
            pl.BlockSpec((seq, V_DIM), lambda b, h, qi: (b, N_HEADS + h)),
            pl.BlockSpec((seq, V_DIM), lambda b, h, qi: (b, 2 * N_HEADS + h)),
            pl.BlockSpec((None, None, past, V_DIM), lambda b, h, qi: (b, layer, 0, h)),
            pl.BlockSpec((None, None, past, V_DIM), lambda b, h, qi: (b, layer, 0, h)),
        ] + _lam_specs(layer, 3),
        out_specs=pl.BlockSpec((TQ_LAT, V_DIM), lambda b, h, qi: (b * nq + qi, h)),
        out_shape=jax.ShapeDtypeStruct((t, width), BF16),
        compiler_params=_cparams(3),
        name="attention_lat",
    )(z, z, z, cache_k, cache_v, *lam_args)


def _seq_masks(row0, rows, seq):
    p = (row0 + lax.broadcasted_iota(jnp.int32, (rows, 1), 0)) % seq
    return (p != 0).astype(F32), (p != seq - 1).astype(F32)


def _pool_mixer(u_ref, upad_ref, pw_ref, ps_ref, lhs_ref, row0, seq):
    halo = BF16_SUBLANES
    tm = u_ref.shape[0]
    pg = pw_ref.shape[-1]
    span = POOL_SUB + 2 * halo
    t_idx = lax.broadcasted_iota(jnp.int32, (POOL_SUB, span), 0)
    c_idx = lax.broadcasted_iota(jnp.int32, (POOL_SUB, span), 1)
    off = c_idx - halo - t_idx
    t_col = lax.broadcasted_iota(jnp.int32, (POOL_SUB, 1), 0)
    for r in range(tm // POOL_SUB):
        pos0 = (row0 + r * POOL_SUB) % seq
        pos_c = pos0 + c_idx - halo
        in_seq = (pos_c >= 0) & (pos_c < seq)
        pos_t = pos0 + t_col
        for gi, w in enumerate(POOL_WINDOWS):
            half = w // 2
            band = ((off >= -half) & (off < w - half) & in_seq).astype(F32).astype(BF16)
            cnt = jnp.minimum(pos_t + (w - half), seq) - jnp.maximum(pos_t - half, 0)
            cols = slice(gi * pg, (gi + 1) * pg)
            win = jnp.dot(band, upad_ref[r * POOL_SUB:r * POOL_SUB + span, cols],
                          preferred_element_type=F32)
            mean = win / cnt.astype(F32)
            dlt = mean - u_ref[r * POOL_SUB:(r + 1) * POOL_SUB, cols].astype(F32)
            y = jnp.dot(dlt.astype(BF16), pw_ref[gi], preferred_element_type=F32) * ps_ref[:, cols]
            lhs_ref[r * POOL_SUB:(r + 1) * POOL_SUB, cols] = y.astype(BF16)


def _branches_kernel(u_ref, up_ref, un_ref, cb_ref, ccx_ref, ccxp_ref, ccxn_ref, ya_ref, g_ref,
                     w_ref, pw_ref, ps_ref, cw_ref, cbias_ref, o_ref,
                     acc_ref, lhs_ref, upad_ref, ppad_ref, *, seq):
    i = pl.program_id(0)
    s = pl.program_id(1)
    tm = u_ref.shape[0]
    halo = BF16_SUBLANES
    cwid = cb_ref.shape[1]
    row0 = i * tm

    def gated(lhs):
        gate = jax.nn.sigmoid(g_ref[...].astype(F32))
        kc = lhs.shape[1]
        w = w_ref[pl.ds(pl.multiple_of(s * kc, kc), kc), :]
        return gate * jnp.dot(lhs, w, preferred_element_type=F32)

    @pl.when(s == 0)
    def _():
        upad_ref[0:halo, :] = up_ref[...]
        upad_ref[halo:halo + tm, :] = u_ref[...]
        upad_ref[halo + tm:, :] = un_ref[...]
        _pool_mixer(u_ref, upad_ref, pw_ref, ps_ref, lhs_ref, row0, seq)
        acc_ref[...] = gated(lhs_ref[...])

    @pl.when((s == 1) | (s == 2))
    def _():
        acc_ref[...] += gated(ya_ref[...])

    @pl.when(s == 3)
    def _():
        def prod(ref):
            v = ref[...].astype(F32)
            return v[:, :cwid] * v[:, cwid:]

        ppad_ref[0:halo, :] = prod(ccxp_ref)
        ppad_ref[halo:halo + tm, :] = prod(ccx_ref)
        ppad_ref[halo + tm:, :] = prod(ccxn_ref)
        m_prev, m_next = _seq_masks(row0, tm, seq)
        p_prev = ppad_ref[halo - 1:halo - 1 + tm, :] * m_prev
        p_next = ppad_ref[halo + 1:halo + 1 + tm, :] * m_next
        conv = (p_prev * cw_ref[0:1, :] + ppad_ref[halo:halo + tm, :] * cw_ref[1:2, :]
                + p_next * cw_ref[2:3, :] + cbias_ref[...])
        y = cb_ref[...].astype(F32) * conv
        o_ref[...] = (acc_ref[...] + gated(y.astype(BF16))).astype(BF16)


def _branches(z, y_attn, w_br, pool_w, pool_scale, conv_w, conv_b, layer, *, seq, d, pool_w_cols,
              conv_cols, col0):
    t = z.shape[0]
    tm = TM_MIX
    halo = BF16_SUBLANES
    hb = tm // halo
    last_hb = t // halo - 1
    kc = w_br.shape[1] // 4
    prev = lambda i, s: jnp.maximum(i * hb - 1, 0)
    nxt = lambda i, s: jnp.minimum((i + 1) * hb, last_hb)
    pool_blk = col0["pool"] // pool_w_cols
    cb_blk = col0["cb"] // conv_cols
    ccx_blk = col0["ccx"] // (2 * conv_cols)
    g_blk = col0["g"] // d
    kern = functools.partial(_branches_kernel, seq=seq)
    return pl.pallas_call(
        kern,
        grid=(t // tm, 4),
        in_specs=[
            pl.BlockSpec((tm, pool_w_cols), lambda i, s: (i, pool_blk)),
            pl.BlockSpec((halo, pool_w_cols), lambda i, s: (prev(i, s), pool_blk)),
            pl.BlockSpec((halo, pool_w_cols), lambda i, s: (nxt(i, s), pool_blk)),
            pl.BlockSpec((tm, conv_cols), lambda i, s: (i, cb_blk)),
            pl.BlockSpec((tm, 2 * conv_cols), lambda i, s: (i, ccx_blk)),
            pl.BlockSpec((halo, 2 * conv_cols), lambda i, s: (prev(i, s), ccx_blk)),
            pl.BlockSpec((halo, 2 * conv_cols), lambda i, s: (nxt(i, s), ccx_blk)),
            pl.BlockSpec((tm, kc), lambda i, s: (i, jnp.clip(s - 1, 0, 1))),
            pl.BlockSpec((tm, d), lambda i, s: (i, g_blk + (s + 1) // 2)),
            pl.BlockSpec((None, 4 * kc, d), lambda i, s: (layer, 0, 0), pipeline_mode=pl.Buffered(1)),
            pl.BlockSpec((None,) + pool_w.shape[1:], lambda i, s: (layer, 0, 0, 0)),
            pl.BlockSpec((None, 1, pool_w_cols), lambda i, s: (layer, 0, 0)),
            pl.BlockSpec((None, 3, conv_cols), lambda i, s: (layer, 0, 0)),
            pl.BlockSpec((None, 1, conv_cols), lambda i, s: (layer, 0, 0)),
        ],
        out_specs=pl.BlockSpec((tm, d), lambda i, s: (i, 0)),
        out_shape=jax.ShapeDtypeStruct((t, d), BF16),
        scratch_shapes=[
            pltpu.VMEM((tm, d), F32),
            pltpu.VMEM((tm, pool_w_cols), BF16),
            pltpu.VMEM((tm + 2 * halo, pool_w_cols), BF16),
            pltpu.VMEM((tm + 2 * halo, conv_cols), F32),
        ],
        compiler_params=_cparams(2),
        name="branches",
    )(z, z, z, z, z, z, z, y_attn, z, w_br, pool_w, pool_scale, conv_w, conv_b)


def _out_proj_kernel(m_ref, w_ref, x_ref, gt_ref, g_ref, gn_ref, scn_ref, shn_ref, o_ref, h_ref):
    half = m_ref.shape[0] // 2
    for r in range(2):
        rows = slice(r * half, (r + 1) * half)
        m = jnp.dot(m_ref[rows, :], w_ref[...], preferred_element_type=F32)
        x1 = x_ref[rows, :] + _rms(m) * (gt_ref[0] * g_ref[...])
        o_ref[rows, :] = x1
        h_ref[rows, :] = _modulated_norm(x1, gn_ref, scn_ref, shn_ref)


def _out_proj(merged, w_out, x, mod3, g_post, g_pre_ffn, layer, mod_row):
    t, d = x.shape
    tm = TM_MIX
    mod_idx = lambda m: (lambda i: (mod_row(i * tm) * N_MOD + m, 0, 0))
    row = pl.BlockSpec((tm, d), lambda i: (i, 0))
    vec = pl.BlockSpec((1, d), lambda i: (0, 0))
    return pl.pallas_call(
        _out_proj_kernel,
        grid=(t // tm,),
        in_specs=[row, pl.BlockSpec((None, d, d), lambda i: (layer, 0, 0)), row,
                  pl.BlockSpec((1, 1, d), mod_idx(2)), vec, vec,
                  pl.BlockSpec((1, 1, d), mod_idx(4)), pl.BlockSpec((1, 1, d), mod_idx(3))],
        out_specs=[row, row],
        out_shape=[jax.ShapeDtypeStruct((t, d), F32), jax.ShapeDtypeStruct((t, d), BF16)],
        compiler_params=_cparams(1),
        name="out_proj",
    )(merged, w_out, x, mod3, g_post, g_pre_ffn, mod3, mod3)


def _gelu_tanh(x):
    c = math.sqrt(2.0 / math.pi)
    hx = 0.5 * x
    return hx * jnp.tanh(x * (c + (c * 0.044715) * (x * x))) + hx


def _ffn_layout(tm, seq):
    halo = BF16_SUBLANES
    pad = 8
    if seq > tm:
        assert seq % tm == 0 and (tm + 2 * halo) % (2 * halo) == 0
        rows = tm + 2 * halo
        return (rows, rows, [(0, rows // 2, 0), (rows // 2, rows // 2, rows // 2)],
                [(halo, tm // 2), (halo + tm // 2, tm // 2)], [])
    assert tm % seq == 0
    nseg = tm // seq
    pieces = [(s * seq, seq, pad + s * (seq + pad)) for s in range(nseg)]
    outs = [(pad + s * (seq + pad), seq) for s in range(nseg)]
    zero_rows = [(s * (seq + pad), pad) for s in range(nseg + 1)]
    return tm, pad + nseg * (seq + pad), pieces, outs, zero_rows


def _ffn_kernel(*refs, seq, with_halo, emit_next):
    refs = list(refs)
    h_ref = refs.pop(0)
    hp_ref, hn_ref = (refs.pop(0), refs.pop(0)) if with_halo else (None, None)
    x_ref, gt_ref, gpost_ref = refs.pop(0), refs.pop(0), refs.pop(0)
    next_refs = (refs.pop(0), refs.pop(0), refs.pop(0)) if emit_next else None
    wg_ref, wv_ref, cp_ref, wd_ref = refs[:4]
    refs = refs[4:]
    o_ref = refs.pop(0)
    hnext_ref = refs.pop(0) if emit_next else None
    lhs_ref = refs.pop(0) if with_halo else h_ref
    acc_ref, u_ref = refs
    i = pl.program_id(0)
    f = pl.program_id(1)
    tm = x_ref.shape[0]
    halo = BF16_SUBLANES
    _, _, pieces, outs, zero_rows = _ffn_layout(tm, seq)

    @pl.when(f == 0)
    def _():
        if with_halo:
            has_prev = (i * tm) % seq != 0
            has_next = ((i + 1) * tm) % seq != 0
            lhs_ref[0:halo, :] = jnp.where(has_prev, hp_ref[...], jnp.zeros_like(hp_ref))
            lhs_ref[halo:halo + tm, :] = h_ref[...]
            lhs_ref[halo + tm:, :] = jnp.where(has_next, hn_ref[...], jnp.zeros_like(hn_ref))

        def zero_rows_of_acc(r, carry):
            acc_ref[pl.ds(pl.multiple_of(r * ZERO_ROWS, ZERO_ROWS), ZERO_ROWS), :] = jnp.zeros(
                (ZERO_ROWS, acc_ref.shape[1]), F32)
            return carry

        lax.fori_loop(0, tm // ZERO_ROWS, zero_rows_of_acc, 0)
        for start, rows in zero_rows:
            u_ref[:, start:start + rows, :] = jnp.zeros((2, rows, u_ref.shape[2]), F32)

    def conv_half(slot, w_ref):
        for lhs0, rows, u0 in pieces:
            u_ref[slot, u0:u0 + rows, :] = jnp.dot(lhs_ref[lhs0:lhs0 + rows, :], w_ref[...],
                                                   preferred_element_type=F32)
        p = slot * CONV_PARAM_ROWS
        return [
            u_ref[slot, u0 - 1:u0 - 1 + rows, :] * cp_ref[p:p + 1, :]
            + u_ref[slot, u0:u0 + rows, :] * cp_ref[p + 1:p + 2, :]
            + u_ref[slot, u0 + 1:u0 + 1 + rows, :] * cp_ref[p + 2:p + 3, :] + cp_ref[p + 3:p + 4, :]
            for u0, rows in outs]

    row = 0
    for gate, val in zip(conv_half(0, wg_ref), conv_half(1, wv_ref)):
        a = (_gelu_tanh(gate) * val).astype(BF16)
        acc_ref[row:row + a.shape[0], :] += jnp.dot(a, wd_ref[...], preferred_element_type=F32)
        row += a.shape[0]

    @pl.when(f == pl.num_programs(1) - 1)
    def _():
        x2 = x_ref[...] + _rms(acc_ref[...]) * (gt_ref[0] * gpost_ref[...])
        o_ref[...] = x2
        if emit_next:
            hnext_ref[...] = _modulated_norm(x2, *next_refs)


def _ffn(h, x, mod3, g_post, w_up, conv_params, w_down, layer, mod_row, *, seq, next_norm=None):
    t, d = x.shape
    tm, tf = TM_MIX, TF_FFN
    halo = BF16_SUBLANES
    hb = tm // halo
    last_hb = t // halo - 1
    nf = w_down.shape[1] // tf
    with_halo = seq > tm
    emit_next = next_norm is not None
    lhs_rows, u_rows, _, _, _ = _ffn_layout(tm, seq)
    row = pl.BlockSpec((tm, d), lambda i, f: (i, 0))
    vec = pl.BlockSpec((1, d), lambda i, f: (0, 0))

    def mod_spec(row_fn, m):
        return pl.BlockSpec((1, 1, d), lambda i, f: (row_fn(i * tm) * N_MOD + m, 0, 0))

    in_specs, args = [row], [h]
    if with_halo:
        in_specs += [pl.BlockSpec((halo, d), lambda i, f: (jnp.maximum(i * hb - 1, 0), 0)),
                     pl.BlockSpec((halo, d), lambda i, f: (jnp.minimum((i + 1) * hb, last_hb), 0))]
        args += [h, h]
    in_specs += [row, mod_spec(mod_row, 5), vec]
    args += [x, mod3, g_post]
    if emit_next:
        in_specs += [vec, mod_spec(next_norm[1], 1), mod_spec(next_norm[1], 0)]
        args += [next_norm[0], mod3, mod3]
    up_tile = pl.BlockSpec((None, None, d, tf), lambda i, f: (layer, f, 0, 0))
    in_specs += [up_tile, up_tile, pl.BlockSpec((None, 2 * CONV_PARAM_ROWS, tf), lambda i, f: (layer, 0, f)),
                 pl.BlockSpec((None, tf, d), lambda i, f: (layer, f, 0))]
    args += [*w_up, conv_params, w_down]
    scratch = [pltpu.VMEM((lhs_rows, d), BF16)] if with_halo else []
    scratch += [pltpu.VMEM((tm, d), F32), pltpu.VMEM((2, u_rows, tf), F32)]
    kern = functools.partial(_ffn_kernel, seq=seq, with_halo=with_halo, emit_next=emit_next)
    return pl.pallas_call(
        kern,
        grid=(t // tm, nf),
        in_specs=in_specs,
        out_specs=[row, row] if emit_next else [row],
        out_shape=[jax.ShapeDtypeStruct((t, d), F32)] + ([jax.ShapeDtypeStruct((t, d), BF16)] if emit_next else []),
        scratch_shapes=scratch,
        compiler_params=_cparams(2),
        name="ffn_halo" if with_halo else "ffn",
    )(*args)


def _prep_up_kernel(w_ref, g_ref, v_ref, *, d_ff):
    n_tiles, rows, tf = g_ref.shape
    for half, o_ref in enumerate((g_ref, v_ref)):
        for f in range(n_tiles):
            lo, hi = f * tf, min((f + 1) * tf, d_ff)
            o_ref[f, :, 0:hi - lo] = w_ref[:, half * d_ff + lo:half * d_ff + hi].astype(BF16)
            if hi - lo < tf:
                o_ref[f, :, hi - lo:] = jnp.zeros((rows, tf - (hi - lo)), BF16)


def _prep_down_kernel(w_ref, o_ref, *, d_ff):
    o_ref[0:d_ff, :] = w_ref[...].astype(BF16)
    o_ref[d_ff:, :] = jnp.zeros((o_ref.shape[0] - d_ff, o_ref.shape[1]), BF16)


def _prep_ffn_weights(w_up, w_down, d_ffp):
    depth, d, _ = w_up.shape
    d_ff = w_down.shape[1]
    n_tiles = d_ffp // TF_FFN
    half = jax.ShapeDtypeStruct((depth, n_tiles, d, TF_FFN), BF16)
    up = pl.pallas_call(
        functools.partial(_prep_up_kernel, d_ff=d_ff),
        grid=(depth, d // PREP_BLOCK),
        in_specs=[pl.BlockSpec((None, PREP_BLOCK, 2 * d_ff), lambda l, r: (l, r, 0))],
        out_specs=[pl.BlockSpec((None, n_tiles, PREP_BLOCK, TF_FFN), lambda l, r: (l, 0, r, 0))] * 2,
        out_shape=[half, half],
        compiler_params=_cparams(2),
        name="prep_w_up",
    )(w_up)
    down = pl.pallas_call(
        functools.partial(_prep_down_kernel, d_ff=d_ff),
        grid=(depth, d // PREP_BLOCK),
        in_specs=[pl.BlockSpec((None, d_ff, PREP_BLOCK), lambda l, c: (l, 0, c))],
        out_specs=pl.BlockSpec((None, d_ffp, PREP_BLOCK), lambda l, c: (l, 0, c)),
        out_shape=jax.ShapeDtypeStruct((depth, d_ffp, d), BF16),
        compiler_params=_cparams(2),
        name="prep_w_down",
    )(w_down)
    return tuple(up), down


def _rope_tables(seq):
    rows = seq // GRID_W
    row = jnp.repeat(jnp.arange(rows), GRID_W)
    col = jnp.tile(jnp.arange(GRID_W), rows)
    inv = ROPE_BASE ** (-jnp.arange(ROPE_FREQS, dtype=F32) / ROPE_FREQS)
    ang = jnp.stack([row, col], axis=-1).astype(F32)[:, :, None] * inv
    cos, sin = jnp.cos(ang), jnp.sin(ang)
    cos_t = jnp.concatenate([cos[:, 0], cos[:, 0], cos[:, 1], cos[:, 1]], axis=-1)
    sin_t = jnp.concatenate([-sin[:, 0], sin[:, 0], -sin[:, 1], sin[:, 1]], axis=-1)
    return cos_t, sin_t


def _pad_axis(a, axis, size):
    pad = [(0, 0)] * a.ndim
    pad[axis] = (0, size - a.shape[axis])
    return jnp.pad(a, pad)


def kernel(x_prompt, x_sample, cache_k, cache_v, c, c_ctx, w_mod, b_mod, g_pre_mix, g_post_mix,
           g_pre_ffn, g_post_ffn, w_in, pool_w, pool_scale, lambda_q1, lambda_k1, lambda_q2, lambda_k2,
           g_subln, conv_w, conv_b, w_br_pool, w_br_attn, w_br_conv, w_out, ffn_w_up, ffn_conv_w,
           ffn_conv_b, ffn_w_down):
    batch, seq, d = x_prompt.shape
    dec_batch, dec_seq, _ = x_sample.shape
    depth = w_in.shape[0]
    past = cache_k.shape[2]
    qk_w = N_HEADS * 2 * HEAD_DIM
    v_w = N_HEADS * V_DIM
    pool_cols = pool_scale.shape[1]
    conv_cols = conv_b.shape[1]
    d_ff = ffn_w_down.shape[1]
    d_ffp = -(-d_ff // TF_FFN) * TF_FFN
    assert qk_w == v_w and dec_batch + 1 <= MOD_ROWS

    col0 = {"pool": 3 * qk_w, "cb": 3 * qk_w + pool_cols, "ccx": 3 * qk_w + pool_cols + conv_cols,
            "g": 3 * qk_w + pool_cols + 3 * conv_cols}
    w_br = jnp.concatenate([w_br_pool, w_br_attn, w_br_conv], axis=1).astype(BF16)
    w_out_b = w_out.astype(BF16)
    pool_w_b = pool_w.astype(BF16)

    def padded_half(a, start, axis, dtype):
        part = lax.slice_in_dim(a, start, start + d_ff, axis=axis).astype(dtype)
        zeros = jnp.zeros(part.shape[:axis] + (d_ffp - d_ff,) + part.shape[axis + 1:], dtype)
        return jnp.concatenate([part, zeros], axis=axis)

    w_up, w_down = _prep_ffn_weights(ffn_w_up, ffn_w_down, d_ffp)
    fcp = jnp.concatenate([blk for start in (0, d_ff) for blk in (
        padded_half(ffn_conv_w, start, 2, F32), padded_half(ffn_conv_b[:, None, :], start, 2, F32))], axis=1)

    cvec = _pad_axis(jnp.concatenate([c_ctx[None, :], c], axis=0), 0, MOD_ROWS)
    mod = _modulation(cvec, w_mod, b_mod)
    mod3 = mod.reshape(depth * MOD_ROWS * N_MOD, 1, d)

    rope_tabs = _rope_tables(dec_seq)
    ck = cache_k.reshape(dec_batch, depth, past, qk_w)
    cv = cache_v.reshape(dec_batch, depth, past, v_w)
    row1 = lambda a: a[:, None, :]
    lam_args = (row1(lambda_q1), row1(lambda_k1), row1(lambda_q2), row1(lambda_k2), row1(g_subln))
    pool_scale3, conv_b3 = row1(pool_scale), row1(conv_b)

    xc = x_prompt.reshape(batch * seq, d)
    xl = x_sample.reshape(dec_batch * dec_seq, d)
    ctx_row = lambda l: (lambda r: l * MOD_ROWS)
    lat_row = lambda l: (lambda r: l * MOD_ROWS + 1 + r // dec_seq)
    gl = lambda a, l: a[l][None, :]
    hc = _prenorm(xc, mod3, gl(g_pre_mix, 0), ctx_row(0))
    hl = _prenorm(xl, mod3, gl(g_pre_mix, 0), lat_row(0))
    new_k, new_v = [], []
    for l in range(depth):
        lam_init = 0.8 - 0.6 * math.exp(-0.3 * l)
        last = l == depth - 1
        mix = functools.partial(_branches, w_br=w_br, pool_w=pool_w_b, pool_scale=pool_scale3,
                                conv_w=conv_w, conv_b=conv_b3, layer=l, d=d, pool_w_cols=pool_cols,
                                conv_cols=conv_cols, col0=col0)
        ffn = functools.partial(_ffn, mod3=mod3, g_post=gl(g_post_ffn, l), w_up=w_up, conv_params=fcp,
                                w_down=w_down, layer=l)
        out_proj = functools.partial(_out_proj, w_out=w_out_b, mod3=mod3, g_post=gl(g_post_mix, l),
                                     g_pre_ffn=gl(g_pre_ffn, l), layer=l)
        next_norm = lambda row: None if last else (gl(g_pre_mix, l + 1), row(l + 1))

        z, k32, v32 = _in_proj(hc, w_in, l, qk_w=qk_w, pool_cols=pool_cols, kv_out=True)
        new_k.append(k32.reshape(batch, seq, N_HEADS, 2, HEAD_DIM))
        new_v.append(v32.reshape(batch, seq, N_HEADS, V_DIM))
        ya = _attention_ctx(z, lam_args, l, lam_init, seq=seq, width=v_w)
        merged = mix(z, ya, seq=seq)
        xc, h2 = out_proj(merged, x=xc, mod_row=ctx_row(l))
        res = ffn(h2, xc, mod_row=ctx_row(l), seq=seq, next_norm=next_norm(ctx_row))
        xc, hc = res if not last else (res[0], None)

        z = _in_proj(hl, w_in, l, qk_w=qk_w, pool_cols=pool_cols, rope_tabs=rope_tabs)[0]
        ya = _attention_lat(z, ck, cv, lam_args, l, lam_init, seq=dec_seq, width=v_w)
        merged = mix(z, ya, seq=dec_seq)
        xl, h2 = out_proj(merged, x=xl, mod_row=lat_row(l))
        res = ffn(h2, xl, mod_row=lat_row(l), seq=dec_seq, next_norm=next_norm(lat_row))
        xl, hl = res if not last else (res[0], None)

    return (xc.reshape(batch, seq, d), xl.reshape(dec_batch, dec_seq, d),
            jnp.stack(new_k, axis=1), jnp.stack(new_v, axis=1))
```

```python
import functools
import math

import jax
import jax.numpy as jnp
from jax import lax
from jax.experimental import pallas as pl
from jax.experimental.pallas import tpu as pltpu

F32 = jnp.float32
BF16 = jnp.bfloat16

GRID_W = 64
N_HEADS = 8
HEAD_DIM = 128
V_DIM = 2 * HEAD_DIM
POOL_WINDOWS = (2, 4, 8, 16)
ROPE_FREQS = HEAD_DIM // 4
ROPE_BASE = 10000.0
EPS = 1e-6
Q_SCALE = HEAD_DIM ** -0.5 * math.log2(math.e)
N_MOD = 6
N_BRANCH = 3

LANES = 128
BF16_SUBLANES = 16
VMEM_LIMIT_BYTES = 56 * 1024 * 1024
MOD_ROWS = 8

TM_IN = 2048
TM_IN_KV = 1024
TN_IN = 512
TM_MIX = 512
POOL_SUB = 256
TQ_LAT = 512
TF_FFN = 512
KEY_CHUNK = 512
TN_MOD = 1024
ZERO_ROWS = 128
CONV_PARAM_ROWS = 4
PREP_BLOCK = 256


def _cparams(n_axes):
    return pltpu.CompilerParams(dimension_semantics=("arbitrary",) * n_axes,
                                vmem_limit_bytes=VMEM_LIMIT_BYTES)


def _rms(x):
    return x * lax.rsqrt(jnp.mean(x * x, axis=-1, keepdims=True) + EPS)


def _modulated_norm(x, gain_ref, sc_ref, sh_ref):
    return (_rms(x) * (gain_ref[...] * (1.0 + sc_ref[0])) + sh_ref[0]).astype(BF16)


def _mod_kernel(c_ref, w_ref, b_ref, o_ref):
    c = c_ref[...]
    s = (c * jax.nn.sigmoid(c)).astype(BF16)
    o_ref[...] = jnp.dot(s, w_ref[...].astype(BF16), preferred_element_type=F32) + b_ref[...]


def _modulation(cvec, w_mod, b_mod):
    depth, d, n = w_mod.shape
    return pl.pallas_call(
        _mod_kernel,
        grid=(depth, n // TN_MOD),
        in_specs=[
            pl.BlockSpec((MOD_ROWS, d), lambda l, j: (0, 0)),
            pl.BlockSpec((None, d, TN_MOD), lambda l, j: (l, 0, j)),
            pl.BlockSpec((None, 1, TN_MOD), lambda l, j: (l, 0, j)),
        ],
        out_specs=pl.BlockSpec((None, MOD_ROWS, TN_MOD), lambda l, j: (l, 0, j)),
        out_shape=jax.ShapeDtypeStruct((depth, MOD_ROWS, n), F32),
        compiler_params=_cparams(2),
        name="modulation",
    )(cvec, w_mod, b_mod.reshape(depth, 1, n))


def _rope_tile(acc, c, s):
    lane = lax.broadcasted_iota(jnp.int32, c.shape, 1)
    upper = (lane & ROPE_FREQS) != 0
    outs = []
    for g in range(acc.shape[1] // LANES):
        xg = acc[:, g * LANES:(g + 1) * LANES]
        partner = jnp.where(upper, pltpu.roll(xg, ROPE_FREQS, axis=1),
                            pltpu.roll(xg, LANES - ROPE_FREQS, axis=1))
        outs.append(xg * c + partner * s)
    return jnp.concatenate(outs, axis=1)


def _prenorm_kernel(x_ref, g_ref, sc_ref, sh_ref, h_ref):
    h_ref[...] = _modulated_norm(x_ref[...], g_ref, sc_ref, sh_ref)


def _prenorm(x, mod3, g_pre, mod_row):
    t, d = x.shape
    tm = TM_MIX
    return pl.pallas_call(
        _prenorm_kernel,
        grid=(t // tm,),
        in_specs=[
            pl.BlockSpec((tm, d), lambda i: (i, 0)),
            pl.BlockSpec((1, d), lambda i: (0, 0)),
            pl.BlockSpec((1, 1, d), lambda i: (mod_row(i * tm) * N_MOD + 1, 0, 0)),
            pl.BlockSpec((1, 1, d), lambda i: (mod_row(i * tm) * N_MOD + 0, 0, 0)),
        ],
        out_specs=pl.BlockSpec((tm, d), lambda i: (i, 0)),
        out_shape=jax.ShapeDtypeStruct((t, d), BF16),
        compiler_params=_cparams(1),
        name="prenorm",
    )(x, g_pre, mod3, mod3)


def _in_proj_kernel(*refs, rope, kv_out, kv_aliased, q_tiles, qk_tiles, k_tiles, v_tiles):
    h_ref, w_ref = refs[:2]
    pos = 2
    if rope:
        cos_ref, sin_ref = refs[pos:pos + 2]
        pos += 2
    pos += kv_aliased
    z_ref = refs[pos]
    pos += 1
    if kv_out:
        k_ref, v_ref = refs[pos:pos + 2]
    j = pl.program_id(1)

    acc = jnp.dot(h_ref[...], w_ref[...].astype(BF16), preferred_element_type=F32)

    if rope:
        z_ref[...] = acc.astype(BF16)

        @pl.when(j < q_tiles)
        def _():
            z_ref[...] = _rope_tile(acc * Q_SCALE, cos_ref[...], sin_ref[...]).astype(BF16)

        @pl.when((j >= q_tiles) & (j < qk_tiles))
        def _():
            z_ref[...] = _rope_tile(acc, cos_ref[...], sin_ref[...]).astype(BF16)
    else:
        z_ref[...] = (acc * jnp.where(j < q_tiles, Q_SCALE, 1.0)).astype(BF16)

    if kv_out:
        n_seq, k_rows, _ = k_ref.shape
        tile_groups = acc.shape[1] // LANES
        lane_groups = tile_groups * (k_tiles[1] - k_tiles[0])
        seq = k_rows // lane_groups
        for jj in range(k_tiles[1] - k_tiles[0]):
            @pl.when(j == k_tiles[0] + jj)
            def _(jj=jj):
                for b in range(n_seq):
                    for g in range(tile_groups):
                        k_ref[b, pl.ds(jj * tile_groups + g, seq, stride=lane_groups), :] = (
                            acc[b * seq:(b + 1) * seq, g * LANES:(g + 1) * LANES])

        @pl.when((j >= v_tiles[0]) & (j < v_tiles[1]))
        def _():
            for b in range(n_seq):
                v_ref[b] = acc[b * seq:(b + 1) * seq, :]


def _in_proj(h, w_in, layer, *, qk_w, pool_cols, rope_tabs=None, kv_bufs=None):
    t, d = h.shape
    n = w_in.shape[2]
    kv_out = kv_bufs is not None
    tm = TM_IN_KV if kv_out else TM_IN
    tn = TN_IN
    nq = qk_w // tn
    n_pool = pool_cols // tn
    rope = rope_tabs is not None

    def w_tile(j):
        return jnp.where(j < 3 * nq, j + n_pool, jnp.where(j < 3 * nq + n_pool, j - 3 * nq, j))

    in_specs = [
        pl.BlockSpec((tm, d), lambda i, j: (i, 0)),
        pl.BlockSpec((None, d, tn), lambda i, j: (layer, 0, w_tile(j))),
    ]
    args = [h, w_in]
    if rope:
        tab_blocks = rope_tabs[0].shape[0] // tm
        tab_spec = pl.BlockSpec((tm, LANES), lambda i, j: (i % tab_blocks, 0))
        in_specs += [tab_spec, tab_spec]
        args += list(rope_tabs)
    out_specs = [pl.BlockSpec((tm, tn), lambda i, j: (i, j))]
    out_shape = [jax.ShapeDtypeStruct((t, n), BF16)]
    aliases = {}
    if kv_out:
        k_buf, v_buf = kv_bufs
        kv_seq = v_buf.shape[2]
        nb = tm // kv_seq
        out_specs += [
            pl.BlockSpec((nb, None) + k_buf.shape[2:], lambda i, j: (i, layer, 0, 0)),
            pl.BlockSpec((nb, None, kv_seq, tn), lambda i, j: (i, layer, 0, jnp.clip(j - 2 * nq, 0, nq - 1))),
        ]
        out_shape += [jax.ShapeDtypeStruct(k_buf.shape, F32), jax.ShapeDtypeStruct(v_buf.shape, F32)]
        aliases = {len(args): 1, len(args) + 1: 2}
        in_specs += [pl.BlockSpec(memory_space=pl.ANY)] * 2
        args += [k_buf, v_buf]
    kern = functools.partial(_in_proj_kernel, rope=rope, kv_out=kv_out, kv_aliased=2 * kv_out,
                             q_tiles=nq, qk_tiles=2 * nq,
                             k_tiles=(nq, 2 * nq), v_tiles=(2 * nq, 3 * nq))
    return pl.pallas_call(
        kern,
        grid=(t // tm, n // tn),
        in_specs=in_specs,
        out_specs=out_specs,
        out_shape=out_shape,
        input_output_aliases=aliases,
        compiler_params=_cparams(2),
        name="in_proj_rope" if rope else "in_proj_kv",
    )(*args)


def _lambda(lq1, lk1, lq2, lk2, lam_init):
    return (jnp.exp(jnp.sum(lq1 * lk1, axis=-1, keepdims=True))
            - jnp.exp(jnp.sum(lq2 * lk2, axis=-1, keepdims=True)) + lam_init)


def _diff_attn_head(q, ks, vs, lam, gain, lam_init):
    def softmax_parts(m):
        cols = slice(m * HEAD_DIM, (m + 1) * HEAD_DIM)
        ss = [lax.dot_general(q[:, cols], k[:, cols], (((1,), (1,)), ((), ())),
                              preferred_element_type=F32) for k in ks]
        mx = functools.reduce(jnp.maximum, [jnp.max(s, axis=-1, keepdims=True) for s in ss])
        es = [jnp.exp2(s - mx) for s in ss]
        den = functools.reduce(jnp.add, [jnp.sum(e, axis=-1, keepdims=True) for e in es])
        return es, 1.0 / den

    e1, inv1 = softmax_parts(0)
    e2, inv2 = softmax_parts(1)
    w2 = lam * inv2
    o = functools.reduce(jnp.add, [
        jnp.dot((a * inv1 - b * w2).astype(BF16), v, preferred_element_type=F32)
        for a, b, v in zip(e1, e2, vs)])
    return _rms(o) * gain * (1.0 - lam_init)


def _attn_ctx_kernel(q_ref, k_ref, v_ref, lq1, lk1, lq2, lk2, g_ref, o_ref, *, lam_init):
    lam = _lambda(lq1[...], lk1[...], lq2[...], lk2[...], lam_init)
    gain = g_ref[...]
    for h in range(N_HEADS):
        cols = slice(h * V_DIM, (h + 1) * V_DIM)
        o = _diff_attn_head(q_ref[:, cols], [k_ref[:, cols]], [v_ref[:, cols]], lam, gain, lam_init)
        o_ref[:, cols] = o.astype(BF16)


def _attn_lat_kernel(q_ref, k_ref, v_ref, ck_ref, cv_ref, lq1, lk1, lq2, lk2, g_ref, o_ref, *,
                     lam_init):
    lam = _lambda(lq1[...], lk1[...], lq2[...], lk2[...], lam_init)
    q = q_ref[...]
    chunks = [(ck_ref[...].astype(BF16), cv_ref[...].astype(BF16))]
    for c in range(k_ref.shape[0] // KEY_CHUNK):
        rows = slice(c * KEY_CHUNK, (c + 1) * KEY_CHUNK)
        chunks.append((k_ref[rows, :], v_ref[rows, :]))
    outs = []
    for m in range(2):
        cols = slice(m * HEAD_DIM, (m + 1) * HEAD_DIM)
        mx = den = acc = None
        for kc, vc in chunks:
            s = lax.dot_general(q[:, cols], kc[:, cols], (((1,), (1,)), ((), ())),
                                preferred_element_type=F32)
            s_max = jnp.max(s, axis=-1, keepdims=True)
            if mx is None:
                mx = s_max
                e = jnp.exp2(s - mx)
                den = jnp.sum(e, axis=-1, keepdims=True)
                acc = jnp.dot(e.astype(BF16), vc, preferred_element_type=F32)
            else:
                mx_new = jnp.maximum(mx, s_max)
                alpha = jnp.exp2(mx - mx_new)
                e = jnp.exp2(s - mx_new)
                den = alpha * den + jnp.sum(e, axis=-1, keepdims=True)
                acc = alpha * acc + jnp.dot(e.astype(BF16), vc, preferred_element_type=F32)
                mx = mx_new
        outs.append(acc * (1.0 / den))
    o = outs[0] - lam * outs[1]
    o_ref[...] = (_rms(o) * g_ref[...] * (1.0 - lam_init)).astype(BF16)


def _lam_specs(layer, n_axes):
    idx = (lambda b: (layer, 0, 0)) if n_axes == 1 else (lambda b, h, qi: (layer, 0, 0))
    return [pl.BlockSpec((None, 1, HEAD_DIM), idx)] * 4 + [pl.BlockSpec((None, 1, V_DIM), idx)]


def _attention_ctx(z, lam_args, layer, lam_init, *, seq, width):
    t = z.shape[0]
    kern = functools.partial(_attn_ctx_kernel, lam_init=lam_init)
    return pl.pallas_call(
        kern,
        grid=(t // seq,),
        in_specs=[pl.BlockSpec((seq, width), lambda b: (b, 0)),
                  pl.BlockSpec((seq, width), lambda b: (b, 1)),
                  pl.BlockSpec((seq, width), lambda b: (b, 2))] + _lam_specs(layer, 1),
        out_specs=pl.BlockSpec((seq, width), lambda b: (b, 0)),
        out_shape=jax.ShapeDtypeStruct((t, width), BF16),
        compiler_params=_cparams(1),
        name="attention_ctx",
    )(z, z, z, *lam_args)


def _attention_lat(z, cache_k, cache_v, lam_args, layer, lam_init, *, seq, width):
    t = z.shape[0]
    nb = t // seq
    nq = seq // TQ_LAT
    past = cache_k.shape[2]
    kern = functools.partial(_attn_lat_kernel, lam_init=lam_init)
    return pl.pallas_call(
        kern,
        grid=(nb, N_HEADS, nq),
        in_specs=[
            pl.BlockSpec((TQ_LAT, V_DIM), lambda b, h, qi: (b * nq + qi, h)),
            pl.BlockSpec((seq, V_DIM), lambda b, h, qi: (b, N_HEADS + h)),
            pl.BlockSpec((seq, V_DIM), lambda b, h, qi: (b, 2 * N_HEADS + h)),
            pl.BlockSpec((None, None, past, V_DIM), lambda b, h, qi: (b, layer, 0, h)),
            pl.BlockSpec((None, None, past, V_DIM), lambda b, h, qi: (b, layer, 0, h)),
        ] + _lam_specs(layer, 3),
        out_specs=pl.BlockSpec((TQ_LAT, V_DIM), lambda b, h, qi: (b * nq + qi, h)),
        out_shape=jax.ShapeDtypeStruct((t, width), BF16),
        compiler_params=_cparams(3),
        name="attention_lat",
    )(z, z, z, cache_k, cache_v, *lam_args)


def _seq_masks(row0, rows, seq):
    p = (row0 + lax.broadcasted_iota(jnp.int32, (rows, 1), 0)) % seq
    return (p != 0).astype(F32), (p != seq - 1).astype(F32)


def _pool_mixer(u_ref, upad_ref, pw_ref, ps_ref, lhs_ref, row0, seq):
    halo = BF16_SUBLANES
    tm = u_ref.shape[0]
    pg = pw_ref.shape[-1]
    span = POOL_SUB + 2 * halo
    t_idx = lax.broadcasted_iota(jnp.int32, (POOL_SUB, span), 0)
    c_idx = lax.broadcasted_iota(jnp.int32, (POOL_SUB, span), 1)
    off = c_idx - halo - t_idx
    t_col = lax.broadcasted_iota(jnp.int32, (POOL_SUB, 1), 0)
    for r in range(tm // POOL_SUB):
        pos0 = (row0 + r * POOL_SUB) % seq
        pos_c = pos0 + c_idx - halo
        in_seq = (pos_c >= 0) & (pos_c < seq)
        pos_t = pos0 + t_col
        for gi, w in enumerate(POOL_WINDOWS):
            half = w // 2
            band = ((off >= -half) & (off < w - half) & in_seq).astype(F32).astype(BF16)
            cnt = jnp.minimum(pos_t + (w - half), seq) - jnp.maximum(pos_t - half, 0)
            cols = slice(gi * pg, (gi + 1) * pg)
            win = jnp.dot(band, upad_ref[r * POOL_SUB:r * POOL_SUB + span, cols],
                          preferred_element_type=F32)
            mean = win / cnt.astype(F32)
            dlt = mean - u_ref[r * POOL_SUB:(r + 1) * POOL_SUB, cols].astype(F32)
            y = jnp.dot(dlt.astype(BF16), pw_ref[gi], preferred_element_type=F32) * ps_ref[:, cols]
            lhs_ref[r * POOL_SUB:(r + 1) * POOL_SUB, cols] = y.astype(BF16)


def _branches_kernel(u_ref, up_ref, un_ref, cb_ref, ccx_ref, ccxp_ref, ccxn_ref, ya_ref, g_ref,
                     w_ref, pw_ref, ps_ref, cw_ref, cbias_ref, o_ref,
                     acc_ref, lhs_ref, upad_ref, ppad_ref, *, seq):
    i = pl.program_id(0)
    s = pl.program_id(1)
    tm = u_ref.shape[0]
    halo = BF16_SUBLANES
    cwid = cb_ref.shape[1]
    row0 = i * tm

    def gated(lhs):
        gate = jax.nn.sigmoid(g_ref[...].astype(F32))
        kc = lhs.shape[1]
        w = w_ref[pl.ds(pl.multiple_of(s * kc, kc), kc), :]
        return gate * jnp.dot(lhs, w, preferred_element_type=F32)

    @pl.when(s == 0)
    def _():
        upad_ref[0:halo, :] = up_ref[...]
        upad_ref[halo:halo + tm, :] = u_ref[...]
        upad_ref[halo + tm:, :] = un_ref[...]
        _pool_mixer(u_ref, upad_ref, pw_ref, ps_ref, lhs_ref, row0, seq)
        acc_ref[...] = gated(lhs_ref[...])

    @pl.when((s == 1) | (s == 2))
    def _():
        acc_ref[...] += gated(ya_ref[...])

    @pl.when(s == 3)
    def _():
        def prod(ref):
            v = ref[...].astype(F32)
            return v[:, :cwid] * v[:, cwid:]

        ppad_ref[0:halo, :] = prod(ccxp_ref)
        ppad_ref[halo:halo + tm, :] = prod(ccx_ref)
        ppad_ref[halo + tm:, :] = prod(ccxn_ref)
        m_prev, m_next = _seq_masks(row0, tm, seq)
        p_prev = ppad_ref[halo - 1:halo - 1 + tm, :] * m_prev
        p_next = ppad_ref[halo + 1:halo + 1 + tm, :] * m_next
        conv = (p_prev * cw_ref[0:1, :] + ppad_ref[halo:halo + tm, :] * cw_ref[1:2, :]
                + p_next * cw_ref[2:3, :] + cbias_ref[...])
        y = cb_ref[...].astype(F32) * conv
        o_ref[...] = (acc_ref[...] + gated(y.astype(BF16))).astype(BF16)


def _branches(z, y_attn, w_br, pool_w, pool_scale, conv_w, conv_b, layer, *, seq, d, pool_w_cols,
              conv_cols, col0):
    t = z.shape[0]
    tm = TM_MIX
    halo = BF16_SUBLANES
    hb = tm // halo
    last_hb = t // halo - 1
    kc = w_br.shape[1] // 4
    prev = lambda i, s: jnp.maximum(i * hb - 1, 0)
    nxt = lambda i, s: jnp.minimum((i + 1) * hb, last_hb)
    pool_blk = col0["pool"] // pool_w_cols
    cb_blk = col0["cb"] // conv_cols
    ccx_blk = col0["ccx"] // (2 * conv_cols)
    g_blk = col0["g"] // d
    kern = functools.partial(_branches_kernel, seq=seq)
    return pl.pallas_call(
        kern,
        grid=(t // tm, 4),
        in_specs=[
            pl.BlockSpec((tm, pool_w_cols), lambda i, s: (i, pool_blk)),
            pl.BlockSpec((halo, pool_w_cols), lambda i, s: (prev(i, s), pool_blk)),
            pl.BlockSpec((halo, pool_w_cols), lambda i, s: (nxt(i, s), pool_blk)),
            pl.BlockSpec((tm, conv_cols), lambda i, s: (i, cb_blk)),
            pl.BlockSpec((tm, 2 * conv_cols), lambda i, s: (i, ccx_blk)),
            pl.BlockSpec((halo, 2 * conv_cols), lambda i, s: (prev(i, s), ccx_blk)),
            pl.BlockSpec((halo, 2 * conv_cols), lambda i, s: (nxt(i, s), ccx_blk)),
            pl.BlockSpec((tm, kc), lambda i, s: (i, jnp.clip(s - 1, 0, 1))),
            pl.BlockSpec((tm, d), lambda i, s: (i, g_blk + (s + 1) // 2)),
            pl.BlockSpec((None, 4 * kc, d), lambda i, s: (layer, 0, 0), pipeline_mode=pl.Buffered(1)),
            pl.BlockSpec((None,) + pool_w.shape[1:], lambda i, s: (layer, 0, 0, 0)),
            pl.BlockSpec((None, 1, pool_w_cols), lambda i, s: (layer, 0, 0)),
            pl.BlockSpec((None, 3, conv_cols), lambda i, s: (layer, 0, 0)),
            pl.BlockSpec((None, 1, conv_cols), lambda i, s: (layer, 0, 0)),
        ],
        out_specs=pl.BlockSpec((tm, d), lambda i, s: (i, 0)),
        out_shape=jax.ShapeDtypeStruct((t, d), BF16),
        scratch_shapes=[
            pltpu.VMEM((tm, d), F32),
            pltpu.VMEM((tm, pool_w_cols), BF16),
            pltpu.VMEM((tm + 2 * halo, pool_w_cols), BF16),
            pltpu.VMEM((tm + 2 * halo, conv_cols), F32),
        ],
        compiler_params=_cparams(2),
        name="branches",
    )(z, z, z, z, z, z, z, y_attn, z, w_br, pool_w, pool_scale, conv_w, conv_b)


def _out_proj_kernel(m_ref, w_ref, x_ref, gt_ref, g_ref, gn_ref, scn_ref, shn_ref, o_ref, h_ref):
    half = m_ref.shape[0] // 2
    for r in range(2):
        rows = slice(r * half, (r + 1) * half)
        m = jnp.dot(m_ref[rows, :], w_ref[...], preferred_element_type=F32)
        x1 = x_ref[rows, :] + _rms(m) * (gt_ref[0] * g_ref[...])
        o_ref[rows, :] = x1
        h_ref[rows, :] = _modulated_norm(x1, gn_ref, scn_ref, shn_ref)


def _out_proj(merged, w_out, x, mod3, g_post, g_pre_ffn, layer, mod_row):
    t, d = x.shape
    tm = TM_MIX
    mod_idx = lambda m: (lambda i: (mod_row(i * tm) * N_MOD + m, 0, 0))
    row = pl.BlockSpec((tm, d), lambda i: (i, 0))
    vec = pl.BlockSpec((1, d), lambda i: (0, 0))
    return pl.pallas_call(
        _out_proj_kernel,
        grid=(t // tm,),
        in_specs=[row, pl.BlockSpec((None, d, d), lambda i: (layer, 0, 0)), row,
                  pl.BlockSpec((1, 1, d), mod_idx(2)), vec, vec,
                  pl.BlockSpec((1, 1, d), mod_idx(4)), pl.BlockSpec((1, 1, d), mod_idx(3))],
        out_specs=[row, row],
        out_shape=[jax.ShapeDtypeStruct((t, d), F32), jax.ShapeDtypeStruct((t, d), BF16)],
        compiler_params=_cparams(1),
        name="out_proj",
    )(merged, w_out, x, mod3, g_post, g_pre_ffn, mod3, mod3)


def _gelu_tanh(x):
    c = math.sqrt(2.0 / math.pi)
    hx = 0.5 * x
    return hx * jnp.tanh(x * (c + (c * 0.044715) * (x * x))) + hx


def _ffn_layout(tm, seq):
    halo = BF16_SUBLANES
    pad = 8
    if seq > tm:
        assert seq % tm == 0 and (tm + 2 * halo) % (2 * halo) == 0
        rows = tm + 2 * halo
        return (rows, rows, [(0, rows // 2, 0), (rows // 2, rows // 2, rows // 2)],
                [(halo, tm // 2), (halo + tm // 2, tm // 2)], [])
    assert tm % seq == 0
    nseg = tm // seq
    pieces = [(s * seq, seq, pad + s * (seq + pad)) for s in range(nseg)]
    outs = [(pad + s * (seq + pad), seq) for s in range(nseg)]
    zero_rows = [(s * (seq + pad), pad) for s in range(nseg + 1)]
    return tm, pad + nseg * (seq + pad), pieces, outs, zero_rows


def _ffn_kernel(*refs, seq, with_halo, emit_next):
    refs = list(refs)
    h_ref = refs.pop(0)
    hp_ref, hn_ref = (refs.pop(0), refs.pop(0)) if with_halo else (None, None)
    x_ref, gt_ref, gpost_ref = refs.pop(0), refs.pop(0), refs.pop(0)
    next_refs = (refs.pop(0), refs.pop(0), refs.pop(0)) if emit_next else None
    wg_ref, wv_ref, cp_ref, wd_ref = refs[:4]
    refs = refs[4:]
    o_ref = refs.pop(0)
    hnext_ref = refs.pop(0) if emit_next else None
    lhs_ref = refs.pop(0) if with_halo else h_ref
    acc_ref, u_ref = refs
    i = pl.program_id(0)
    f = pl.program_id(1)
    tm = x_ref.shape[0]
    halo = BF16_SUBLANES
    _, _, pieces, outs, zero_rows = _ffn_layout(tm, seq)

    @pl.when(f == 0)
    def _():
        if with_halo:
            has_prev = (i * tm) % seq != 0
            has_next = ((i + 1) * tm) % seq != 0
            lhs_ref[0:halo, :] = jnp.where(has_prev, hp_ref[...], jnp.zeros_like(hp_ref))
            lhs_ref[halo:halo + tm, :] = h_ref[...]
            lhs_ref[halo + tm:, :] = jnp.where(has_next, hn_ref[...], jnp.zeros_like(hn_ref))

        def zero_rows_of_acc(r, carry):
            acc_ref[pl.ds(pl.multiple_of(r * ZERO_ROWS, ZERO_ROWS), ZERO_ROWS), :] = jnp.zeros(
                (ZERO_ROWS, acc_ref.shape[1]), F32)
            return carry

        lax.fori_loop(0, tm // ZERO_ROWS, zero_rows_of_acc, 0)
        for start, rows in zero_rows:
            u_ref[:, start:start + rows, :] = jnp.zeros((2, rows, u_ref.shape[2]), F32)

    def conv_half(slot, w_ref):
        for lhs0, rows, u0 in pieces:
            u_ref[slot, u0:u0 + rows, :] = jnp.dot(lhs_ref[lhs0:lhs0 + rows, :], w_ref[...],
                                                   preferred_element_type=F32)
        p = slot * CONV_PARAM_ROWS
        return [
            u_ref[slot, u0 - 1:u0 - 1 + rows, :] * cp_ref[p:p + 1, :]
            + u_ref[slot, u0:u0 + rows, :] * cp_ref[p + 1:p + 2, :]
            + u_ref[slot, u0 + 1:u0 + 1 + rows, :] * cp_ref[p + 2:p + 3, :] + cp_ref[p + 3:p + 4, :]
            for u0, rows in outs]

    row = 0
    for gate, val in zip(conv_half(0, wg_ref), conv_half(1, wv_ref)):
        a = (_gelu_tanh(gate) * val).astype(BF16)
        acc_ref[row:row + a.shape[0], :] += jnp.dot(a, wd_ref[...], preferred_element_type=F32)
        row += a.shape[0]

    @pl.when(f == pl.num_programs(1) - 1)
    def _():
        x2 = x_ref[...] + _rms(acc_ref[...]) * (gt_ref[0] * gpost_ref[...])
        o_ref[...] = x2
        if emit_next:
            hnext_ref[...] = _modulated_norm(x2, *next_refs)


def _ffn(h, x, mod3, g_post, w_up, conv_params, w_down, layer, mod_row, *, seq, next_norm=None):
    t, d = x.shape
    tm, tf = TM_MIX, TF_FFN
    halo = BF16_SUBLANES
    hb = tm // halo
    last_hb = t // halo - 1
    nf = w_down.shape[1] // tf
    with_halo = seq > tm
    emit_next = next_norm is not None
    lhs_rows, u_rows, _, _, _ = _ffn_layout(tm, seq)
    row = pl.BlockSpec((tm, d), lambda i, f: (i, 0))
    vec = pl.BlockSpec((1, d), lambda i, f: (0, 0))

    def mod_spec(row_fn, m):
        return pl.BlockSpec((1, 1, d), lambda i, f: (row_fn(i * tm) * N_MOD + m, 0, 0))

    in_specs, args = [row], [h]
    if with_halo:
        in_specs += [pl.BlockSpec((halo, d), lambda i, f: (jnp.maximum(i * hb - 1, 0), 0)),
                     pl.BlockSpec((halo, d), lambda i, f: (jnp.minimum((i + 1) * hb, last_hb), 0))]
        args += [h, h]
    in_specs += [row, mod_spec(mod_row, 5), vec]
    args += [x, mod3, g_post]
    if emit_next:
        in_specs += [vec, mod_spec(next_norm[1], 1), mod_spec(next_norm[1], 0)]
        args += [next_norm[0], mod3, mod3]
    up_tile = pl.BlockSpec((None, None, d, tf), lambda i, f: (layer, f, 0, 0))
    in_specs += [up_tile, up_tile, pl.BlockSpec((None, 2 * CONV_PARAM_ROWS, tf), lambda i, f: (layer, 0, f)),
                 pl.BlockSpec((None, tf, d), lambda i, f: (layer, f, 0))]
    args += [*w_up, conv_params, w_down]
    scratch = [pltpu.VMEM((lhs_rows, d), BF16)] if with_halo else []
    scratch += [pltpu.VMEM((tm, d), F32), pltpu.VMEM((2, u_rows, tf), F32)]
    kern = functools.partial(_ffn_kernel, seq=seq, with_halo=with_halo, emit_next=emit_next)
    return pl.pallas_call(
        kern,
        grid=(t // tm, nf),
        in_specs=in_specs,
        out_specs=[row, row] if emit_next else [row],
        out_shape=[jax.ShapeDtypeStruct((t, d), F32)] + ([jax.ShapeDtypeStruct((t, d), BF16)] if emit_next else []),
        scratch_shapes=scratch,
        compiler_params=_cparams(2),
        name="ffn_halo" if with_halo else "ffn",
    )(*args)


def _prep_up_kernel(w_ref, g_ref, v_ref, *, d_ff):
    n_tiles, rows, tf = g_ref.shape
    for half, o_ref in enumerate((g_ref, v_ref)):
        for f in range(n_tiles):
            lo, hi = f * tf, min((f + 1) * tf, d_ff)
            o_ref[f, :, 0:hi - lo] = w_ref[:, half * d_ff + lo:half * d_ff + hi].astype(BF16)
            if hi - lo < tf:
                o_ref[f, :, hi - lo:] = jnp.zeros((rows, tf - (hi - lo)), BF16)


def _prep_down_kernel(w_ref, o_ref, *, d_ff):
    o_ref[0:d_ff, :] = w_ref[...].astype(BF16)
    o_ref[d_ff:, :] = jnp.zeros((o_ref.shape[0] - d_ff, o_ref.shape[1]), BF16)


def _prep_ffn_weights(w_up, w_down, d_ffp):
    depth, d, _ = w_up.shape
    d_ff = w_down.shape[1]
    n_tiles = d_ffp // TF_FFN
    half = jax.ShapeDtypeStruct((depth, n_tiles, d, TF_FFN), BF16)
    up = pl.pallas_call(
        functools.partial(_prep_up_kernel, d_ff=d_ff),
        grid=(depth, d // PREP_BLOCK),
        in_specs=[pl.BlockSpec((None, PREP_BLOCK, 2 * d_ff), lambda l, r: (l, r, 0))],
        out_specs=[pl.BlockSpec((None, n_tiles, PREP_BLOCK, TF_FFN), lambda l, r: (l, 0, r, 0))] * 2,
        out_shape=[half, half],
        compiler_params=_cparams(2),
        name="prep_w_up",
    )(w_up)
    down = pl.pallas_call(
        functools.partial(_prep_down_kernel, d_ff=d_ff),
        grid=(depth, d // PREP_BLOCK),
        in_specs=[pl.BlockSpec((None, d_ff, PREP_BLOCK), lambda l, c: (l, 0, c))],
        out_specs=pl.BlockSpec((None, d_ffp, PREP_BLOCK), lambda l, c: (l, 0, c)),
        out_shape=jax.ShapeDtypeStruct((depth, d_ffp, d), BF16),
        compiler_params=_cparams(2),
        name="prep_w_down",
    )(w_down)
    return tuple(up), down


def _rope_tables(seq):
    rows = seq // GRID_W
    row = jnp.repeat(jnp.arange(rows), GRID_W)
    col = jnp.tile(jnp.arange(GRID_W), rows)
    inv = ROPE_BASE ** (-jnp.arange(ROPE_FREQS, dtype=F32) / ROPE_FREQS)
    ang = jnp.stack([row, col], axis=-1).astype(F32)[:, :, None] * inv
    cos, sin = jnp.cos(ang), jnp.sin(ang)
    cos_t = jnp.concatenate([cos[:, 0], cos[:, 0], cos[:, 1], cos[:, 1]], axis=-1)
    sin_t = jnp.concatenate([-sin[:, 0], sin[:, 0], -sin[:, 1], sin[:, 1]], axis=-1)
    return cos_t, sin_t


def _pad_axis(a, axis, size):
    pad = [(0, 0)] * a.ndim
    pad[axis] = (0, size - a.shape[axis])
    return jnp.pad(a, pad)


def kernel(x_prompt, x_sample, cache_k, cache_v, c, c_ctx, w_mod, b_mod, g_pre_mix, g_post_mix,
           g_pre_ffn, g_post_ffn, w_in, pool_w, pool_scale, lambda_q1, lambda_k1, lambda_q2, lambda_k2,
           g_subln, conv_w, conv_b, w_br_pool, w_br_attn, w_br_conv, w_out, ffn_w_up, ffn_conv_w,
           ffn_conv_b, ffn_w_down):
    batch, seq, d = x_prompt.shape
    dec_batch, dec_seq, _ = x_sample.shape
    depth = w_in.shape[0]
    past = cache_k.shape[2]
    qk_w = N_HEADS * 2 * HEAD_DIM
    v_w = N_HEADS * V_DIM
    pool_cols = pool_scale.shape[1]
    conv_cols = conv_b.shape[1]
    d_ff = ffn_w_down.shape[1]
    d_ffp = -(-d_ff // TF_FFN) * TF_FFN
    assert qk_w == v_w and dec_batch + 1 <= MOD_ROWS

    col0 = {"pool": 3 * qk_w, "cb": 3 * qk_w + pool_cols, "ccx": 3 * qk_w + pool_cols + conv_cols,
            "g": 3 * qk_w + pool_cols + 3 * conv_cols}
    w_br = jnp.concatenate([w_br_pool, w_br_attn, w_br_conv], axis=1).astype(BF16)
    w_out_b = w_out.astype(BF16)
    pool_w_b = pool_w.astype(BF16)

    def padded_half(a, start, axis, dtype):
        part = lax.slice_in_dim(a, start, start + d_ff, axis=axis).astype(dtype)
        zeros = jnp.zeros(part.shape[:axis] + (d_ffp - d_ff,) + part.shape[axis + 1:], dtype)
        return jnp.concatenate([part, zeros], axis=axis)

    w_up, w_down = _prep_ffn_weights(ffn_w_up, ffn_w_down, d_ffp)
    fcp = jnp.concatenate([blk for start in (0, d_ff) for blk in (
        padded_half(ffn_conv_w, start, 2, F32), padded_half(ffn_conv_b[:, None, :], start, 2, F32))], axis=1)

    cvec = _pad_axis(jnp.concatenate([c_ctx[None, :], c], axis=0), 0, MOD_ROWS)
    mod = _modulation(cvec, w_mod, b_mod)
    mod3 = mod.reshape(depth * MOD_ROWS * N_MOD, 1, d)

    rope_tabs = _rope_tables(dec_seq)
    ck = cache_k.reshape(dec_batch, depth, past, qk_w)
    cv = cache_v.reshape(dec_batch, depth, past, v_w)
    row1 = lambda a: a[:, None, :]
    lam_args = (row1(lambda_q1), row1(lambda_k1), row1(lambda_q2), row1(lambda_k2), row1(g_subln))
    pool_scale3, conv_b3 = row1(pool_scale), row1(conv_b)

    xc = x_prompt.reshape(batch * seq, d)
    xl = x_sample.reshape(dec_batch * dec_seq, d)
    ctx_row = lambda l: (lambda r: l * MOD_ROWS)
    lat_row = lambda l: (lambda r: l * MOD_ROWS + 1 + r // dec_seq)
    gl = lambda a, l: a[l][None, :]
    hc = _prenorm(xc, mod3, gl(g_pre_mix, 0), ctx_row(0))
    hl = _prenorm(xl, mod3, gl(g_pre_mix, 0), lat_row(0))
    kv_bufs = (jnp.zeros((batch, depth, seq * (qk_w // LANES), LANES), F32),
               jnp.zeros((batch, depth, seq, v_w), F32))
    for l in range(depth):
        lam_init = 0.8 - 0.6 * math.exp(-0.3 * l)
        last = l == depth - 1
        mix = functools.partial(_branches, w_br=w_br, pool_w=pool_w_b, pool_scale=pool_scale3,
                                conv_w=conv_w, conv_b=conv_b3, layer=l, d=d, pool_w_cols=pool_cols,
                                conv_cols=conv_cols, col0=col0)
        ffn = functools.partial(_ffn, mod3=mod3, g_post=gl(g_post_ffn, l), w_up=w_up, conv_params=fcp,
                                w_down=w_down, layer=l)
        out_proj = functools.partial(_out_proj, w_out=w_out_b, mod3=mod3, g_post=gl(g_post_mix, l),
                                     g_pre_ffn=gl(g_pre_ffn, l), layer=l)
        next_norm = lambda row: None if last else (gl(g_pre_mix, l + 1), row(l + 1))

        z, *kv_bufs = _in_proj(hc, w_in, l, qk_w=qk_w, pool_cols=pool_cols, kv_bufs=kv_bufs)
        ya = _attention_ctx(z, lam_args, l, lam_init, seq=seq, width=v_w)
        merged = mix(z, ya, seq=seq)
        xc, h2 = out_proj(merged, x=xc, mod_row=ctx_row(l))
        res = ffn(h2, xc, mod_row=ctx_row(l), seq=seq, next_norm=next_norm(ctx_row))
        xc, hc = res if not last else (res[0], None)

        z = _in_proj(hl, w_in, l, qk_w=qk_w, pool_cols=pool_cols, rope_tabs=rope_tabs)[0]
        ya = _attention_lat(z, ck, cv, lam_args, l, lam_init, seq=dec_seq, width=v_w)
        merged = mix(z, ya, seq=dec_seq)
        xl, h2 = out_proj(merged, x=xl, mod_row=lat_row(l))
        res = ffn(h2, xl, mod_row=lat_row(l), seq=dec_seq, next_norm=next_norm(lat_row))
        xl, hl = res if not last else (res[0], None)

    return (xc.reshape(batch, seq, d), xl.reshape(dec_batch, dec_seq, d),
            kv_bufs[0].reshape(batch, depth, seq, N_HEADS, 2, HEAD_DIM),
            kv_bufs[1].reshape(batch, depth, seq, N_HEADS, V_DIM))
```

```python
import functools
import math

import jax
import jax.numpy as jnp
from jax import lax
from jax.experimental import pallas as pl
from jax.experimental.pallas import tpu as pltpu

F32 = jnp.float32
BF16 = jnp.bfloat16

GRID_W = 64
N_HEADS = 8
HEAD_DIM = 128
V_DIM = 2 * HEAD_DIM
POOL_WINDOWS = (2, 4, 8, 16)
ROPE_FREQS = HEAD_DIM // 4
ROPE_BASE = 10000.0
EPS = 1e-6
Q_SCALE = HEAD_DIM ** -0.5 * math.log2(math.e)
N_MOD = 6
N_BRANCH = 3

LANES = 128
BF16_SUBLANES = 16
VMEM_LIMIT_BYTES = 56 * 1024 * 1024
MOD_ROWS = 8

TM_IN = 2048
TM_IN_KV = 1024
TN_IN = 512
TM_MIX = 512
POOL_SUB = 256
TQ_LAT = 512
TF_FFN = 512
KEY_CHUNK = 512
TN_MOD = 1024
ZERO_ROWS = 128
CONV_PARAM_ROWS = 4
PREP_BLOCK = 256


def _cparams(n_axes):
    return pltpu.CompilerParams(dimension_semantics=("arbitrary",) * n_axes,
                                vmem_limit_bytes=VMEM_LIMIT_BYTES)


def _rms(x):
    return x * lax.rsqrt(jnp.mean(x * x, axis=-1, keepdims=True) + EPS)


def _modulated_norm(x, gain_ref, sc_ref, sh_ref):
    return (_rms(x) * (gain_ref[...] * (1.0 + sc_ref[0])) + sh_ref[0]).astype(BF16)


def _mod_kernel(c_ref, w_ref, b_ref, o_ref):
    c = c_ref[...]
    s = (c * jax.nn.sigmoid(c)).astype(BF16)
    o_ref[...] = jnp.dot(s, w_ref[...].astype(BF16), preferred_element_type=F32) + b_ref[...]


def _modulation(cvec, w_mod, b_mod):
    depth, d, n = w_mod.shape
    return pl.pallas_call(
        _mod_kernel,
        grid=(depth, n // TN_MOD),
        in_specs=[
            pl.BlockSpec((MOD_ROWS, d), lambda l, j: (0, 0)),
            pl.BlockSpec((None, d, TN_MOD), lambda l, j: (l, 0, j)),
            pl.BlockSpec((None, 1, TN_MOD), lambda l, j: (l, 0, j)),
        ],
        out_specs=pl.BlockSpec((None, MOD_ROWS, TN_MOD), lambda l, j: (l, 0, j)),
        out_shape=jax.ShapeDtypeStruct((depth, MOD_ROWS, n), F32),
        compiler_params=_cparams(2),
        name="modulation",
    )(cvec, w_mod, b_mod.reshape(depth, 1, n))


def _rope_tile(acc, c, s):
    lane = lax.broadcasted_iota(jnp.int32, c.shape, 1)
    upper = (lane & ROPE_FREQS) != 0
    outs = []
    for g in range(acc.shape[1] // LANES):
        xg = acc[:, g * LANES:(g + 1) * LANES]
        partner = jnp.where(upper, pltpu.roll(xg, ROPE_FREQS, axis=1),
                            pltpu.roll(xg, LANES - ROPE_FREQS, axis=1))
        outs.append(xg * c + partner * s)
    return jnp.concatenate(outs, axis=1)


def _prenorm_kernel(x_ref, g_ref, sc_ref, sh_ref, h_ref):
    h_ref[...] = _modulated_norm(x_ref[...], g_ref, sc_ref, sh_ref)


def _prenorm(x, mod3, g_pre, mod_row):
    t, d = x.shape
    tm = TM_MIX
    return pl.pallas_call(
        _prenorm_kernel,
        grid=(t // tm,),
        in_specs=[
            pl.BlockSpec((tm, d), lambda i: (i, 0)),
            pl.BlockSpec((1, d), lambda i: (0, 0)),
            pl.BlockSpec((1, 1, d), lambda i: (mod_row(i * tm) * N_MOD + 1, 0, 0)),
            pl.BlockSpec((1, 1, d), lambda i: (mod_row(i * tm) * N_MOD + 0, 0, 0)),
        ],
        out_specs=pl.BlockSpec((tm, d), lambda i: (i, 0)),
        out_shape=jax.ShapeDtypeStruct((t, d), BF16),
        compiler_params=_cparams(1),
        name="prenorm",
    )(x, g_pre, mod3, mod3)


def _in_proj_kernel(*refs, rope, kv_out, kv_aliased, q_tiles, qk_tiles, k_tiles, v_tiles):
    h_ref, w_ref = refs[:2]
    pos = 2
    if rope:
        cos_ref, sin_ref = refs[pos:pos + 2]
        pos += 2
    pos += kv_aliased
    z_ref = refs[pos]
    pos += 1
    if kv_out:
        k_ref, v_ref = refs[pos:pos + 2]
    j = pl.program_id(1)

    acc = jnp.dot(h_ref[...], w_ref[...].astype(BF16), preferred_element_type=F32)

    if rope:
        z_ref[...] = acc.astype(BF16)

        @pl.when(j < q_tiles)
        def _():
            z_ref[...] = _rope_tile(acc * Q_SCALE, cos_ref[...], sin_ref[...]).astype(BF16)

        @pl.when((j >= q_tiles) & (j < qk_tiles))
        def _():
            z_ref[...] = _rope_tile(acc, cos_ref[...], sin_ref[...]).astype(BF16)
    else:
        z_ref[...] = (acc * jnp.where(j < q_tiles, Q_SCALE, 1.0)).astype(BF16)

    if kv_out:
        n_seq, k_rows, _ = k_ref.shape
        tile_groups = acc.shape[1] // LANES
        lane_groups = tile_groups * (k_tiles[1] - k_tiles[0])
        seq = k_rows // lane_groups
        for jj in range(k_tiles[1] - k_tiles[0]):
            @pl.when(j == k_tiles[0] + jj)
            def _(jj=jj):
                for b in range(n_seq):
                    for g in range(tile_groups):
                        k_ref[b, pl.ds(jj * tile_groups + g, seq, stride=lane_groups), :] = (
                            acc[b * seq:(b + 1) * seq, g * LANES:(g + 1) * LANES])

        @pl.when((j >= v_tiles[0]) & (j < v_tiles[1]))
        def _():
            for b in range(n_seq):
                v_ref[b] = acc[b * seq:(b + 1) * seq, :]


def _in_proj(h, w_in, layer, *, qk_w, pool_cols, rope_tabs=None, kv_bufs=None):
    t, d = h.shape
    n = w_in.shape[2]
    kv_out = kv_bufs is not None
    tm = TM_IN_KV if kv_out else TM_IN
    tn = TN_IN
    nq = qk_w // tn
    n_pool = pool_cols // tn
    rope = rope_tabs is not None

    def w_tile(j):
        return jnp.where(j < 3 * nq, j + n_pool, jnp.where(j < 3 * nq + n_pool, j - 3 * nq, j))

    in_specs = [
        pl.BlockSpec((tm, d), lambda i, j: (i, 0)),
        pl.BlockSpec((None, d, tn), lambda i, j: (layer, 0, w_tile(j))),
    ]
    args = [h, w_in]
    if rope:
        tab_blocks = rope_tabs[0].shape[0] // tm
        tab_spec = pl.BlockSpec((tm, LANES), lambda i, j: (i % tab_blocks, 0))
        in_specs += [tab_spec, tab_spec]
        args += list(rope_tabs)
    out_specs = [pl.BlockSpec((tm, tn), lambda i, j: (i, j))]
    out_shape = [jax.ShapeDtypeStruct((t, n), BF16)]
    aliases = {}
    if kv_out:
        k_buf, v_buf = kv_bufs
        kv_seq = v_buf.shape[2]
        nb = tm // kv_seq
        out_specs += [
            pl.BlockSpec((nb, None) + k_buf.shape[2:], lambda i, j: (i, layer, 0, 0)),
            pl.BlockSpec((nb, None, kv_seq, tn), lambda i, j: (i, layer, 0, jnp.clip(j - 2 * nq, 0, nq - 1))),
        ]
        out_shape += [jax.ShapeDtypeStruct(k_buf.shape, F32), jax.ShapeDtypeStruct(v_buf.shape, F32)]
        aliases = {len(args): 1, len(args) + 1: 2}
        in_specs += [pl.BlockSpec(memory_space=pl.ANY)] * 2
        args += [k_buf, v_buf]
    kern = functools.partial(_in_proj_kernel, rope=rope, kv_out=kv_out, kv_aliased=2 * kv_out,
                             q_tiles=nq, qk_tiles=2 * nq,
                             k_tiles=(nq, 2 * nq), v_tiles=(2 * nq, 3 * nq))
    return pl.pallas_call(
        kern,
        grid=(t // tm, n // tn),
        in_specs=in_specs,
        out_specs=out_specs,
        out_shape=out_shape,
        input_output_aliases=aliases,
        compiler_params=_cparams(2),
        name="in_proj_rope" if rope else "in_proj_kv",
    )(*args)


def _lambda(lq1, lk1, lq2, lk2, lam_init):
    return (jnp.exp(jnp.sum(lq1 * lk1, axis=-1, keepdims=True))
            - jnp.exp(jnp.sum(lq2 * lk2, axis=-1, keepdims=True)) + lam_init)


def _diff_attn_head(q, ks, vs, lam, gain, lam_init):
    def softmax_parts(m):
        cols = slice(m * HEAD_DIM, (m + 1) * HEAD_DIM)
        ss = [lax.dot_general(q[:, cols], k[:, cols], (((1,), (1,)), ((), ())),
                              preferred_element_type=F32) for k in ks]
        mx = functools.reduce(jnp.maximum, [jnp.max(s, axis=-1, keepdims=True) for s in ss])
        es = [jnp.exp2(s - mx) for s in ss]
        den = functools.reduce(jnp.add, [jnp.sum(e, axis=-1, keepdims=True) for e in es])
        return es, 1.0 / den

    e1, inv1 = softmax_parts(0)
    e2, inv2 = softmax_parts(1)
    w2 = lam * inv2
    o = functools.reduce(jnp.add, [
        jnp.dot((a * inv1 - b * w2).astype(BF16), v, preferred_element_type=F32)
        for a, b, v in zip(e1, e2, vs)])
    return _rms(o) * gain * (1.0 - lam_init)


def _attn_ctx_kernel(q_ref, k_ref, v_ref, lq1, lk1, lq2, lk2, g_ref, o_ref, *, lam_init):
    lam = _lambda(lq1[...], lk1[...], lq2[...], lk2[...], lam_init)
    gain = g_ref[...]
    for h in range(N_HEADS):
        cols = slice(h * V_DIM, (h + 1) * V_DIM)
        o = _diff_attn_head(q_ref[:, cols], [k_ref[:, cols]], [v_ref[:, cols]], lam, gain, lam_init)
        o_ref[:, cols] = o.astype(BF16)


def _attn_lat_kernel(q_ref, k_ref, v_ref, ck_ref, cv_ref, lq1, lk1, lq2, lk2, g_ref, o_ref, *,
                     lam_init):
    lam = _lambda(lq1[...], lk1[...], lq2[...], lk2[...], lam_init)
    q = q_ref[...]
    chunks = [(ck_ref[...].astype(BF16), cv_ref[...].astype(BF16))]
    for c in range(k_ref.shape[0] // KEY_CHUNK):
        rows = slice(c * KEY_CHUNK, (c + 1) * KEY_CHUNK)
        chunks.append((k_ref[rows, :], v_ref[rows, :]))
    outs = []
    for m in range(2):
        cols = slice(m * HEAD_DIM, (m + 1) * HEAD_DIM)
        mx = den = acc = None
        for kc, vc in chunks:
            s = lax.dot_general(q[:, cols], kc[:, cols], (((1,), (1,)), ((), ())),
                                preferred_element_type=F32)
            s_max = jnp.max(s, axis=-1, keepdims=True)
            if mx is None:
                mx = s_max
                e = jnp.exp2(s - mx)
                den = jnp.sum(e, axis=-1, keepdims=True)
                acc = jnp.dot(e.astype(BF16), vc, preferred_element_type=F32)
            else:
                mx_new = jnp.maximum(mx, s_max)
                alpha = jnp.exp2(mx - mx_new)
                e = jnp.exp2(s - mx_new)
                den = alpha * den + jnp.sum(e, axis=-1, keepdims=True)
                acc = alpha * acc + jnp.dot(e.astype(BF16), vc, preferred_element_type=F32)
                mx = mx_new
        outs.append(acc * (1.0 / den))
    o = outs[0] - lam * outs[1]
    o_ref[...] = (_rms(o) * g_ref[...] * (1.0 - lam_init)).astype(BF16)


def _lam_specs(layer, n_axes):
    idx = (lambda b: (layer, 0, 0)) if n_axes == 1 else (lambda b, h, qi: (layer, 0, 0))
    return [pl.BlockSpec((None, 1, HEAD_DIM), idx)] * 4 + [pl.BlockSpec((None, 1, V_DIM), idx)]


def _attention_ctx(z, lam_args, layer, lam_init, *, seq, width):
    t = z.shape[0]
    kern = functools.partial(_attn_ctx_kernel, lam_init=lam_init)
    return pl.pallas_call(
        kern,
        grid=(t // seq,),
        in_specs=[pl.BlockSpec((seq, width), lambda b: (b, 0)),
                  pl.BlockSpec((seq, width), lambda b: (b, 1)),
                  pl.BlockSpec((seq, width), lambda b: (b, 2))] + _lam_specs(layer, 1),
        out_specs=pl.BlockSpec((seq, width), lambda b: (b, 0)),
        out_shape=jax.ShapeDtypeStruct((t, width), BF16),
        compiler_params=_cparams(1),
        name="attention_ctx",
    )(z, z, z, *lam_args)


def _attention_lat(z, cache_k, cache_v, lam_args, layer, lam_init, *, seq, width):
    t = z.shape[0]
    nb = t // seq
    nq = seq // TQ_LAT
    past = cache_k.shape[2]
    kern = functools.partial(_attn_lat_kernel, lam_init=lam_init)
    return pl.pallas_call(
        kern,
        grid=(nb, N_HEADS, nq),
        in_specs=[
            pl.BlockSpec((TQ_LAT, V_DIM), lambda b, h, qi: (b * nq + qi, h)),
            pl.BlockSpec((seq, V_DIM), lambda b, h, qi: (b, N_HEADS + h)),
            pl.BlockSpec((seq, V_DIM), lambda b, h, qi: (b, 2 * N_HEADS + h)),
            pl.BlockSpec((None, None, past, V_DIM), lambda b, h, qi: (b, layer, 0, h)),
            pl.BlockSpec((None, None, past, V_DIM), lambda b, h, qi: (b, layer, 0, h)),
        ] + _lam_specs(layer, 3),
        out_specs=pl.BlockSpec((TQ_LAT, V_DIM), lambda b, h, qi: (b * nq + qi, h)),
        out_shape=jax.ShapeDtypeStruct((t, width), BF16),
        compiler_params=_cparams(3),
        name="attention_lat",
    )(z, z, z, cache_k, cache_v, *lam_args)


def _seq_masks(row0, rows, seq):
    p = (row0 + lax.broadcasted_iota(jnp.int32, (rows, 1), 0)) % seq
    return (p != 0).astype(F32), (p != seq - 1).astype(F32)


def _pool_mixer(u_ref, upad_ref, pw_ref, ps_ref, lhs_ref, row0, seq):
    halo = BF16_SUBLANES
    tm = u_ref.shape[0]
    pg = pw_ref.shape[-1]
    span = POOL_SUB + 2 * halo
    t_idx = lax.broadcasted_iota(jnp.int32, (POOL_SUB, span), 0)
    c_idx = lax.broadcasted_iota(jnp.int32, (POOL_SUB, span), 1)
    off = c_idx - halo - t_idx
    t_col = lax.broadcasted_iota(jnp.int32, (POOL_SUB, 1), 0)
    for r in range(tm // POOL_SUB):
        pos0 = (row0 + r * POOL_SUB) % seq
        pos_c = pos0 + c_idx - halo
        in_seq = (pos_c >= 0) & (pos_c < seq)
        pos_t = pos0 + t_col
        for gi, w in enumerate(POOL_WINDOWS):
            half = w // 2
            band = ((off >= -half) & (off < w - half) & in_seq).astype(F32).astype(BF16)
            cnt = jnp.minimum(pos_t + (w - half), seq) - jnp.maximum(pos_t - half, 0)
            cols = slice(gi * pg, (gi + 1) * pg)
            win = jnp.dot(band, upad_ref[r * POOL_SUB:r * POOL_SUB + span, cols],
                          preferred_element_type=F32)
            mean = win / cnt.astype(F32)
            dlt = mean - u_ref[r * POOL_SUB:(r + 1) * POOL_SUB, cols].astype(F32)
            y = jnp.dot(dlt.astype(BF16), pw_ref[gi], preferred_element_type=F32) * ps_ref[:, cols]
            lhs_ref[r * POOL_SUB:(r + 1) * POOL_SUB, cols] = y.astype(BF16)


def _branches_kernel(u_ref, up_ref, un_ref, cb_ref, ccx_ref, ccxp_ref, ccxn_ref, ya_ref, g_ref,
                     wp_ref, wa_ref, wc_ref, pw_ref, ps_ref, cw_ref, cbias_ref, o_ref,
                     acc_ref, lhs_ref, upad_ref, ppad_ref, *, seq):
    i = pl.program_id(0)
    s = pl.program_id(1)
    tm = u_ref.shape[0]
    halo = BF16_SUBLANES
    cwid = cb_ref.shape[1]
    row0 = i * tm

    def gated(lhs, w):
        gate = jax.nn.sigmoid(g_ref[...].astype(F32))
        return gate * jnp.dot(lhs, w, preferred_element_type=F32)

    @pl.when(s == 0)
    def _():
        upad_ref[0:halo, :] = up_ref[...]
        upad_ref[halo:halo + tm, :] = u_ref[...]
        upad_ref[halo + tm:, :] = un_ref[...]
        _pool_mixer(u_ref, upad_ref, pw_ref, ps_ref, lhs_ref, row0, seq)
        acc_ref[...] = gated(lhs_ref[...], wp_ref[...])

    @pl.when((s == 1) | (s == 2))
    def _():
        kc = ya_ref.shape[1]
        acc_ref[...] += gated(ya_ref[...], wa_ref[pl.ds(pl.multiple_of((s - 1) * kc, kc), kc), :])

    @pl.when(s == 3)
    def _():
        def prod(ref):
            v = ref[...].astype(F32)
            return v[:, :cwid] * v[:, cwid:]

        ppad_ref[0:halo, :] = prod(ccxp_ref)
        ppad_ref[halo:halo + tm, :] = prod(ccx_ref)
        ppad_ref[halo + tm:, :] = prod(ccxn_ref)
        m_prev, m_next = _seq_masks(row0, tm, seq)
        p_prev = ppad_ref[halo - 1:halo - 1 + tm, :] * m_prev
        p_next = ppad_ref[halo + 1:halo + 1 + tm, :] * m_next
        conv = (p_prev * cw_ref[0:1, :] + ppad_ref[halo:halo + tm, :] * cw_ref[1:2, :]
                + p_next * cw_ref[2:3, :] + cbias_ref[...])
        y = cb_ref[...].astype(F32) * conv
        o_ref[...] = (acc_ref[...] + gated(y.astype(BF16), wc_ref[...])).astype(BF16)


def _branches(z, y_attn, w_br, pool_w, pool_scale, conv_w, conv_b, layer, *, seq, d, pool_w_cols,
              conv_cols, col0):
    t = z.shape[0]
    tm = TM_MIX
    halo = BF16_SUBLANES
    hb = tm // halo
    last_hb = t // halo - 1
    kc = w_br[1].shape[1] // 2
    resident = lambda w: pl.BlockSpec((None,) + w.shape[1:], lambda i, s: (layer, 0, 0),
                                      pipeline_mode=pl.Buffered(1))
    prev = lambda i, s: jnp.maximum(i * hb - 1, 0)
    nxt = lambda i, s: jnp.minimum((i + 1) * hb, last_hb)
    pool_blk = col0["pool"] // pool_w_cols
    cb_blk = col0["cb"] // conv_cols
    ccx_blk = col0["ccx"] // (2 * conv_cols)
    g_blk = col0["g"] // d
    kern = functools.partial(_branches_kernel, seq=seq)
    return pl.pallas_call(
        kern,
        grid=(t // tm, 4),
        in_specs=[
            pl.BlockSpec((tm, pool_w_cols), lambda i, s: (i, pool_blk)),
            pl.BlockSpec((halo, pool_w_cols), lambda i, s: (prev(i, s), pool_blk)),
            pl.BlockSpec((halo, pool_w_cols), lambda i, s: (nxt(i, s), pool_blk)),
            pl.BlockSpec((tm, conv_cols), lambda i, s: (i, cb_blk)),
            pl.BlockSpec((tm, 2 * conv_cols), lambda i, s: (i, ccx_blk)),
            pl.BlockSpec((halo, 2 * conv_cols), lambda i, s: (prev(i, s), ccx_blk)),
            pl.BlockSpec((halo, 2 * conv_cols), lambda i, s: (nxt(i, s), ccx_blk)),
            pl.BlockSpec((tm, kc), lambda i, s: (i, jnp.clip(s - 1, 0, 1))),
            pl.BlockSpec((tm, d), lambda i, s: (i, g_blk + (s + 1) // 2)),
            resident(w_br[0]), resident(w_br[1]), resident(w_br[2]),
            pl.BlockSpec((None,) + pool_w.shape[1:], lambda i, s: (layer, 0, 0, 0)),
            pl.BlockSpec((None, 1, pool_w_cols), lambda i, s: (layer, 0, 0)),
            pl.BlockSpec((None, 3, conv_cols), lambda i, s: (layer, 0, 0)),
            pl.BlockSpec((None, 1, conv_cols), lambda i, s: (layer, 0, 0)),
        ],
        out_specs=pl.BlockSpec((tm, d), lambda i, s: (i, 0)),
        out_shape=jax.ShapeDtypeStruct((t, d), BF16),
        scratch_shapes=[
            pltpu.VMEM((tm, d), F32),
            pltpu.VMEM((tm, pool_w_cols), BF16),
            pltpu.VMEM((tm + 2 * halo, pool_w_cols), BF16),
            pltpu.VMEM((tm + 2 * halo, conv_cols), F32),
        ],
        compiler_params=_cparams(2),
        name="branches",
    )(z, z, z, z, z, z, z, y_attn, z, *w_br, pool_w, pool_scale, conv_w, conv_b)


def _out_proj_kernel(m_ref, w_ref, x_ref, gt_ref, g_ref, gn_ref, scn_ref, shn_ref, o_ref, h_ref):
    half = m_ref.shape[0] // 2
    for r in range(2):
        rows = slice(r * half, (r + 1) * half)
        m = jnp.dot(m_ref[rows, :], w_ref[...], preferred_element_type=F32)
        x1 = x_ref[rows, :] + _rms(m) * (gt_ref[0] * g_ref[...])
        o_ref[rows, :] = x1
        h_ref[rows, :] = _modulated_norm(x1, gn_ref, scn_ref, shn_ref)


def _out_proj(merged, w_out, x, mod3, g_post, g_pre_ffn, layer, mod_row):
    t, d = x.shape
    tm = TM_MIX
    mod_idx = lambda m: (lambda i: (mod_row(i * tm) * N_MOD + m, 0, 0))
    row = pl.BlockSpec((tm, d), lambda i: (i, 0))
    vec = pl.BlockSpec((1, d), lambda i: (0, 0))
    return pl.pallas_call(
        _out_proj_kernel,
        grid=(t // tm,),
        in_specs=[row, pl.BlockSpec((None, d, d), lambda i: (layer, 0, 0)), row,
                  pl.BlockSpec((1, 1, d), mod_idx(2)), vec, vec,
                  pl.BlockSpec((1, 1, d), mod_idx(4)), pl.BlockSpec((1, 1, d), mod_idx(3))],
        out_specs=[row, row],
        out_shape=[jax.ShapeDtypeStruct((t, d), F32), jax.ShapeDtypeStruct((t, d), BF16)],
        compiler_params=_cparams(1),
        name="out_proj",
    )(merged, w_out, x, mod3, g_post, g_pre_ffn, mod3, mod3)


def _gelu_tanh(x):
    c = math.sqrt(2.0 / math.pi)
    hx = 0.5 * x
    return hx * jnp.tanh(x * (c + (c * 0.044715) * (x * x))) + hx


def _ffn_layout(tm, seq):
    halo = BF16_SUBLANES
    pad = 8
    if seq > tm:
        assert seq % tm == 0 and (tm + 2 * halo) % (2 * halo) == 0
        rows = tm + 2 * halo
        return (rows, rows, [(0, rows // 2, 0), (rows // 2, rows // 2, rows // 2)],
                [(halo, tm // 2), (halo + tm // 2, tm // 2)], [])
    assert tm % seq == 0
    nseg = tm // seq
    pieces = [(s * seq, seq, pad + s * (seq + pad)) for s in range(nseg)]
    outs = [(pad + s * (seq + pad), seq) for s in range(nseg)]
    zero_rows = [(s * (seq + pad), pad) for s in range(nseg + 1)]
    return tm, pad + nseg * (seq + pad), pieces, outs, zero_rows


def _ffn_kernel(*refs, seq, with_halo, emit_next):
    refs = list(refs)
    h_ref = refs.pop(0)
    hp_ref, hn_ref = (refs.pop(0), refs.pop(0)) if with_halo else (None, None)
    x_ref, gt_ref, gpost_ref = refs.pop(0), refs.pop(0), refs.pop(0)
    next_refs = (refs.pop(0), refs.pop(0), refs.pop(0)) if emit_next else None
    wg_ref, wv_ref, cp_ref, wd_ref = refs[:4]
    refs = refs[4:]
    o_ref = refs.pop(0)
    hnext_ref = refs.pop(0) if emit_next else None
    lhs_ref = refs.pop(0) if with_halo else h_ref
    acc_ref, u_ref = refs
    i = pl.program_id(0)
    f = pl.program_id(1)
    tm = x_ref.shape[0]
    halo = BF16_SUBLANES
    _, _, pieces, outs, zero_rows = _ffn_layout(tm, seq)

    @pl.when(f == 0)
    def _():
        if with_halo:
            has_prev = (i * tm) % seq != 0
            has_next = ((i + 1) * tm) % seq != 0
            lhs_ref[0:halo, :] = jnp.where(has_prev, hp_ref[...], jnp.zeros_like(hp_ref))
            lhs_ref[halo:halo + tm, :] = h_ref[...]
            lhs_ref[halo + tm:, :] = jnp.where(has_next, hn_ref[...], jnp.zeros_like(hn_ref))

        def zero_rows_of_acc(r, carry):
            acc_ref[pl.ds(pl.multiple_of(r * ZERO_ROWS, ZERO_ROWS), ZERO_ROWS), :] = jnp.zeros(
                (ZERO_ROWS, acc_ref.shape[1]), F32)
            return carry

        lax.fori_loop(0, tm // ZERO_ROWS, zero_rows_of_acc, 0)
        for start, rows in zero_rows:
            u_ref[:, start:start + rows, :] = jnp.zeros((2, rows, u_ref.shape[2]), F32)

    def conv_half(slot, w_ref):
        for lhs0, rows, u0 in pieces:
            u_ref[slot, u0:u0 + rows, :] = jnp.dot(lhs_ref[lhs0:lhs0 + rows, :], w_ref[...],
                                                   preferred_element_type=F32)
        p = slot * CONV_PARAM_ROWS
        return [
            u_ref[slot, u0 - 1:u0 - 1 + rows, :] * cp_ref[p:p + 1, :]
            + u_ref[slot, u0:u0 + rows, :] * cp_ref[p + 1:p + 2, :]
            + u_ref[slot, u0 + 1:u0 + 1 + rows, :] * cp_ref[p + 2:p + 3, :] + cp_ref[p + 3:p + 4, :]
            for u0, rows in outs]

    row = 0
    for gate, val in zip(conv_half(0, wg_ref), conv_half(1, wv_ref)):
        a = (_gelu_tanh(gate) * val).astype(BF16)
        acc_ref[row:row + a.shape[0], :] += jnp.dot(a, wd_ref[...], preferred_element_type=F32)
        row += a.shape[0]

    @pl.when(f == pl.num_programs(1) - 1)
    def _():
        x2 = x_ref[...] + _rms(acc_ref[...]) * (gt_ref[0] * gpost_ref[...])
        o_ref[...] = x2
        if emit_next:
            hnext_ref[...] = _modulated_norm(x2, *next_refs)


def _ffn(h, x, mod3, g_post, w_up, conv_params, w_down, layer, mod_row, *, seq, next_norm=None):
    t, d = x.shape
    tm, tf = TM_MIX, TF_FFN
    halo = BF16_SUBLANES
    hb = tm // halo
    last_hb = t // halo - 1
    nf = w_down.shape[1] // tf
    with_halo = seq > tm
    emit_next = next_norm is not None
    lhs_rows, u_rows, _, _, _ = _ffn_layout(tm, seq)
    row = pl.BlockSpec((tm, d), lambda i, f: (i, 0))
    vec = pl.BlockSpec((1, d), lambda i, f: (0, 0))

    def mod_spec(row_fn, m):
        return pl.BlockSpec((1, 1, d), lambda i, f: (row_fn(i * tm) * N_MOD + m, 0, 0))

    in_specs, args = [row], [h]
    if with_halo:
        in_specs += [pl.BlockSpec((halo, d), lambda i, f: (jnp.maximum(i * hb - 1, 0), 0)),
                     pl.BlockSpec((halo, d), lambda i, f: (jnp.minimum((i + 1) * hb, last_hb), 0))]
        args += [h, h]
    in_specs += [row, mod_spec(mod_row, 5), vec]
    args += [x, mod3, g_post]
    if emit_next:
        in_specs += [vec, mod_spec(next_norm[1], 1), mod_spec(next_norm[1], 0)]
        args += [next_norm[0], mod3, mod3]
    up_tile = pl.BlockSpec((None, None, d, tf), lambda i, f: (layer, f, 0, 0))
    in_specs += [up_tile, up_tile, pl.BlockSpec((None, 2 * CONV_PARAM_ROWS, tf), lambda i, f: (layer, 0, f)),
                 pl.BlockSpec((None, tf, d), lambda i, f: (layer, f, 0))]
    args += [*w_up, conv_params, w_down]
    scratch = [pltpu.VMEM((lhs_rows, d), BF16)] if with_halo else []
    scratch += [pltpu.VMEM((tm, d), F32), pltpu.VMEM((2, u_rows, tf), F32)]
    kern = functools.partial(_ffn_kernel, seq=seq, with_halo=with_halo, emit_next=emit_next)
    return pl.pallas_call(
        kern,
        grid=(t // tm, nf),
        in_specs=in_specs,
        out_specs=[row, row] if emit_next else [row],
        out_shape=[jax.ShapeDtypeStruct((t, d), F32)] + ([jax.ShapeDtypeStruct((t, d), BF16)] if emit_next else []),
        scratch_shapes=scratch,
        compiler_params=_cparams(2),
        name="ffn_halo" if with_halo else "ffn",
    )(*args)


def _prep_up_kernel(w_ref, g_ref, v_ref, *, d_ff):
    n_tiles, rows, tf = g_ref.shape
    for half, o_ref in enumerate((g_ref, v_ref)):
        for f in range(n_tiles):
            lo, hi = f * tf, min((f + 1) * tf, d_ff)
            o_ref[f, :, 0:hi - lo] = w_ref[:, half * d_ff + lo:half * d_ff + hi].astype(BF16)
            if hi - lo < tf:
                o_ref[f, :, hi - lo:] = jnp.zeros((rows, tf - (hi - lo)), BF16)


def _prep_down_kernel(w_ref, o_ref, *, d_ff):
    o_ref[0:d_ff, :] = w_ref[...].astype(BF16)
    o_ref[d_ff:, :] = jnp.zeros((o_ref.shape[0] - d_ff, o_ref.shape[1]), BF16)


def _prep_ffn_weights(w_up, w_down, d_ffp):
    depth, d, _ = w_up.shape
    d_ff = w_down.shape[1]
    n_tiles = d_ffp // TF_FFN
    half = jax.ShapeDtypeStruct((depth, n_tiles, d, TF_FFN), BF16)
    up = pl.pallas_call(
        functools.partial(_prep_up_kernel, d_ff=d_ff),
        grid=(depth, d // PREP_BLOCK),
        in_specs=[pl.BlockSpec((None, PREP_BLOCK, 2 * d_ff), lambda l, r: (l, r, 0))],
        out_specs=[pl.BlockSpec((None, n_tiles, PREP_BLOCK, TF_FFN), lambda l, r: (l, 0, r, 0))] * 2,
        out_shape=[half, half],
        compiler_params=_cparams(2),
        name="prep_w_up",
    )(w_up)
    down = pl.pallas_call(
        functools.partial(_prep_down_kernel, d_ff=d_ff),
        grid=(depth, d // PREP_BLOCK),
        in_specs=[pl.BlockSpec((None, d_ff, PREP_BLOCK), lambda l, c: (l, 0, c))],
        out_specs=pl.BlockSpec((None, d_ffp, PREP_BLOCK), lambda l, c: (l, 0, c)),
        out_shape=jax.ShapeDtypeStruct((depth, d_ffp, d), BF16),
        compiler_params=_cparams(2),
        name="prep_w_down",
    )(w_down)
    return tuple(up), down


def _rope_tables(seq):
    rows = seq // GRID_W
    row = jnp.repeat(jnp.arange(rows), GRID_W)
    col = jnp.tile(jnp.arange(GRID_W), rows)
    inv = ROPE_BASE ** (-jnp.arange(ROPE_FREQS, dtype=F32) / ROPE_FREQS)
    ang = jnp.stack([row, col], axis=-1).astype(F32)[:, :, None] * inv
    cos, sin = jnp.cos(ang), jnp.sin(ang)
    cos_t = jnp.concatenate([cos[:, 0], cos[:, 0], cos[:, 1], cos[:, 1]], axis=-1)
    sin_t = jnp.concatenate([-sin[:, 0], sin[:, 0], -sin[:, 1], sin[:, 1]], axis=-1)
    return cos_t, sin_t


def _pad_axis(a, axis, size):
    pad = [(0, 0)] * a.ndim
    pad[axis] = (0, size - a.shape[axis])
    return jnp.pad(a, pad)


def kernel(x_prompt, x_sample, cache_k, cache_v, c, c_ctx, w_mod, b_mod, g_pre_mix, g_post_mix,
           g_pre_ffn, g_post_ffn, w_in, pool_w, pool_scale, lambda_q1, lambda_k1, lambda_q2, lambda_k2,
           g_subln, conv_w, conv_b, w_br_pool, w_br_attn, w_br_conv, w_out, ffn_w_up, ffn_conv_w,
           ffn_conv_b, ffn_w_down):
    batch, seq, d = x_prompt.shape
    dec_batch, dec_seq, _ = x_sample.shape
    depth = w_in.shape[0]
    past = cache_k.shape[2]
    qk_w = N_HEADS * 2 * HEAD_DIM
    v_w = N_HEADS * V_DIM
    pool_cols = pool_scale.shape[1]
    conv_cols = conv_b.shape[1]
    d_ff = ffn_w_down.shape[1]
    d_ffp = -(-d_ff // TF_FFN) * TF_FFN
    assert qk_w == v_w and dec_batch + 1 <= MOD_ROWS

    col0 = {"pool": 3 * qk_w, "cb": 3 * qk_w + pool_cols, "ccx": 3 * qk_w + pool_cols + conv_cols,
            "g": 3 * qk_w + pool_cols + 3 * conv_cols}
    w_br = tuple(w.astype(BF16) for w in (w_br_pool, w_br_attn, w_br_conv))
    w_out_b = w_out.astype(BF16)
    pool_w_b = pool_w.astype(BF16)

    def padded_half(a, start, axis, dtype):
        part = lax.slice_in_dim(a, start, start + d_ff, axis=axis).astype(dtype)
        zeros = jnp.zeros(part.shape[:axis] + (d_ffp - d_ff,) + part.shape[axis + 1:], dtype)
        return jnp.concatenate([part, zeros], axis=axis)

    w_up, w_down = _prep_ffn_weights(ffn_w_up, ffn_w_down, d_ffp)
    fcp = jnp.concatenate([blk for start in (0, d_ff) for blk in (
        padded_half(ffn_conv_w, start, 2, F32), padded_half(ffn_conv_b[:, None, :], start, 2, F32))], axis=1)

    cvec = _pad_axis(jnp.concatenate([c_ctx[None, :], c], axis=0), 0, MOD_ROWS)
    mod = _modulation(cvec, w_mod, b_mod)
    mod3 = mod.reshape(depth * MOD_ROWS * N_MOD, 1, d)

    rope_tabs = _rope_tables(dec_seq)
    ck = cache_k.reshape(dec_batch, depth, past, qk_w)
    cv = cache_v.reshape(dec_batch, depth, past, v_w)
    row1 = lambda a: a[:, None, :]
    lam_args = (row1(lambda_q1), row1(lambda_k1), row1(lambda_q2), row1(lambda_k2), row1(g_subln))
    pool_scale3, conv_b3 = row1(pool_scale), row1(conv_b)

    xc = x_prompt.reshape(batch * seq, d)
    xl = x_sample.reshape(dec_batch * dec_seq, d)
    ctx_row = lambda l: (lambda r: l * MOD_ROWS)
    lat_row = lambda l: (lambda r: l * MOD_ROWS + 1 + r // dec_seq)
    gl = lambda a, l: a[l][None, :]
    hc = _prenorm(xc, mod3, gl(g_pre_mix, 0), ctx_row(0))
    hl = _prenorm(xl, mod3, gl(g_pre_mix, 0), lat_row(0))
    kv_bufs = (jnp.zeros((batch, depth, seq * (qk_w // LANES), LANES), F32),
               jnp.zeros((batch, depth, seq, v_w), F32))
    for l in range(depth):
        lam_init = 0.8 - 0.6 * math.exp(-0.3 * l)
        last = l == depth - 1
        mix = functools.partial(_branches, w_br=w_br, pool_w=pool_w_b, pool_scale=pool_scale3,
                                conv_w=conv_w, conv_b=conv_b3, layer=l, d=d, pool_w_cols=pool_cols,
                                conv_cols=conv_cols, col0=col0)
        ffn = functools.partial(_ffn, mod3=mod3, g_post=gl(g_post_ffn, l), w_up=w_up, conv_params=fcp,
                                w_down=w_down, layer=l)
        out_proj = functools.partial(_out_proj, w_out=w_out_b, mod3=mod3, g_post=gl(g_post_mix, l),
                                     g_pre_ffn=gl(g_pre_ffn, l), layer=l)
        next_norm = lambda row: None if last else (gl(g_pre_mix, l + 1), row(l + 1))

        z, *kv_bufs = _in_proj(hc, w_in, l, qk_w=qk_w, pool_cols=pool_cols, kv_bufs=kv_bufs)
        ya = _attention_ctx(z, lam_args, l, lam_init, seq=seq, width=v_w)
        merged = mix(z, ya, seq=seq)
        xc, h2 = out_proj(merged, x=xc, mod_row=ctx_row(l))
        res = ffn(h2, xc, mod_row=ctx_row(l), seq=seq, next_norm=next_norm(ctx_row))
        xc, hc = res if not last else (res[0], None)

        z = _in_proj(hl, w_in, l, qk_w=qk_w, pool_cols=pool_cols, rope_tabs=rope_tabs)[0]
        ya = _attention_lat(z, ck, cv, lam_args, l, lam_init, seq=dec_seq, width=v_w)
        merged = mix(z, ya, seq=dec_seq)
        xl, h2 = out_proj(merged, x=xl, mod_row=lat_row(l))
        res = ffn(h2, xl, mod_row=lat_row(l), seq=dec_seq, next_norm=next_norm(lat_row))
        xl, hl = res if not last else (res[0], None)

    return (xc.reshape(batch, seq, d), xl.reshape(dec_batch, dec_seq, d),
            kv_bufs[0].reshape(batch, depth, seq, N_HEADS, 2, HEAD_DIM),
            kv_bufs[1].reshape(batch, depth, seq, N_HEADS, V_DIM))
```

```python
import functools
import math

import jax
import jax.numpy as jnp
from jax import lax
from jax.experimental import pallas as pl
from jax.experimental.pallas import tpu as pltpu

F32 = jnp.float32
BF16 = jnp.bfloat16

GRID_W = 64
N_HEADS = 8
HEAD_DIM = 128
V_DIM = 2 * HEAD_DIM
POOL_WINDOWS = (2, 4, 8, 16)
ROPE_FREQS = HEAD_DIM // 4
ROPE_BASE = 10000.0
EPS = 1e-6
Q_SCALE = HEAD_DIM ** -0.5 * math.log2(math.e)
N_MOD = 6

LANES = 128
F32_SUBLANES = 8
BF16_SUBLANES = 16
VMEM_LIMIT_BYTES = 56 * 1024 * 1024
MOD_ROWS = 8

TM_IN = 2048
TM_IN_KV = 1024
K_PITCH = 20
TN_IN = 512
TM_MIX = 512
POOL_SUB = 256
TQ_LAT = 512
TF_FFN = 512
KEY_CHUNK = 512
TN_MOD = 1024
ZERO_ROWS = 128
CONV_PARAM_ROWS = 4
PREP_BLOCK = 256


def _cparams(n_axes):
    return pltpu.CompilerParams(dimension_semantics=("arbitrary",) * n_axes,
                                vmem_limit_bytes=VMEM_LIMIT_BYTES)


def _rms(x):
    return x * lax.rsqrt(jnp.mean(x * x, axis=-1, keepdims=True) + EPS)


def _modulated_norm(x, gain_ref, sc_ref, sh_ref):
    return (_rms(x) * (gain_ref[...] * (1.0 + sc_ref[0])) + sh_ref[0]).astype(BF16)


def _mod_kernel(c_ref, w_ref, b_ref, o_ref):
    c = c_ref[...]
    s = (c * jax.nn.sigmoid(c)).astype(BF16)
    o_ref[...] = jnp.dot(s, w_ref[...].astype(BF16), preferred_element_type=F32) + b_ref[...]


def _modulation(cvec, w_mod, b_mod):
    depth, d, n = w_mod.shape
    return pl.pallas_call(
        _mod_kernel,
        grid=(depth, n // TN_MOD),
        in_specs=[
            pl.BlockSpec((MOD_ROWS, d), lambda l, j: (0, 0)),
            pl.BlockSpec((None, d, TN_MOD), lambda l, j: (l, 0, j)),
            pl.BlockSpec((None, 1, TN_MOD), lambda l, j: (l, 0, j)),
        ],
        out_specs=pl.BlockSpec((None, MOD_ROWS, TN_MOD), lambda l, j: (l, 0, j)),
        out_shape=jax.ShapeDtypeStruct((depth, MOD_ROWS, n), F32),
        compiler_params=_cparams(2),
        name="modulation",
    )(cvec, w_mod, b_mod.reshape(depth, 1, n))


def _rope_tile(acc, c, s):
    lane = lax.broadcasted_iota(jnp.int32, c.shape, 1)
    upper = (lane & ROPE_FREQS) != 0
    outs = []
    for g in range(acc.shape[1] // LANES):
        xg = acc[:, g * LANES:(g + 1) * LANES]
        partner = jnp.where(upper, pltpu.roll(xg, ROPE_FREQS, axis=1),
                            pltpu.roll(xg, LANES - ROPE_FREQS, axis=1))
        outs.append(xg * c + partner * s)
    return jnp.concatenate(outs, axis=1)


def _prenorm_kernel(x_ref, g_ref, sc_ref, sh_ref, h_ref):
    h_ref[...] = _modulated_norm(x_ref[...], g_ref, sc_ref, sh_ref)


def _prenorm(x, mod3, g_pre, mod_row):
    t, d = x.shape
    tm = TM_MIX
    return pl.pallas_call(
        _prenorm_kernel,
        grid=(t // tm,),
        in_specs=[
            pl.BlockSpec((tm, d), lambda i: (i, 0)),
            pl.BlockSpec((1, d), lambda i: (0, 0)),
            pl.BlockSpec((1, 1, d), lambda i: (mod_row(i * tm) * N_MOD + 1, 0, 0)),
            pl.BlockSpec((1, 1, d), lambda i: (mod_row(i * tm) * N_MOD + 0, 0, 0)),
        ],
        out_specs=pl.BlockSpec((tm, d), lambda i: (i, 0)),
        out_shape=jax.ShapeDtypeStruct((t, d), BF16),
        compiler_params=_cparams(1),
        name="prenorm",
    )(x, g_pre, mod3, mod3)


def _in_proj_kernel(*refs, rope, kv_out, kv_aliased, q_tiles, qk_tiles, k_tiles, v_tiles):
    h_ref, w_ref = refs[:2]
    pos = 2
    if rope:
        cos_ref, sin_ref = refs[pos:pos + 2]
        pos += 2
    pos += kv_aliased
    z_ref = refs[pos]
    pos += 1
    if kv_out:
        k_ref, v_ref, ks_ref = refs[pos:pos + 3]
    j = pl.program_id(1)

    acc = jnp.dot(h_ref[...], w_ref[...].astype(BF16), preferred_element_type=F32)

    if rope:
        z_ref[...] = acc.astype(BF16)

        @pl.when(j < q_tiles)
        def _():
            z_ref[...] = _rope_tile(acc * Q_SCALE, cos_ref[...], sin_ref[...]).astype(BF16)

        @pl.when((j >= q_tiles) & (j < qk_tiles))
        def _():
            z_ref[...] = _rope_tile(acc, cos_ref[...], sin_ref[...]).astype(BF16)
    else:
        z_ref[...] = (acc * jnp.where(j < q_tiles, Q_SCALE, 1.0)).astype(BF16)

    if kv_out:
        n_seq, k_rows, _ = k_ref.shape
        tile_groups = acc.shape[1] // LANES
        lane_groups = tile_groups * (k_tiles[1] - k_tiles[0])
        seq = k_rows // lane_groups
        n_k = k_tiles[1] - k_tiles[0]
        for jj in range(n_k):
            @pl.when(j == k_tiles[0] + jj)
            def _(jj=jj):
                for b in range(n_seq):
                    for g in range(tile_groups):
                        ks_ref[b, pl.ds(jj * tile_groups + g, seq, stride=K_PITCH), :] = (
                            acc[b * seq:(b + 1) * seq, g * LANES:(g + 1) * LANES])
                if jj == n_k - 1:
                    def compact(p, carry):
                        for b in range(n_seq):
                            k_ref[b, pl.ds(pl.multiple_of(p * lane_groups, lane_groups), lane_groups), :] = (
                                ks_ref[b, pl.ds(p * K_PITCH, lane_groups), :])
                        return carry

                    lax.fori_loop(0, seq, compact, 0)

        @pl.when((j >= v_tiles[0]) & (j < v_tiles[1]))
        def _():
            for b in range(n_seq):
                v_ref[b] = acc[b * seq:(b + 1) * seq, :]


def _in_proj(h, w_in, layer, *, qk_w, pool_cols, rope_tabs=None, kv_bufs=None):
    t, d = h.shape
    n = w_in.shape[2]
    kv_out = kv_bufs is not None
    tm = TM_IN_KV if kv_out else TM_IN
    tn = TN_IN
    nq = qk_w // tn
    n_pool = pool_cols // tn
    rope = rope_tabs is not None

    def w_tile(j):
        return jnp.where(j < 3 * nq, j + n_pool, jnp.where(j < 3 * nq + n_pool, j - 3 * nq, j))

    in_specs = [
        pl.BlockSpec((tm, d), lambda i, j: (i, 0)),
        pl.BlockSpec((None, d, tn), lambda i, j: (layer, 0, w_tile(j))),
    ]
    args = [h, w_in]
    if rope:
        tab_blocks = rope_tabs[0].shape[0] // tm
        tab_spec = pl.BlockSpec((tm, LANES), lambda i, j: (i % tab_blocks, 0))
        in_specs += [tab_spec, tab_spec]
        args += list(rope_tabs)
    out_specs = [pl.BlockSpec((tm, tn), lambda i, j: (i, j))]
    out_shape = [jax.ShapeDtypeStruct((t, n), BF16)]
    aliases, scratch = {}, []
    if kv_out:
        k_buf, v_buf = kv_bufs
        kv_seq = v_buf.shape[2]
        nb = tm // kv_seq
        out_specs += [
            pl.BlockSpec((nb, None) + k_buf.shape[2:], lambda i, j: (i, layer, 0, 0)),
            pl.BlockSpec((nb, None, kv_seq, tn), lambda i, j: (i, layer, 0, jnp.clip(j - 2 * nq, 0, nq - 1))),
        ]
        out_shape += [jax.ShapeDtypeStruct(k_buf.shape, F32), jax.ShapeDtypeStruct(v_buf.shape, F32)]
        aliases = {len(args): 1, len(args) + 1: 2}
        in_specs += [pl.BlockSpec(memory_space=pl.ANY)] * 2
        args += [k_buf, v_buf]
        scratch = [pltpu.VMEM((nb, kv_seq * K_PITCH, LANES), F32)]
    kern = functools.partial(_in_proj_kernel, rope=rope, kv_out=kv_out, kv_aliased=2 * kv_out,
                             q_tiles=nq, qk_tiles=2 * nq,
                             k_tiles=(nq, 2 * nq), v_tiles=(2 * nq, 3 * nq))
    return pl.pallas_call(
        kern,
        grid=(t // tm, n // tn),
        in_specs=in_specs,
        out_specs=out_specs,
        out_shape=out_shape,
        input_output_aliases=aliases,
        scratch_shapes=scratch,
        compiler_params=_cparams(2),
        name="in_proj_rope" if rope else "in_proj_kv",
    )(*args)


def _lambda(lq1, lk1, lq2, lk2, lam_init):
    return (jnp.exp(jnp.sum(lq1 * lk1, axis=-1, keepdims=True))
            - jnp.exp(jnp.sum(lq2 * lk2, axis=-1, keepdims=True)) + lam_init)


def _diff_attn_head(q, ks, vs, lam, gain, lam_init):
    def softmax_parts(m):
        cols = slice(m * HEAD_DIM, (m + 1) * HEAD_DIM)
        ss = [lax.dot_general(q[:, cols], k[:, cols], (((1,), (1,)), ((), ())),
                              preferred_element_type=F32) for k in ks]
        mx = functools.reduce(jnp.maximum, [jnp.max(s, axis=-1, keepdims=True) for s in ss])
        es = [jnp.exp2(s - mx) for s in ss]
        den = functools.reduce(jnp.add, [jnp.sum(e, axis=-1, keepdims=True) for e in es])
        return es, 1.0 / den

    e1, inv1 = softmax_parts(0)
    e2, inv2 = softmax_parts(1)
    w2 = lam * inv2
    o = functools.reduce(jnp.add, [
        jnp.dot((a * inv1 - b * w2).astype(BF16), v, preferred_element_type=F32)
        for a, b, v in zip(e1, e2, vs)])
    return _rms(o) * gain * (1.0 - lam_init)


def _attn_ctx_kernel(q_ref, k_ref, v_ref, lq1, lk1, lq2, lk2, g_ref, o_ref, *, lam_init):
    lam = _lambda(lq1[...], lk1[...], lq2[...], lk2[...], lam_init)
    gain = g_ref[...]
    for h in range(N_HEADS):
        cols = slice(h * V_DIM, (h + 1) * V_DIM)
        o = _diff_attn_head(q_ref[:, cols], [k_ref[:, cols]], [v_ref[:, cols]], lam, gain, lam_init)
        o_ref[:, cols] = o.astype(BF16)


def _attn_lat_kernel(q_ref, k_ref, v_ref, ck_ref, cv_ref, lq1, lk1, lq2, lk2, g_ref, o_ref, *,
                     lam_init):
    lam = _lambda(lq1[...], lk1[...], lq2[...], lk2[...], lam_init)
    q = q_ref[...]
    chunks = [(ck_ref[...].astype(BF16), cv_ref[...].astype(BF16))]
    for c in range(k_ref.shape[0] // KEY_CHUNK):
        rows = slice(c * KEY_CHUNK, (c + 1) * KEY_CHUNK)
        chunks.append((k_ref[rows, :], v_ref[rows, :]))
    outs = []
    for m in range(2):
        cols = slice(m * HEAD_DIM, (m + 1) * HEAD_DIM)
        mx = den = acc = None
        for kc, vc in chunks:
            s = lax.dot_general(q[:, cols], kc[:, cols], (((1,), (1,)), ((), ())),
                                preferred_element_type=F32)
            s_max = jnp.max(s, axis=-1, keepdims=True)
            if mx is None:
                mx = s_max
                e = jnp.exp2(s - mx)
                den = jnp.sum(e, axis=-1, keepdims=True)
                acc = jnp.dot(e.astype(BF16), vc, preferred_element_type=F32)
            else:
                mx_new = jnp.maximum(mx, s_max)
                alpha = jnp.exp2(mx - mx_new)
                e = jnp.exp2(s - mx_new)
                den = alpha * den + jnp.sum(e, axis=-1, keepdims=True)
                acc = alpha * acc + jnp.dot(e.astype(BF16), vc, preferred_element_type=F32)
                mx = mx_new
        outs.append(acc * (1.0 / den))
    o = outs[0] - lam * outs[1]
    o_ref[...] = (_rms(o) * g_ref[...] * (1.0 - lam_init)).astype(BF16)


def _lam_specs(layer, n_axes):
    idx = (lambda b: (layer, 0, 0)) if n_axes == 1 else (lambda b, h, qi: (layer, 0, 0))
    return [pl.BlockSpec((None, 1, HEAD_DIM), idx)] * 4 + [pl.BlockSpec((None, 1, V_DIM), idx)]


def _attention_ctx(z, lam_args, layer, lam_init, *, seq, width):
    t = z.shape[0]
    kern = functools.partial(_attn_ctx_kernel, lam_init=lam_init)
    return pl.pallas_call(
        kern,
        grid=(t // seq,),
        in_specs=[pl.BlockSpec((seq, width), lambda b: (b, 0)),
                  pl.BlockSpec((seq, width), lambda b: (b, 1)),
                  pl.BlockSpec((seq, width), lambda b: (b, 2))] + _lam_specs(layer, 1),
        out_specs=pl.BlockSpec((seq, width), lambda b: (b, 0)),
        out_shape=jax.ShapeDtypeStruct((t, width), BF16),
        compiler_params=_cparams(1),
        name="attention_ctx",
    )(z, z, z, *lam_args)


def _attention_lat(z, cache_k, cache_v, lam_args, layer, lam_init, *, seq, width):
    t = z.shape[0]
    nb = t // seq
    nq = seq // TQ_LAT
    past = cache_k.shape[2]
    kern = functools.partial(_attn_lat_kernel, lam_init=lam_init)
    return pl.pallas_call(
        kern,
        grid=(nb, N_HEADS, nq),
        in_specs=[
            pl.BlockSpec((TQ_LAT, V_DIM), lambda b, h, qi: (b * nq + qi, h)),
            pl.BlockSpec((seq, V_DIM), lambda b, h, qi: (b, N_HEADS + h)),
            pl.BlockSpec((seq, V_DIM), lambda b, h, qi: (b, 2 * N_HEADS + h)),
            pl.BlockSpec((None, None, past, V_DIM), lambda b, h, qi: (b, layer, 0, h)),
            pl.BlockSpec((None, None, past, V_DIM), lambda b, h, qi: (b, layer, 0, h)),
        ] + _lam_specs(layer, 3),
        out_specs=pl.BlockSpec((TQ_LAT, V_DIM), lambda b, h, qi: (b * nq + qi, h)),
        out_shape=jax.ShapeDtypeStruct((t, width), BF16),
        compiler_params=_cparams(3),
        name="attention_lat",
    )(z, z, z, cache_k, cache_v, *lam_args)


def _seq_masks(row0, rows, seq):
    p = (row0 + lax.broadcasted_iota(jnp.int32, (rows, 1), 0)) % seq
    return (p != 0).astype(F32), (p != seq - 1).astype(F32)


def _pool_mixer(u_ref, upad_ref, pw_ref, ps_ref, lhs_ref, row0, seq):
    halo = BF16_SUBLANES
    tm = u_ref.shape[0]
    pg = pw_ref.shape[-1]
    span = POOL_SUB + 2 * halo
    t_idx = lax.broadcasted_iota(jnp.int32, (POOL_SUB, span), 0)
    c_idx = lax.broadcasted_iota(jnp.int32, (POOL_SUB, span), 1)
    off = c_idx - halo - t_idx
    t_col = lax.broadcasted_iota(jnp.int32, (POOL_SUB, 1), 0)
    for r in range(tm // POOL_SUB):
        pos0 = (row0 + r * POOL_SUB) % seq
        pos_c = pos0 + c_idx - halo
        in_seq = (pos_c >= 0) & (pos_c < seq)
        pos_t = pos0 + t_col
        for gi, w in enumerate(POOL_WINDOWS):
            half = w // 2
            band = ((off >= -half) & (off < w - half) & in_seq).astype(F32).astype(BF16)
            cnt = jnp.minimum(pos_t + (w - half), seq) - jnp.maximum(pos_t - half, 0)
            cols = slice(gi * pg, (gi + 1) * pg)
            win = jnp.dot(band, upad_ref[r * POOL_SUB:r * POOL_SUB + span, cols],
                          preferred_element_type=F32)
            mean = win / cnt.astype(F32)
            dlt = mean - u_ref[r * POOL_SUB:(r + 1) * POOL_SUB, cols].astype(F32)
            y = jnp.dot(dlt.astype(BF16), pw_ref[gi], preferred_element_type=F32) * ps_ref[:, cols]
            lhs_ref[r * POOL_SUB:(r + 1) * POOL_SUB, cols] = y.astype(BF16)


def _branches_kernel(u_ref, up_ref, un_ref, cb_ref, ccx_ref, ccxp_ref, ccxn_ref, ya_ref, g_ref,
                     wp_ref, wa_ref, wc_ref, pw_ref, ps_ref, cw_ref, cbias_ref, o_ref,
                     acc_ref, lhs_ref, upad_ref, ppad_ref, *, seq):
    i = pl.program_id(0)
    s = pl.program_id(1)
    tm = u_ref.shape[0]
    halo = BF16_SUBLANES
    cwid = cb_ref.shape[1]
    row0 = i * tm

    def gated(lhs, w):
        gate = jax.nn.sigmoid(g_ref[...].astype(F32))
        return gate * jnp.dot(lhs, w, preferred_element_type=F32)

    @pl.when(s == 0)
    def _():
        upad_ref[0:halo, :] = up_ref[...]
        upad_ref[halo:halo + tm, :] = u_ref[...]
        upad_ref[halo + tm:, :] = un_ref[...]
        _pool_mixer(u_ref, upad_ref, pw_ref, ps_ref, lhs_ref, row0, seq)
        acc_ref[...] = gated(lhs_ref[...], wp_ref[...])

    @pl.when((s == 1) | (s == 2))
    def _():
        kc = ya_ref.shape[1]
        acc_ref[...] += gated(ya_ref[...], wa_ref[pl.ds(pl.multiple_of((s - 1) * kc, kc), kc), :])

    @pl.when(s == 3)
    def _():
        def prod(ref):
            v = ref[...].astype(F32)
            return v[:, :cwid] * v[:, cwid:]

        ppad_ref[0:halo, :] = prod(ccxp_ref)
        ppad_ref[halo:halo + tm, :] = prod(ccx_ref)
        ppad_ref[halo + tm:, :] = prod(ccxn_ref)
        m_prev, m_next = _seq_masks(row0, tm, seq)
        p_prev = ppad_ref[halo - 1:halo - 1 + tm, :] * m_prev
        p_next = ppad_ref[halo + 1:halo + 1 + tm, :] * m_next
        conv = (p_prev * cw_ref[0:1, :] + ppad_ref[halo:halo + tm, :] * cw_ref[1:2, :]
                + p_next * cw_ref[2:3, :] + cbias_ref[...])
        y = cb_ref[...].astype(F32) * conv
        o_ref[...] = (acc_ref[...] + gated(y.astype(BF16), wc_ref[...])).astype(BF16)


def _branches(z, y_attn, w_br, pool_w, pool_scale, conv_w, conv_b, layer, *, seq, d, pool_w_cols,
              conv_cols, col0):
    t = z.shape[0]
    tm = TM_MIX
    halo = BF16_SUBLANES
    hb = tm // halo
    last_hb = t // halo - 1
    kc = w_br[1].shape[1] // 2
    resident = lambda w: pl.BlockSpec((None,) + w.shape[1:], lambda i, s: (layer, 0, 0),
                                      pipeline_mode=pl.Buffered(1))
    prev = lambda i, s: jnp.maximum(i * hb - 1, 0)
    nxt = lambda i, s: jnp.minimum((i + 1) * hb, last_hb)
    pool_blk = col0["pool"] // pool_w_cols
    cb_blk = col0["cb"] // conv_cols
    ccx_blk = col0["ccx"] // (2 * conv_cols)
    g_blk = col0["g"] // d
    kern = functools.partial(_branches_kernel, seq=seq)
    return pl.pallas_call(
        kern,
        grid=(t // tm, 4),
        in_specs=[
            pl.BlockSpec((tm, pool_w_cols), lambda i, s: (i, pool_blk)),
            pl.BlockSpec((halo, pool_w_cols), lambda i, s: (prev(i, s), pool_blk)),
            pl.BlockSpec((halo, pool_w_cols), lambda i, s: (nxt(i, s), pool_blk)),
            pl.BlockSpec((tm, conv_cols), lambda i, s: (i, cb_blk)),
            pl.BlockSpec((tm, 2 * conv_cols), lambda i, s: (i, ccx_blk)),
            pl.BlockSpec((halo, 2 * conv_cols), lambda i, s: (prev(i, s), ccx_blk)),
            pl.BlockSpec((halo, 2 * conv_cols), lambda i, s: (nxt(i, s), ccx_blk)),
            pl.BlockSpec((tm, kc), lambda i, s: (i, jnp.clip(s - 1, 0, 1))),
            pl.BlockSpec((tm, d), lambda i, s: (i, g_blk + (s + 1) // 2)),
            resident(w_br[0]), resident(w_br[1]), resident(w_br[2]),
            pl.BlockSpec((None,) + pool_w.shape[1:], lambda i, s: (layer, 0, 0, 0)),
            pl.BlockSpec((None, 1, pool_w_cols), lambda i, s: (layer, 0, 0)),
            pl.BlockSpec((None, 3, conv_cols), lambda i, s: (layer, 0, 0)),
            pl.BlockSpec((None, 1, conv_cols), lambda i, s: (layer, 0, 0)),
        ],
        out_specs=pl.BlockSpec((tm, d), lambda i, s: (i, 0)),
        out_shape=jax.ShapeDtypeStruct((t, d), BF16),
        scratch_shapes=[
            pltpu.VMEM((tm, d), F32),
            pltpu.VMEM((tm, pool_w_cols), BF16),
            pltpu.VMEM((tm + 2 * halo, pool_w_cols), BF16),
            pltpu.VMEM((tm + 2 * halo, conv_cols), F32),
        ],
        compiler_params=_cparams(2),
        name="branches",
    )(z, z, z, z, z, z, z, y_attn, z, *w_br, pool_w, pool_scale, conv_w, conv_b)


def _out_proj_kernel(m_ref, w_ref, x_ref, gt_ref, g_ref, gn_ref, scn_ref, shn_ref, o_ref, h_ref):
    half = m_ref.shape[0] // 2
    for r in range(2):
        rows = slice(r * half, (r + 1) * half)
        m = jnp.dot(m_ref[rows, :], w_ref[...], preferred_element_type=F32)
        x1 = x_ref[rows, :] + _rms(m) * (gt_ref[0] * g_ref[...])
        o_ref[rows, :] = x1
        h_ref[rows, :] = _modulated_norm(x1, gn_ref, scn_ref, shn_ref)


def _out_proj(merged, w_out, x, mod3, g_post, g_pre_ffn, layer, mod_row):
    t, d = x.shape
    tm = TM_MIX
    mod_idx = lambda m: (lambda i: (mod_row(i * tm) * N_MOD + m, 0, 0))
    row = pl.BlockSpec((tm, d), lambda i: (i, 0))
    vec = pl.BlockSpec((1, d), lambda i: (0, 0))
    return pl.pallas_call(
        _out_proj_kernel,
        grid=(t // tm,),
        in_specs=[row, pl.BlockSpec((None, d, d), lambda i: (layer, 0, 0)), row,
                  pl.BlockSpec((1, 1, d), mod_idx(2)), vec, vec,
                  pl.BlockSpec((1, 1, d), mod_idx(4)), pl.BlockSpec((1, 1, d), mod_idx(3))],
        out_specs=[row, row],
        out_shape=[jax.ShapeDtypeStruct((t, d), F32), jax.ShapeDtypeStruct((t, d), BF16)],
        compiler_params=_cparams(1),
        name="out_proj",
    )(merged, w_out, x, mod3, g_post, g_pre_ffn, mod3, mod3)


def _gelu_tanh(x):
    c = math.sqrt(2.0 / math.pi)
    hx = 0.5 * x
    return hx * jnp.tanh(x * (c + (c * 0.044715) * (x * x))) + hx


def _ffn_layout(tm, seq):
    halo = BF16_SUBLANES
    pad = F32_SUBLANES
    if seq > tm:
        assert seq % tm == 0 and (tm + 2 * halo) % (2 * halo) == 0
        rows = tm + 2 * halo
        return (rows, rows, [(0, rows // 2, 0), (rows // 2, rows // 2, rows // 2)],
                [(halo, tm // 2), (halo + tm // 2, tm // 2)], [])
    assert tm % seq == 0
    nseg = tm // seq
    pieces = [(s * seq, seq, pad + s * (seq + pad)) for s in range(nseg)]
    outs = [(pad + s * (seq + pad), seq) for s in range(nseg)]
    zero_rows = [(s * (seq + pad), pad) for s in range(nseg + 1)]
    return tm, pad + nseg * (seq + pad), pieces, outs, zero_rows


def _ffn_kernel(*refs, seq, with_halo, emit_next):
    refs = list(refs)
    h_ref = refs.pop(0)
    hp_ref, hn_ref = (refs.pop(0), refs.pop(0)) if with_halo else (None, None)
    x_ref, gt_ref, gpost_ref = refs.pop(0), refs.pop(0), refs.pop(0)
    next_refs = (refs.pop(0), refs.pop(0), refs.pop(0)) if emit_next else None
    wg_ref, wv_ref, cp_ref, wd_ref = refs[:4]
    refs = refs[4:]
    o_ref = refs.pop(0)
    hnext_ref = refs.pop(0) if emit_next else None
    lhs_ref = refs.pop(0) if with_halo else h_ref
    acc_ref, u_ref = refs
    i = pl.program_id(0)
    f = pl.program_id(1)
    tm = x_ref.shape[0]
    halo = BF16_SUBLANES
    _, _, pieces, outs, zero_rows = _ffn_layout(tm, seq)

    @pl.when(f == 0)
    def _():
        if with_halo:
            has_prev = (i * tm) % seq != 0
            has_next = ((i + 1) * tm) % seq != 0
            lhs_ref[0:halo, :] = jnp.where(has_prev, hp_ref[...], jnp.zeros_like(hp_ref))
            lhs_ref[halo:halo + tm, :] = h_ref[...]
            lhs_ref[halo + tm:, :] = jnp.where(has_next, hn_ref[...], jnp.zeros_like(hn_ref))

        def zero_rows_of_acc(r, carry):
            acc_ref[pl.ds(pl.multiple_of(r * ZERO_ROWS, ZERO_ROWS), ZERO_ROWS), :] = jnp.zeros(
                (ZERO_ROWS, acc_ref.shape[1]), F32)
            return carry

        lax.fori_loop(0, tm // ZERO_ROWS, zero_rows_of_acc, 0)
        for start, rows in zero_rows:
            u_ref[:, start:start + rows, :] = jnp.zeros((2, rows, u_ref.shape[2]), F32)

    def conv_half(slot, w_ref):
        for lhs0, rows, u0 in pieces:
            u_ref[slot, u0:u0 + rows, :] = jnp.dot(lhs_ref[lhs0:lhs0 + rows, :], w_ref[...],
                                                   preferred_element_type=F32)
        p = slot * CONV_PARAM_ROWS
        return [
            u_ref[slot, u0 - 1:u0 - 1 + rows, :] * cp_ref[p:p + 1, :]
            + u_ref[slot, u0:u0 + rows, :] * cp_ref[p + 1:p + 2, :]
            + u_ref[slot, u0 + 1:u0 + 1 + rows, :] * cp_ref[p + 2:p + 3, :] + cp_ref[p + 3:p + 4, :]
            for u0, rows in outs]

    row = 0
    for gate, val in zip(conv_half(0, wg_ref), conv_half(1, wv_ref)):
        a = (_gelu_tanh(gate) * val).astype(BF16)
        acc_ref[row:row + a.shape[0], :] += jnp.dot(a, wd_ref[...], preferred_element_type=F32)
        row += a.shape[0]

    @pl.when(f == pl.num_programs(1) - 1)
    def _():
        x2 = x_ref[...] + _rms(acc_ref[...]) * (gt_ref[0] * gpost_ref[...])
        o_ref[...] = x2
        if emit_next:
            hnext_ref[...] = _modulated_norm(x2, *next_refs)


def _ffn(h, x, mod3, g_post, w_up, conv_params, w_down, layer, mod_row, *, seq, next_norm=None):
    t, d = x.shape
    tm, tf = TM_MIX, TF_FFN
    halo = BF16_SUBLANES
    hb = tm // halo
    last_hb = t // halo - 1
    nf = w_down.shape[1] // tf
    with_halo = seq > tm
    emit_next = next_norm is not None
    lhs_rows, u_rows, _, _, _ = _ffn_layout(tm, seq)
    row = pl.BlockSpec((tm, d), lambda i, f: (i, 0))
    vec = pl.BlockSpec((1, d), lambda i, f: (0, 0))

    def mod_spec(row_fn, m):
        return pl.BlockSpec((1, 1, d), lambda i, f: (row_fn(i * tm) * N_MOD + m, 0, 0))

    in_specs, args = [row], [h]
    if with_halo:
        in_specs += [pl.BlockSpec((halo, d), lambda i, f: (jnp.maximum(i * hb - 1, 0), 0)),
                     pl.BlockSpec((halo, d), lambda i, f: (jnp.minimum((i + 1) * hb, last_hb), 0))]
        args += [h, h]
    in_specs += [row, mod_spec(mod_row, 5), vec]
    args += [x, mod3, g_post]
    if emit_next:
        in_specs += [vec, mod_spec(next_norm[1], 1), mod_spec(next_norm[1], 0)]
        args += [next_norm[0], mod3, mod3]
    up_tile = pl.BlockSpec((None, None, d, tf), lambda i, f: (layer, f, 0, 0))
    in_specs += [up_tile, up_tile, pl.BlockSpec((None, 2 * CONV_PARAM_ROWS, tf), lambda i, f: (layer, 0, f)),
                 pl.BlockSpec((None, tf, d), lambda i, f: (layer, f, 0))]
    args += [*w_up, conv_params, w_down]
    scratch = [pltpu.VMEM((lhs_rows, d), BF16)] if with_halo else []
    scratch += [pltpu.VMEM((tm, d), F32), pltpu.VMEM((2, u_rows, tf), F32)]
    kern = functools.partial(_ffn_kernel, seq=seq, with_halo=with_halo, emit_next=emit_next)
    return pl.pallas_call(
        kern,
        grid=(t // tm, nf),
        in_specs=in_specs,
        out_specs=[row, row] if emit_next else [row],
        out_shape=[jax.ShapeDtypeStruct((t, d), F32)] + ([jax.ShapeDtypeStruct((t, d), BF16)] if emit_next else []),
        scratch_shapes=scratch,
        compiler_params=_cparams(2),
        name="ffn_halo" if with_halo else "ffn",
    )(*args)


def _prep_up_kernel(w_ref, g_ref, v_ref, *, d_ff):
    n_tiles, rows, tf = g_ref.shape
    for half, o_ref in enumerate((g_ref, v_ref)):
        for f in range(n_tiles):
            lo, hi = f * tf, min((f + 1) * tf, d_ff)
            o_ref[f, :, 0:hi - lo] = w_ref[:, half * d_ff + lo:half * d_ff + hi].astype(BF16)
            if hi - lo < tf:
                o_ref[f, :, hi - lo:] = jnp.zeros((rows, tf - (hi - lo)), BF16)


def _prep_down_kernel(w_ref, o_ref, *, d_ff):
    o_ref[0:d_ff, :] = w_ref[...].astype(BF16)
    o_ref[d_ff:, :] = jnp.zeros((o_ref.shape[0] - d_ff, o_ref.shape[1]), BF16)


def _prep_ffn_weights(w_up, w_down, d_ffp):
    depth, d, _ = w_up.shape
    d_ff = w_down.shape[1]
    n_tiles = d_ffp // TF_FFN
    half = jax.ShapeDtypeStruct((depth, n_tiles, d, TF_FFN), BF16)
    up = pl.pallas_call(
        functools.partial(_prep_up_kernel, d_ff=d_ff),
        grid=(depth, d // PREP_BLOCK),
        in_specs=[pl.BlockSpec((None, PREP_BLOCK, 2 * d_ff), lambda l, r: (l, r, 0))],
        out_specs=[pl.BlockSpec((None, n_tiles, PREP_BLOCK, TF_FFN), lambda l, r: (l, 0, r, 0))] * 2,
        out_shape=[half, half],
        compiler_params=_cparams(2),
        name="prep_w_up",
    )(w_up)
    down = pl.pallas_call(
        functools.partial(_prep_down_kernel, d_ff=d_ff),
        grid=(depth, d // PREP_BLOCK),
        in_specs=[pl.BlockSpec((None, d_ff, PREP_BLOCK), lambda l, c: (l, 0, c))],
        out_specs=pl.BlockSpec((None, d_ffp, PREP_BLOCK), lambda l, c: (l, 0, c)),
        out_shape=jax.ShapeDtypeStruct((depth, d_ffp, d), BF16),
        compiler_params=_cparams(2),
        name="prep_w_down",
    )(w_down)
    return tuple(up), down


def _rope_tables(seq):
    rows = seq // GRID_W
    row = jnp.repeat(jnp.arange(rows), GRID_W)
    col = jnp.tile(jnp.arange(GRID_W), rows)
    inv = ROPE_BASE ** (-jnp.arange(ROPE_FREQS, dtype=F32) / ROPE_FREQS)
    ang = jnp.stack([row, col], axis=-1).astype(F32)[:, :, None] * inv
    cos, sin = jnp.cos(ang), jnp.sin(ang)
    cos_t = jnp.concatenate([cos[:, 0], cos[:, 0], cos[:, 1], cos[:, 1]], axis=-1)
    sin_t = jnp.concatenate([-sin[:, 0], sin[:, 0], -sin[:, 1], sin[:, 1]], axis=-1)
    return cos_t, sin_t


def _pad_axis(a, axis, size):
    pad = [(0, 0)] * a.ndim
    pad[axis] = (0, size - a.shape[axis])
    return jnp.pad(a, pad)


def kernel(x_prompt, x_sample, cache_k, cache_v, c, c_ctx, w_mod, b_mod, g_pre_mix, g_post_mix,
           g_pre_ffn, g_post_ffn, w_in, pool_w, pool_scale, lambda_q1, lambda_k1, lambda_q2, lambda_k2,
           g_subln, conv_w, conv_b, w_br_pool, w_br_attn, w_br_conv, w_out, ffn_w_up, ffn_conv_w,
           ffn_conv_b, ffn_w_down):
    batch, seq, d = x_prompt.shape
    dec_batch, dec_seq, _ = x_sample.shape
    depth = w_in.shape[0]
    past = cache_k.shape[2]
    qk_w = N_HEADS * 2 * HEAD_DIM
    v_w = N_HEADS * V_DIM
    pool_cols = pool_scale.shape[1]
    conv_cols = conv_b.shape[1]
    d_ff = ffn_w_down.shape[1]
    d_ffp = -(-d_ff // TF_FFN) * TF_FFN
    assert qk_w == v_w and dec_batch + 1 <= MOD_ROWS

    col0 = {"pool": 3 * qk_w, "cb": 3 * qk_w + pool_cols, "ccx": 3 * qk_w + pool_cols + conv_cols,
            "g": 3 * qk_w + pool_cols + 3 * conv_cols}
    w_br = tuple(w.astype(BF16) for w in (w_br_pool, w_br_attn, w_br_conv))
    w_out_b = w_out.astype(BF16)
    pool_w_b = pool_w.astype(BF16)

    def padded_half(a, start, axis, dtype):
        part = lax.slice_in_dim(a, start, start + d_ff, axis=axis).astype(dtype)
        zeros = jnp.zeros(part.shape[:axis] + (d_ffp - d_ff,) + part.shape[axis + 1:], dtype)
        return jnp.concatenate([part, zeros], axis=axis)

    w_up, w_down = _prep_ffn_weights(ffn_w_up, ffn_w_down, d_ffp)
    fcp = jnp.concatenate([blk for start in (0, d_ff) for blk in (
        padded_half(ffn_conv_w, start, 2, F32), padded_half(ffn_conv_b[:, None, :], start, 2, F32))], axis=1)

    cvec = _pad_axis(jnp.concatenate([c_ctx[None, :], c], axis=0), 0, MOD_ROWS)
    mod = _modulation(cvec, w_mod, b_mod)
    mod3 = mod.reshape(depth * MOD_ROWS * N_MOD, 1, d)

    rope_tabs = _rope_tables(dec_seq)
    ck = cache_k.reshape(dec_batch, depth, past, qk_w)
    cv = cache_v.reshape(dec_batch, depth, past, v_w)
    row1 = lambda a: a[:, None, :]
    lam_args = (row1(lambda_q1), row1(lambda_k1), row1(lambda_q2), row1(lambda_k2), row1(g_subln))
    pool_scale3, conv_b3 = row1(pool_scale), row1(conv_b)

    xc = x_prompt.reshape(batch * seq, d)
    xl = x_sample.reshape(dec_batch * dec_seq, d)
    ctx_row = lambda l: (lambda r: l * MOD_ROWS)
    lat_row = lambda l: (lambda r: l * MOD_ROWS + 1 + r // dec_seq)
    gl = lambda a, l: a[l][None, :]
    hc = _prenorm(xc, mod3, gl(g_pre_mix, 0), ctx_row(0))
    hl = _prenorm(xl, mod3, gl(g_pre_mix, 0), lat_row(0))
    kv_bufs = (jnp.zeros((batch, depth, seq * (qk_w // LANES), LANES), F32),
               jnp.zeros((batch, depth, seq, v_w), F32))
    for l in range(depth):
        lam_init = 0.8 - 0.6 * math.exp(-0.3 * l)
        last = l == depth - 1
        mix = functools.partial(_branches, w_br=w_br, pool_w=pool_w_b, pool_scale=pool_scale3,
                                conv_w=conv_w, conv_b=conv_b3, layer=l, d=d, pool_w_cols=pool_cols,
                                conv_cols=conv_cols, col0=col0)
        ffn = functools.partial(_ffn, mod3=mod3, g_post=gl(g_post_ffn, l), w_up=w_up, conv_params=fcp,
                                w_down=w_down, layer=l)
        out_proj = functools.partial(_out_proj, w_out=w_out_b, mod3=mod3, g_post=gl(g_post_mix, l),
                                     g_pre_ffn=gl(g_pre_ffn, l), layer=l)
        next_norm = lambda row: None if last else (gl(g_pre_mix, l + 1), row(l + 1))

        z, *kv_bufs = _in_proj(hc, w_in, l, qk_w=qk_w, pool_cols=pool_cols, kv_bufs=kv_bufs)
        ya = _attention_ctx(z, lam_args, l, lam_init, seq=seq, width=v_w)
        merged = mix(z, ya, seq=seq)
        xc, h2 = out_proj(merged, x=xc, mod_row=ctx_row(l))
        res = ffn(h2, xc, mod_row=ctx_row(l), seq=seq, next_norm=next_norm(ctx_row))
        xc, hc = res if not last else (res[0], None)

        z = _in_proj(hl, w_in, l, qk_w=qk_w, pool_cols=pool_cols, rope_tabs=rope_tabs)[0]
        ya = _attention_lat(z, ck, cv, lam_args, l, lam_init, seq=dec_seq, width=v_w)
        merged = mix(z, ya, seq=dec_seq)
        xl, h2 = out_proj(merged, x=xl, mod_row=lat_row(l))
        res = ffn(h2, xl, mod_row=lat_row(l), seq=dec_seq, next_norm=next_norm(lat_row))
        xl, hl = res if not last else (res[0], None)

    return (xc.reshape(batch, seq, d), xl.reshape(dec_batch, dec_seq, d),
            kv_bufs[0].reshape(batch, depth, seq, N_HEADS, 2, HEAD_DIM),
            kv_bufs[1].reshape(batch, depth, seq, N_HEADS, V_DIM))
```

```python
import functools
import math

import jax
import jax.numpy as jnp
from jax import lax
from jax.experimental import pallas as pl
from jax.experimental.pallas import tpu as pltpu

F32 = jnp.float32
BF16 = jnp.bfloat16

GRID_W = 64
N_HEADS = 8
HEAD_DIM = 128
V_DIM = 2 * HEAD_DIM
POOL_WINDOWS = (2, 4, 8, 16)
ROPE_FREQS = HEAD_DIM // 4
ROPE_BASE = 10000.0
EPS = 1e-6
Q_SCALE = HEAD_DIM ** -0.5 * math.log2(math.e)
N_MOD = 6

LANES = 128
F32_SUBLANES = 8
BF16_SUBLANES = 16
VMEM_LIMIT_BYTES = 56 * 1024 * 1024
MOD_ROWS = 8

TM_IN = 2048
TN_IN = 512
TM_MIX = 512
POOL_SUB = 256
TQ_LAT = 512
TF_FFN = 512
KEY_CHUNK = 512
TN_MOD = 1024
ZERO_ROWS = 128
CONV_PARAM_ROWS = 4
PREP_BLOCK = 256


def _cparams(n_axes):
    return pltpu.CompilerParams(dimension_semantics=("arbitrary",) * n_axes,
                                vmem_limit_bytes=VMEM_LIMIT_BYTES)


def _rms(x):
    return x * lax.rsqrt(jnp.mean(x * x, axis=-1, keepdims=True) + EPS)


def _modulated_norm(x, gain_ref, sc_ref, sh_ref):
    return (_rms(x) * (gain_ref[...] * (1.0 + sc_ref[0])) + sh_ref[0]).astype(BF16)


def _mod_kernel(c_ref, w_ref, b_ref, o_ref):
    c = c_ref[...]
    s = (c * jax.nn.sigmoid(c)).astype(BF16)
    o_ref[...] = jnp.dot(s, w_ref[...].astype(BF16), preferred_element_type=F32) + b_ref[...]


def _modulation(cvec, w_mod, b_mod):
    depth, d, n = w_mod.shape
    return pl.pallas_call(
        _mod_kernel,
        grid=(depth, n // TN_MOD),
        in_specs=[
            pl.BlockSpec((MOD_ROWS, d), lambda l, j: (0, 0)),
            pl.BlockSpec((None, d, TN_MOD), lambda l, j: (l, 0, j)),
            pl.BlockSpec((None, 1, TN_MOD), lambda l, j: (l, 0, j)),
        ],
        out_specs=pl.BlockSpec((None, MOD_ROWS, TN_MOD), lambda l, j: (l, 0, j)),
        out_shape=jax.ShapeDtypeStruct((depth, MOD_ROWS, n), F32),
        compiler_params=_cparams(2),
        name="modulation",
    )(cvec, w_mod, b_mod.reshape(depth, 1, n))


def _rope_tile(acc, c, s):
    lane = lax.broadcasted_iota(jnp.int32, c.shape, 1)
    upper = (lane & ROPE_FREQS) != 0
    outs = []
    for g in range(acc.shape[1] // LANES):
        xg = acc[:, g * LANES:(g + 1) * LANES]
        partner = jnp.where(upper, pltpu.roll(xg, ROPE_FREQS, axis=1),
                            pltpu.roll(xg, LANES - ROPE_FREQS, axis=1))
        outs.append(xg * c + partner * s)
    return jnp.concatenate(outs, axis=1)


def _prenorm_kernel(x_ref, g_ref, sc_ref, sh_ref, h_ref):
    h_ref[...] = _modulated_norm(x_ref[...], g_ref, sc_ref, sh_ref)


def _prenorm(x, mod3, g_pre, mod_row):
    t, d = x.shape
    tm = TM_MIX
    return pl.pallas_call(
        _prenorm_kernel,
        grid=(t // tm,),
        in_specs=[
            pl.BlockSpec((tm, d), lambda i: (i, 0)),
            pl.BlockSpec((1, d), lambda i: (0, 0)),
            pl.BlockSpec((1, 1, d), lambda i: (mod_row(i * tm) * N_MOD + 1, 0, 0)),
            pl.BlockSpec((1, 1, d), lambda i: (mod_row(i * tm) * N_MOD + 0, 0, 0)),
        ],
        out_specs=pl.BlockSpec((tm, d), lambda i: (i, 0)),
        out_shape=jax.ShapeDtypeStruct((t, d), BF16),
        compiler_params=_cparams(1),
        name="prenorm",
    )(x, g_pre, mod3, mod3)


def _in_proj_kernel(*refs, rope, kv_out, kv_aliased, q_tiles, qk_tiles, k_tiles, v_tiles):
    h_ref, w_ref = refs[:2]
    pos = 2
    if rope:
        cos_ref, sin_ref = refs[pos:pos + 2]
        pos += 2
    pos += kv_aliased
    z_ref = refs[pos]
    pos += 1
    if kv_out:
        k_ref, v_ref = refs[pos:pos + 2]
    j = pl.program_id(1)

    acc = jnp.dot(h_ref[...], w_ref[...].astype(BF16), preferred_element_type=F32)

    if rope:
        z_ref[...] = acc.astype(BF16)

        @pl.when(j < q_tiles)
        def _():
            z_ref[...] = _rope_tile(acc * Q_SCALE, cos_ref[...], sin_ref[...]).astype(BF16)

        @pl.when((j >= q_tiles) & (j < qk_tiles))
        def _():
            z_ref[...] = _rope_tile(acc, cos_ref[...], sin_ref[...]).astype(BF16)
    else:
        z_ref[...] = (acc * jnp.where(j < q_tiles, Q_SCALE, 1.0)).astype(BF16)

    if kv_out:
        n_seq, k_rows, _ = k_ref.shape
        tile_groups = acc.shape[1] // LANES
        lane_groups = tile_groups * (k_tiles[1] - k_tiles[0])
        seq = k_rows // lane_groups
        for jj in range(k_tiles[1] - k_tiles[0]):
            @pl.when(j == k_tiles[0] + jj)
            def _(jj=jj):
                for b in range(n_seq):
                    for g in range(tile_groups):
                        k_ref[b, pl.ds(jj * tile_groups + g, seq, stride=lane_groups), :] = (
                            acc[b * seq:(b + 1) * seq, g * LANES:(g + 1) * LANES])

        @pl.when((j >= v_tiles[0]) & (j < v_tiles[1]))
        def _():
            for b in range(n_seq):
                v_ref[b] = acc[b * seq:(b + 1) * seq, :]


def _in_proj(h, w_in, layer, *, qk_w, pool_cols, rope_tabs=None, kv_bufs=None):
    t, d = h.shape
    n = w_in.shape[2]
    kv_out = kv_bufs is not None
    tm, tn = TM_IN, TN_IN
    nq = qk_w // tn
    n_pool = pool_cols // tn
    rope = rope_tabs is not None

    def w_tile(j):
        return jnp.where(j < 3 * nq, j + n_pool, jnp.where(j < 3 * nq + n_pool, j - 3 * nq, j))

    in_specs = [
        pl.BlockSpec((tm, d), lambda i, j: (i, 0)),
        pl.BlockSpec((None, d, tn), lambda i, j: (layer, 0, w_tile(j))),
    ]
    args = [h, w_in]
    if rope:
        tab_blocks = rope_tabs[0].shape[0] // tm
        tab_spec = pl.BlockSpec((tm, LANES), lambda i, j: (i % tab_blocks, 0))
        in_specs += [tab_spec, tab_spec]
        args += list(rope_tabs)
    out_specs = [pl.BlockSpec((tm, tn), lambda i, j: (i, j))]
    out_shape = [jax.ShapeDtypeStruct((t, n), BF16)]
    aliases = {}
    if kv_out:
        k_buf, v_buf = kv_bufs
        kv_seq = v_buf.shape[2]
        nb = tm // kv_seq
        out_specs += [
            pl.BlockSpec((nb, None) + k_buf.shape[2:], lambda i, j: (i, layer, 0, 0),
                         pipeline_mode=pl.Buffered(1)),
            pl.BlockSpec((nb, None, kv_seq, tn), lambda i, j: (i, layer, 0, jnp.clip(j - 2 * nq, 0, nq - 1))),
        ]
        out_shape += [jax.ShapeDtypeStruct(k_buf.shape, F32), jax.ShapeDtypeStruct(v_buf.shape, F32)]
        aliases = {len(args): 1, len(args) + 1: 2}
        in_specs += [pl.BlockSpec(memory_space=pl.ANY)] * 2
        args += [k_buf, v_buf]
    kern = functools.partial(_in_proj_kernel, rope=rope, kv_out=kv_out, kv_aliased=2 * kv_out,
                             q_tiles=nq, qk_tiles=2 * nq,
                             k_tiles=(nq, 2 * nq), v_tiles=(2 * nq, 3 * nq))
    return pl.pallas_call(
        kern,
        grid=(t // tm, n // tn),
        in_specs=in_specs,
        out_specs=out_specs,
        out_shape=out_shape,
        input_output_aliases=aliases,
        compiler_params=_cparams(2),
        name="in_proj_rope" if rope else "in_proj_kv",
    )(*args)


def _lambda(lq1, lk1, lq2, lk2, lam_init):
    return (jnp.exp(jnp.sum(lq1 * lk1, axis=-1, keepdims=True))
            - jnp.exp(jnp.sum(lq2 * lk2, axis=-1, keepdims=True)) + lam_init)


def _diff_attn_head(q, ks, vs, lam, gain, lam_init):
    def softmax_parts(m):
        cols = slice(m * HEAD_DIM, (m + 1) * HEAD_DIM)
        ss = [lax.dot_general(q[:, cols], k[:, cols], (((1,), (1,)), ((), ())),
                              preferred_element_type=F32) for k in ks]
        mx = functools.reduce(jnp.maximum, [jnp.max(s, axis=-1, keepdims=True) for s in ss])
        es = [jnp.exp2(s - mx) for s in ss]
        den = functools.reduce(jnp.add, [jnp.sum(e, axis=-1, keepdims=True) for e in es])
        return es, 1.0 / den

    e1, inv1 = softmax_parts(0)
    e2, inv2 = softmax_parts(1)
    w2 = lam * inv2
    o = functools.reduce(jnp.add, [
        jnp.dot((a * inv1 - b * w2).astype(BF16), v, preferred_element_type=F32)
        for a, b, v in zip(e1, e2, vs)])
    return _rms(o) * gain * (1.0 - lam_init)


def _attn_ctx_kernel(q_ref, k_ref, v_ref, lq1, lk1, lq2, lk2, g_ref, o_ref, *, lam_init):
    lam = _lambda(lq1[...], lk1[...], lq2[...], lk2[...], lam_init)
    gain = g_ref[...]
    for h in range(N_HEADS):
        cols = slice(h * V_DIM, (h + 1) * V_DIM)
        o = _diff_attn_head(q_ref[:, cols], [k_ref[:, cols]], [v_ref[:, cols]], lam, gain, lam_init)
        o_ref[:, cols] = o.astype(BF16)


def _attn_lat_kernel(q_ref, k_ref, v_ref, ck_ref, cv_ref, lq1, lk1, lq2, lk2, g_ref, o_ref, *,
                     lam_init):
    lam = _lambda(lq1[...], lk1[...], lq2[...], lk2[...], lam_init)
    q = q_ref[...]
    chunks = [(ck_ref[...].astype(BF16), cv_ref[...].astype(BF16))]
    for c in range(k_ref.shape[0] // KEY_CHUNK):
        rows = slice(c * KEY_CHUNK, (c + 1) * KEY_CHUNK)
        chunks.append((k_ref[rows, :], v_ref[rows, :]))
    outs = []
    for m in range(2):
        cols = slice(m * HEAD_DIM, (m + 1) * HEAD_DIM)
        mx = den = acc = None
        for kc, vc in chunks:
            s = lax.dot_general(q[:, cols], kc[:, cols], (((1,), (1,)), ((), ())),
                                preferred_element_type=F32)
            s_max = jnp.max(s, axis=-1, keepdims=True)
            if mx is None:
                mx = s_max
                e = jnp.exp2(s - mx)
                den = jnp.sum(e, axis=-1, keepdims=True)
                acc = jnp.dot(e.astype(BF16), vc, preferred_element_type=F32)
            else:
                mx_new = jnp.maximum(mx, s_max)
                alpha = jnp.exp2(mx - mx_new)
                e = jnp.exp2(s - mx_new)
                den = alpha * den + jnp.sum(e, axis=-1, keepdims=True)
                acc = alpha * acc + jnp.dot(e.astype(BF16), vc, preferred_element_type=F32)
                mx = mx_new
        outs.append(acc * (1.0 / den))
    o = outs[0] - lam * outs[1]
    o_ref[...] = (_rms(o) * g_ref[...] * (1.0 - lam_init)).astype(BF16)


def _lam_specs(layer, n_axes):
    idx = (lambda b: (layer, 0, 0)) if n_axes == 1 else (lambda b, h, qi: (layer, 0, 0))
    return [pl.BlockSpec((None, 1, HEAD_DIM), idx)] * 4 + [pl.BlockSpec((None, 1, V_DIM), idx)]


def _attention_ctx(z, lam_args, layer, lam_init, *, seq, width):
    t = z.shape[0]
    kern = functools.partial(_attn_ctx_kernel, lam_init=lam_init)
    return pl.pallas_call(
        kern,
        grid=(t // seq,),
        in_specs=[pl.BlockSpec((seq, width), lambda b: (b, 0)),
                  pl.BlockSpec((seq, width), lambda b: (b, 1)),
                  pl.BlockSpec((seq, width), lambda b: (b, 2))] + _lam_specs(layer, 1),
        out_specs=pl.BlockSpec((seq, width), lambda b: (b, 0)),
        out_shape=jax.ShapeDtypeStruct((t, width), BF16),
        compiler_params=_cparams(1),
        name="attention_ctx",
    )(z, z, z, *lam_args)


def _attention_lat(z, cache_k, cache_v, lam_args, layer, lam_init, *, seq, width):
    t = z.shape[0]
    nb = t // seq
    nq = seq // TQ_LAT
    past = cache_k.shape[2]
    kern = functools.partial(_attn_lat_kernel, lam_init=lam_init)
    return pl.pallas_call(
        kern,
        grid=(nb, N_HEADS, nq),
        in_specs=[
            pl.BlockSpec((TQ_LAT, V_DIM), lambda b, h, qi: (b * nq + qi, h)),
            pl.BlockSpec((seq, V_DIM), lambda b, h, qi: (b, N_HEADS + h)),
            pl.BlockSpec((seq, V_DIM), lambda b, h, qi: (b, 2 * N_HEADS + h)),
            pl.BlockSpec((None, None, past, V_DIM), lambda b, h, qi: (b, layer, 0, h)),
            pl.BlockSpec((None, None, past, V_DIM), lambda b, h, qi: (b, layer, 0, h)),
        ] + _lam_specs(layer, 3),
        out_specs=pl.BlockSpec((TQ_LAT, V_DIM), lambda b, h, qi: (b * nq + qi, h)),
        out_shape=jax.ShapeDtypeStruct((t, width), BF16),
        compiler_params=_cparams(3),
        name="attention_lat",
    )(z, z, z, cache_k, cache_v, *lam_args)


def _seq_masks(row0, rows, seq):
    p = (row0 + lax.broadcasted_iota(jnp.int32, (rows, 1), 0)) % seq
    return (p != 0).astype(F32), (p != seq - 1).astype(F32)


def _pool_mixer(u_ref, upad_ref, pw_ref, ps_ref, lhs_ref, row0, seq):
    halo = BF16_SUBLANES
    tm = u_ref.shape[0]
    pg = pw_ref.shape[-1]
    span = POOL_SUB + 2 * halo
    t_idx = lax.broadcasted_iota(jnp.int32, (POOL_SUB, span), 0)
    c_idx = lax.broadcasted_iota(jnp.int32, (POOL_SUB, span), 1)
    off = c_idx - halo - t_idx
    t_col = lax.broadcasted_iota(jnp.int32, (POOL_SUB, 1), 0)
    for r in range(tm // POOL_SUB):
        pos0 = (row0 + r * POOL_SUB) % seq
        pos_c = pos0 + c_idx - halo
        in_seq = (pos_c >= 0) & (pos_c < seq)
        pos_t = pos0 + t_col
        for gi, w in enumerate(POOL_WINDOWS):
            half = w // 2
            band = ((off >= -half) & (off < w - half) & in_seq).astype(F32).astype(BF16)
            cnt = jnp.minimum(pos_t + (w - half), seq) - jnp.maximum(pos_t - half, 0)
            cols = slice(gi * pg, (gi + 1) * pg)
            win = jnp.dot(band, upad_ref[r * POOL_SUB:r * POOL_SUB + span, cols],
                          preferred_element_type=F32)
            mean = win / cnt.astype(F32)
            dlt = mean - u_ref[r * POOL_SUB:(r + 1) * POOL_SUB, cols].astype(F32)
            y = jnp.dot(dlt.astype(BF16), pw_ref[gi], preferred_element_type=F32) * ps_ref[:, cols]
            lhs_ref[r * POOL_SUB:(r + 1) * POOL_SUB, cols] = y.astype(BF16)


def _branches_kernel(u_ref, up_ref, un_ref, cb_ref, ccx_ref, ccxp_ref, ccxn_ref, ya_ref, g_ref,
                     wp_ref, wa_ref, wc_ref, pw_ref, ps_ref, cw_ref, cbias_ref, o_ref,
                     acc_ref, lhs_ref, upad_ref, ppad_ref, *, seq):
    i = pl.program_id(0)
    s = pl.program_id(1)
    tm = u_ref.shape[0]
    halo = BF16_SUBLANES
    cwid = cb_ref.shape[1]
    row0 = i * tm

    def gated(lhs, w):
        gate = jax.nn.sigmoid(g_ref[...].astype(F32))
        return gate * jnp.dot(lhs, w, preferred_element_type=F32)

    @pl.when(s == 0)
    def _():
        upad_ref[0:halo, :] = up_ref[...]
        upad_ref[halo:halo + tm, :] = u_ref[...]
        upad_ref[halo + tm:, :] = un_ref[...]
        _pool_mixer(u_ref, upad_ref, pw_ref, ps_ref, lhs_ref, row0, seq)
        acc_ref[...] = gated(lhs_ref[...], wp_ref[...])

    @pl.when((s == 1) | (s == 2))
    def _():
        kc = ya_ref.shape[1]
        acc_ref[...] += gated(ya_ref[...], wa_ref[pl.ds(pl.multiple_of((s - 1) * kc, kc), kc), :])

    @pl.when(s == 3)
    def _():
        def prod(ref):
            v = ref[...].astype(F32)
            return v[:, :cwid] * v[:, cwid:]

        ppad_ref[0:halo, :] = prod(ccxp_ref)
        ppad_ref[halo:halo + tm, :] = prod(ccx_ref)
        ppad_ref[halo + tm:, :] = prod(ccxn_ref)
        m_prev, m_next = _seq_masks(row0, tm, seq)
        p_prev = ppad_ref[halo - 1:halo - 1 + tm, :] * m_prev
        p_next = ppad_ref[halo + 1:halo + 1 + tm, :] * m_next
        conv = (p_prev * cw_ref[0:1, :] + ppad_ref[halo:halo + tm, :] * cw_ref[1:2, :]
                + p_next * cw_ref[2:3, :] + cbias_ref[...])
        y = cb_ref[...].astype(F32) * conv
        o_ref[...] = (acc_ref[...] + gated(y.astype(BF16), wc_ref[...])).astype(BF16)


def _branches(z, y_attn, w_br, pool_w, pool_scale, conv_w, conv_b, layer, *, seq, d, pool_w_cols,
              conv_cols, col0):
    t = z.shape[0]
    tm = TM_MIX
    halo = BF16_SUBLANES
    hb = tm // halo
    last_hb = t // halo - 1
    kc = w_br[1].shape[1] // 2
    resident = lambda w: pl.BlockSpec((None,) + w.shape[1:], lambda i, s: (layer, 0, 0),
                                      pipeline_mode=pl.Buffered(1))
    prev = lambda i, s: jnp.maximum(i * hb - 1, 0)
    nxt = lambda i, s: jnp.minimum((i + 1) * hb, last_hb)
    pool_blk = col0["pool"] // pool_w_cols
    cb_blk = col0["cb"] // conv_cols
    ccx_blk = col0["ccx"] // (2 * conv_cols)
    g_blk = col0["g"] // d
    kern = functools.partial(_branches_kernel, seq=seq)
    return pl.pallas_call(
        kern,
        grid=(t // tm, 4),
        in_specs=[
            pl.BlockSpec((tm, pool_w_cols), lambda i, s: (i, pool_blk)),
            pl.BlockSpec((halo, pool_w_cols), lambda i, s: (prev(i, s), pool_blk)),
            pl.BlockSpec((halo, pool_w_cols), lambda i, s: (nxt(i, s), pool_blk)),
            pl.BlockSpec((tm, conv_cols), lambda i, s: (i, cb_blk)),
            pl.BlockSpec((tm, 2 * conv_cols), lambda i, s: (i, ccx_blk)),
            pl.BlockSpec((halo, 2 * conv_cols), lambda i, s: (prev(i, s), ccx_blk)),
            pl.BlockSpec((halo, 2 * conv_cols), lambda i, s: (nxt(i, s), ccx_blk)),
            pl.BlockSpec((tm, kc), lambda i, s: (i, jnp.clip(s - 1, 0, 1))),
            pl.BlockSpec((tm, d), lambda i, s: (i, g_blk + (s + 1) // 2)),
            resident(w_br[0]), resident(w_br[1]), resident(w_br[2]),
            pl.BlockSpec((None,) + pool_w.shape[1:], lambda i, s: (layer, 0, 0, 0)),
            pl.BlockSpec((None, 1, pool_w_cols), lambda i, s: (layer, 0, 0)),
            pl.BlockSpec((None, 3, conv_cols), lambda i, s: (layer, 0, 0)),
            pl.BlockSpec((None, 1, conv_cols), lambda i, s: (layer, 0, 0)),
        ],
        out_specs=pl.BlockSpec((tm, d), lambda i, s: (i, 0)),
        out_shape=jax.ShapeDtypeStruct((t, d), BF16),
        scratch_shapes=[
            pltpu.VMEM((tm, d), F32),
            pltpu.VMEM((tm, pool_w_cols), BF16),
            pltpu.VMEM((tm + 2 * halo, pool_w_cols), BF16),
            pltpu.VMEM((tm + 2 * halo, conv_cols), F32),
        ],
        compiler_params=_cparams(2),
        name="branches",
    )(z, z, z, z, z, z, z, y_attn, z, *w_br, pool_w, pool_scale, conv_w, conv_b)


def _out_proj_kernel(m_ref, w_ref, x_ref, gt_ref, g_ref, gn_ref, scn_ref, shn_ref, o_ref, h_ref):
    half = m_ref.shape[0] // 2
    for r in range(2):
        rows = slice(r * half, (r + 1) * half)
        m = jnp.dot(m_ref[rows, :], w_ref[...], preferred_element_type=F32)
        x1 = x_ref[rows, :] + _rms(m) * (gt_ref[0] * g_ref[...])
        o_ref[rows, :] = x1
        h_ref[rows, :] = _modulated_norm(x1, gn_ref, scn_ref, shn_ref)


def _out_proj(merged, w_out, x, mod3, g_post, g_pre_ffn, layer, mod_row):
    t, d = x.shape
    tm = TM_MIX
    mod_idx = lambda m: (lambda i: (mod_row(i * tm) * N_MOD + m, 0, 0))
    row = pl.BlockSpec((tm, d), lambda i: (i, 0))
    vec = pl.BlockSpec((1, d), lambda i: (0, 0))
    return pl.pallas_call(
        _out_proj_kernel,
        grid=(t // tm,),
        in_specs=[row, pl.BlockSpec((None, d, d), lambda i: (layer, 0, 0)), row,
                  pl.BlockSpec((1, 1, d), mod_idx(2)), vec, vec,
                  pl.BlockSpec((1, 1, d), mod_idx(4)), pl.BlockSpec((1, 1, d), mod_idx(3))],
        out_specs=[row, row],
        out_shape=[jax.ShapeDtypeStruct((t, d), F32), jax.ShapeDtypeStruct((t, d), BF16)],
        compiler_params=_cparams(1),
        name="out_proj",
    )(merged, w_out, x, mod3, g_post, g_pre_ffn, mod3, mod3)


def _gelu_tanh(x):
    c = math.sqrt(2.0 / math.pi)
    hx = 0.5 * x
    return hx * jnp.tanh(x * (c + (c * 0.044715) * (x * x))) + hx


def _ffn_layout(tm, seq):
    halo = BF16_SUBLANES
    pad = F32_SUBLANES
    if seq > tm:
        assert seq % tm == 0 and (tm + 2 * halo) % (2 * halo) == 0
        rows = tm + 2 * halo
        return (rows, rows, [(0, rows // 2, 0), (rows // 2, rows // 2, rows // 2)],
                [(halo, tm // 2), (halo + tm // 2, tm // 2)], [])
    assert tm % seq == 0
    nseg = tm // seq
    pieces = [(s * seq, seq, pad + s * (seq + pad)) for s in range(nseg)]
    outs = [(pad + s * (seq + pad), seq) for s in range(nseg)]
    zero_rows = [(s * (seq + pad), pad) for s in range(nseg + 1)]
    return tm, pad + nseg * (seq + pad), pieces, outs, zero_rows


def _ffn_kernel(*refs, seq, with_halo, emit_next):
    refs = list(refs)
    h_ref = refs.pop(0)
    hp_ref, hn_ref = (refs.pop(0), refs.pop(0)) if with_halo else (None, None)
    x_ref, gt_ref, gpost_ref = refs.pop(0), refs.pop(0), refs.pop(0)
    next_refs = (refs.pop(0), refs.pop(0), refs.pop(0)) if emit_next else None
    wg_ref, wv_ref, cp_ref, wd_ref = refs[:4]
    refs = refs[4:]
    o_ref = refs.pop(0)
    hnext_ref = refs.pop(0) if emit_next else None
    lhs_ref = refs.pop(0) if with_halo else h_ref
    acc_ref, u_ref = refs
    i = pl.program_id(0)
    f = pl.program_id(1)
    tm = x_ref.shape[0]
    halo = BF16_SUBLANES
    _, _, pieces, outs, zero_rows = _ffn_layout(tm, seq)

    @pl.when(f == 0)
    def _():
        if with_halo:
            has_prev = (i * tm) % seq != 0
            has_next = ((i + 1) * tm) % seq != 0
            lhs_ref[0:halo, :] = jnp.where(has_prev, hp_ref[...], jnp.zeros_like(hp_ref))
            lhs_ref[halo:halo + tm, :] = h_ref[...]
            lhs_ref[halo + tm:, :] = jnp.where(has_next, hn_ref[...], jnp.zeros_like(hn_ref))

        def zero_rows_of_acc(r, carry):
            acc_ref[pl.ds(pl.multiple_of(r * ZERO_ROWS, ZERO_ROWS), ZERO_ROWS), :] = jnp.zeros(
                (ZERO_ROWS, acc_ref.shape[1]), F32)
            return carry

        lax.fori_loop(0, tm // ZERO_ROWS, zero_rows_of_acc, 0)
        for start, rows in zero_rows:
            u_ref[:, start:start + rows, :] = jnp.zeros((2, rows, u_ref.shape[2]), F32)

    def conv_half(slot, w_ref):
        for lhs0, rows, u0 in pieces:
            u_ref[slot, u0:u0 + rows, :] = jnp.dot(lhs_ref[lhs0:lhs0 + rows, :], w_ref[...],
                                                   preferred_element_type=F32)
        p = slot * CONV_PARAM_ROWS
        return [
            u_ref[slot, u0 - 1:u0 - 1 + rows, :] * cp_ref[p:p + 1, :]
            + u_ref[slot, u0:u0 + rows, :] * cp_ref[p + 1:p + 2, :]
            + u_ref[slot, u0 + 1:u0 + 1 + rows, :] * cp_ref[p + 2:p + 3, :] + cp_ref[p + 3:p + 4, :]
            for u0, rows in outs]

    row = 0
    for gate, val in zip(conv_half(0, wg_ref), conv_half(1, wv_ref)):
        a = (_gelu_tanh(gate) * val).astype(BF16)
        acc_ref[row:row + a.shape[0], :] += jnp.dot(a, wd_ref[...], preferred_element_type=F32)
        row += a.shape[0]

    @pl.when(f == pl.num_programs(1) - 1)
    def _():
        x2 = x_ref[...] + _rms(acc_ref[...]) * (gt_ref[0] * gpost_ref[...])
        o_ref[...] = x2
        if emit_next:
            hnext_ref[...] = _modulated_norm(x2, *next_refs)


def _ffn(h, x, mod3, g_post, w_up, conv_params, w_down, layer, mod_row, *, seq, next_norm=None):
    t, d = x.shape
    tm, tf = TM_MIX, TF_FFN
    halo = BF16_SUBLANES
    hb = tm // halo
    last_hb = t // halo - 1
    nf = w_down.shape[1] // tf
    with_halo = seq > tm
    emit_next = next_norm is not None
    lhs_rows, u_rows, _, _, _ = _ffn_layout(tm, seq)
    row = pl.BlockSpec((tm, d), lambda i, f: (i, 0))
    vec = pl.BlockSpec((1, d), lambda i, f: (0, 0))

    def mod_spec(row_fn, m):
        return pl.BlockSpec((1, 1, d), lambda i, f: (row_fn(i * tm) * N_MOD + m, 0, 0))

    in_specs, args = [row], [h]
    if with_halo:
        in_specs += [pl.BlockSpec((halo, d), lambda i, f: (jnp.maximum(i * hb - 1, 0), 0)),
                     pl.BlockSpec((halo, d), lambda i, f: (jnp.minimum((i + 1) * hb, last_hb), 0))]
        args += [h, h]
    in_specs += [row, mod_spec(mod_row, 5), vec]
    args += [x, mod3, g_post]
    if emit_next:
        in_specs += [vec, mod_spec(next_norm[1], 1), mod_spec(next_norm[1], 0)]
        args += [next_norm[0], mod3, mod3]
    up_tile = pl.BlockSpec((None, None, d, tf), lambda i, f: (layer, f, 0, 0))
    in_specs += [up_tile, up_tile, pl.BlockSpec((None, 2 * CONV_PARAM_ROWS, tf), lambda i, f: (layer, 0, f)),
                 pl.BlockSpec((None, tf, d), lambda i, f: (layer, f, 0))]
    args += [*w_up, conv_params, w_down]
    scratch = [pltpu.VMEM((lhs_rows, d), BF16)] if with_halo else []
    scratch += [pltpu.VMEM((tm, d), F32), pltpu.VMEM((2, u_rows, tf), F32)]
    kern = functools.partial(_ffn_kernel, seq=seq, with_halo=with_halo, emit_next=emit_next)
    return pl.pallas_call(
        kern,
        grid=(t // tm, nf),
        in_specs=in_specs,
        out_specs=[row, row] if emit_next else [row],
        out_shape=[jax.ShapeDtypeStruct((t, d), F32)] + ([jax.ShapeDtypeStruct((t, d), BF16)] if emit_next else []),
        scratch_shapes=scratch,
        compiler_params=_cparams(2),
        name="ffn_halo" if with_halo else "ffn",
    )(*args)


def _prep_up_kernel(w_ref, g_ref, v_ref, *, d_ff):
    n_tiles, rows, tf = g_ref.shape
    for half, o_ref in enumerate((g_ref, v_ref)):
        for f in range(n_tiles):
            lo, hi = f * tf, min((f + 1) * tf, d_ff)
            o_ref[f, :, 0:hi - lo] = w_ref[:, half * d_ff + lo:half * d_ff + hi].astype(BF16)
            if hi - lo < tf:
                o_ref[f, :, hi - lo:] = jnp.zeros((rows, tf - (hi - lo)), BF16)


def _prep_down_kernel(w_ref, o_ref, *, d_ff):
    o_ref[0:d_ff, :] = w_ref[...].astype(BF16)
    o_ref[d_ff:, :] = jnp.zeros((o_ref.shape[0] - d_ff, o_ref.shape[1]), BF16)


def _prep_ffn_weights(w_up, w_down, d_ffp):
    depth, d, _ = w_up.shape
    d_ff = w_down.shape[1]
    n_tiles = d_ffp // TF_FFN
    half = jax.ShapeDtypeStruct((depth, n_tiles, d, TF_FFN), BF16)
    up = pl.pallas_call(
        functools.partial(_prep_up_kernel, d_ff=d_ff),
        grid=(depth, d // PREP_BLOCK),
        in_specs=[pl.BlockSpec((None, PREP_BLOCK, 2 * d_ff), lambda l, r: (l, r, 0))],
        out_specs=[pl.BlockSpec((None, n_tiles, PREP_BLOCK, TF_FFN), lambda l, r: (l, 0, r, 0))] * 2,
        out_shape=[half, half],
        compiler_params=_cparams(2),
        name="prep_w_up",
    )(w_up)
    down = pl.pallas_call(
        functools.partial(_prep_down_kernel, d_ff=d_ff),
        grid=(depth, d // PREP_BLOCK),
        in_specs=[pl.BlockSpec((None, d_ff, PREP_BLOCK), lambda l, c: (l, 0, c))],
        out_specs=pl.BlockSpec((None, d_ffp, PREP_BLOCK), lambda l, c: (l, 0, c)),
        out_shape=jax.ShapeDtypeStruct((depth, d_ffp, d), BF16),
        compiler_params=_cparams(2),
        name="prep_w_down",
    )(w_down)
    return tuple(up), down


def _rope_tables(seq):
    rows = seq // GRID_W
    row = jnp.repeat(jnp.arange(rows), GRID_W)
    col = jnp.tile(jnp.arange(GRID_W), rows)
    inv = ROPE_BASE ** (-jnp.arange(ROPE_FREQS, dtype=F32) / ROPE_FREQS)
    ang = jnp.stack([row, col], axis=-1).astype(F32)[:, :, None] * inv
    cos, sin = jnp.cos(ang), jnp.sin(ang)
    cos_t = jnp.concatenate([cos[:, 0], cos[:, 0], cos[:, 1], cos[:, 1]], axis=-1)
    sin_t = jnp.concatenate([-sin[:, 0], sin[:, 0], -sin[:, 1], sin[:, 1]], axis=-1)
    return cos_t, sin_t


def _pad_axis(a, axis, size):
    pad = [(0, 0)] * a.ndim
    pad[axis] = (0, size - a.shape[axis])
    return jnp.pad(a, pad)


def kernel(x_prompt, x_sample, cache_k, cache_v, c, c_ctx, w_mod, b_mod, g_pre_mix, g_post_mix,
           g_pre_ffn, g_post_ffn, w_in, pool_w, pool_scale, lambda_q1, lambda_k1, lambda_q2, lambda_k2,
           g_subln, conv_w, conv_b, w_br_pool, w_br_attn, w_br_conv, w_out, ffn_w_up, ffn_conv_w,
           ffn_conv_b, ffn_w_down):
    batch, seq, d = x_prompt.shape
    dec_batch, dec_seq, _ = x_sample.shape
    depth = w_in.shape[0]
    past = cache_k.shape[2]
    qk_w = N_HEADS * 2 * HEAD_DIM
    v_w = N_HEADS * V_DIM
    pool_cols = pool_scale.shape[1]
    conv_cols = conv_b.shape[1]
    d_ff = ffn_w_down.shape[1]
    d_ffp = -(-d_ff // TF_FFN) * TF_FFN
    assert qk_w == v_w and dec_batch + 1 <= MOD_ROWS

    col0 = {"pool": 3 * qk_w, "cb": 3 * qk_w + pool_cols, "ccx": 3 * qk_w + pool_cols + conv_cols,
            "g": 3 * qk_w + pool_cols + 3 * conv_cols}
    w_br = tuple(w.astype(BF16) for w in (w_br_pool, w_br_attn, w_br_conv))
    w_out_b = w_out.astype(BF16)
    pool_w_b = pool_w.astype(BF16)

    def padded_half(a, start, axis, dtype):
        part = lax.slice_in_dim(a, start, start + d_ff, axis=axis).astype(dtype)
        zeros = jnp.zeros(part.shape[:axis] + (d_ffp - d_ff,) + part.shape[axis + 1:], dtype)
        return jnp.concatenate([part, zeros], axis=axis)

    w_up, w_down = _prep_ffn_weights(ffn_w_up, ffn_w_down, d_ffp)
    fcp = jnp.concatenate([blk for start in (0, d_ff) for blk in (
        padded_half(ffn_conv_w, start, 2, F32), padded_half(ffn_conv_b[:, None, :], start, 2, F32))], axis=1)

    cvec = _pad_axis(jnp.concatenate([c_ctx[None, :], c], axis=0), 0, MOD_ROWS)
    mod = _modulation(cvec, w_mod, b_mod)
    mod3 = mod.reshape(depth * MOD_ROWS * N_MOD, 1, d)

    rope_tabs = _rope_tables(dec_seq)
    ck = cache_k.reshape(dec_batch, depth, past, qk_w)
    cv = cache_v.reshape(dec_batch, depth, past, v_w)
    row1 = lambda a: a[:, None, :]
    lam_args = (row1(lambda_q1), row1(lambda_k1), row1(lambda_q2), row1(lambda_k2), row1(g_subln))
    pool_scale3, conv_b3 = row1(pool_scale), row1(conv_b)

    xc = x_prompt.reshape(batch * seq, d)
    xl = x_sample.reshape(dec_batch * dec_seq, d)
    ctx_row = lambda l: (lambda r: l * MOD_ROWS)
    lat_row = lambda l: (lambda r: l * MOD_ROWS + 1 + r // dec_seq)
    gl = lambda a, l: a[l][None, :]
    hc = _prenorm(xc, mod3, gl(g_pre_mix, 0), ctx_row(0))
    hl = _prenorm(xl, mod3, gl(g_pre_mix, 0), lat_row(0))
    kv_bufs = (jnp.zeros((batch, depth, seq * (qk_w // LANES), LANES), F32),
               jnp.zeros((batch, depth, seq, v_w), F32))
    for l in range(depth):
        lam_init = 0.8 - 0.6 * math.exp(-0.3 * l)
        last = l == depth - 1
        mix = functools.partial(_branches, w_br=w_br, pool_w=pool_w_b, pool_scale=pool_scale3,
                                conv_w=conv_w, conv_b=conv_b3, layer=l, d=d, pool_w_cols=pool_cols,
                                conv_cols=conv_cols, col0=col0)
        ffn = functools.partial(_ffn, mod3=mod3, g_post=gl(g_post_ffn, l), w_up=w_up, conv_params=fcp,
                                w_down=w_down, layer=l)
        out_proj = functools.partial(_out_proj, w_out=w_out_b, mod3=mod3, g_post=gl(g_post_mix, l),
                                     g_pre_ffn=gl(g_pre_ffn, l), layer=l)
        next_norm = lambda row: None if last else (gl(g_pre_mix, l + 1), row(l + 1))

        z, *kv_bufs = _in_proj(hc, w_in, l, qk_w=qk_w, pool_cols=pool_cols, kv_bufs=kv_bufs)
        ya = _attention_ctx(z, lam_args, l, lam_init, seq=seq, width=v_w)
        merged = mix(z, ya, seq=seq)
        xc, h2 = out_proj(merged, x=xc, mod_row=ctx_row(l))
        res = ffn(h2, xc, mod_row=ctx_row(l), seq=seq, next_norm=next_norm(ctx_row))
        xc, hc = res if not last else (res[0], None)

        z = _in_proj(hl, w_in, l, qk_w=qk_w, pool_cols=pool_cols, rope_tabs=rope_tabs)[0]
        ya = _attention_lat(z, ck, cv, lam_args, l, lam_init, seq=dec_seq, width=v_w)
        merged = mix(z, ya, seq=dec_seq)
        xl, h2 = out_proj(merged, x=xl, mod_row=lat_row(l))
        res = ffn(h2, xl, mod_row=lat_row(l), seq=dec_seq, next_norm=next_norm(lat_row))
        xl, hl = res if not last else (res[0], None)

    return (xc.reshape(batch, seq, d), xl.reshape(dec_batch, dec_seq, d),
            kv_bufs[0].reshape(batch, depth, seq, N_HEADS, 2, HEAD_DIM),
            kv_bufs[1].reshape(batch, depth, seq, N_HEADS, V_DIM))
```

```python
import functools
import math

import jax
import jax.numpy as jnp
from jax import lax
from jax.experimental import pallas as pl
from jax.experimental.pallas import tpu as pltpu

F32 = jnp.float32
BF16 = jnp.bfloat16

GRID_W = 64
N_HEADS = 8
HEAD_DIM = 128
V_DIM = 2 * HEAD_DIM
POOL_WINDOWS = (2, 4, 8, 16)
ROPE_FREQS = HEAD_DIM // 4
ROPE_BASE = 10000.0
EPS = 1e-6
Q_SCALE = HEAD_DIM ** -0.5 * math.log2(math.e)
N_MOD = 6

LANES = 128
F32_SUBLANES = 8
BF16_SUBLANES = 16
VMEM_LIMIT_BYTES = 56 * 1024 * 1024
MOD_ROWS = F32_SUBLANES

TM_IN = 2048
TN_IN = 512
TM_MIX = 512
POOL_SUB = 256
TQ_LAT = 1024
TF_FFN = 512
KEY_CHUNK = 512
TN_MOD = 1024
ZERO_ROWS = 128
CONV_PARAM_ROWS = 4
PREP_BLOCK = 256


def _cparams(n_axes):
    return pltpu.CompilerParams(dimension_semantics=("arbitrary",) * n_axes,
                                vmem_limit_bytes=VMEM_LIMIT_BYTES)


def _rms(x):
    return x * lax.rsqrt(jnp.mean(x * x, axis=-1, keepdims=True) + EPS)


def _modulated_norm(x, gain_ref, sc_ref, sh_ref):
    return (_rms(x) * (gain_ref[...] * (1.0 + sc_ref[0])) + sh_ref[0]).astype(BF16)


def _mod_kernel(c_ref, w_ref, b_ref, o_ref):
    c = c_ref[...]
    s = (c * jax.nn.sigmoid(c)).astype(BF16)
    o_ref[...] = jnp.dot(s, w_ref[...].astype(BF16), preferred_element_type=F32) + b_ref[...]


def _modulation(cvec, w_mod, b_mod):
    depth, d, n = w_mod.shape
    return pl.pallas_call(
        _mod_kernel,
        grid=(depth, n // TN_MOD),
        in_specs=[
            pl.BlockSpec((MOD_ROWS, d), lambda l, j: (0, 0)),
            pl.BlockSpec((None, d, TN_MOD), lambda l, j: (l, 0, j)),
            pl.BlockSpec((None, 1, TN_MOD), lambda l, j: (l, 0, j)),
        ],
        out_specs=pl.BlockSpec((None, MOD_ROWS, TN_MOD), lambda l, j: (l, 0, j)),
        out_shape=jax.ShapeDtypeStruct((depth, MOD_ROWS, n), F32),
        compiler_params=_cparams(2),
        name="modulation",
    )(cvec, w_mod, b_mod.reshape(depth, 1, n))


def _rope_tile(acc, c, s):
    lane = lax.broadcasted_iota(jnp.int32, c.shape, 1)
    upper = (lane & ROPE_FREQS) != 0
    outs = []
    for g in range(acc.shape[1] // LANES):
        xg = acc[:, g * LANES:(g + 1) * LANES]
        partner = jnp.where(upper, pltpu.roll(xg, ROPE_FREQS, axis=1),
                            pltpu.roll(xg, LANES - ROPE_FREQS, axis=1))
        outs.append(xg * c + partner * s)
    return jnp.concatenate(outs, axis=1)


def _prenorm_kernel(x_ref, g_ref, sc_ref, sh_ref, h_ref):
    h_ref[...] = _modulated_norm(x_ref[...], g_ref, sc_ref, sh_ref)


def _prenorm(x, mod3, g_pre, mod_row):
    t, d = x.shape
    tm = TM_MIX
    return pl.pallas_call(
        _prenorm_kernel,
        grid=(t // tm,),
        in_specs=[
            pl.BlockSpec((tm, d), lambda i: (i, 0)),
            pl.BlockSpec((1, d), lambda i: (0, 0)),
            pl.BlockSpec((1, 1, d), lambda i: (mod_row(i * tm) * N_MOD + 1, 0, 0)),
            pl.BlockSpec((1, 1, d), lambda i: (mod_row(i * tm) * N_MOD + 0, 0, 0)),
        ],
        out_specs=pl.BlockSpec((tm, d), lambda i: (i, 0)),
        out_shape=jax.ShapeDtypeStruct((t, d), BF16),
        compiler_params=_cparams(1),
        name="prenorm",
    )(x, g_pre, mod3, mod3)


def _in_proj_kernel(*refs, rope, kv_out, kv_aliased, q_tiles, qk_tiles, k_tiles, v_tiles):
    h_ref, w_ref = refs[:2]
    pos = 2
    if rope:
        cos_ref, sin_ref = refs[pos:pos + 2]
        pos += 2
    pos += kv_aliased
    z_ref = refs[pos]
    pos += 1
    if kv_out:
        k_ref, v_ref = refs[pos:pos + 2]
    j = pl.program_id(1)

    acc = jnp.dot(h_ref[...], w_ref[...].astype(BF16), preferred_element_type=F32)

    if rope:
        z_ref[...] = acc.astype(BF16)

        @pl.when(j < q_tiles)
        def _():
            z_ref[...] = _rope_tile(acc * Q_SCALE, cos_ref[...], sin_ref[...]).astype(BF16)

        @pl.when((j >= q_tiles) & (j < qk_tiles))
        def _():
            z_ref[...] = _rope_tile(acc, cos_ref[...], sin_ref[...]).astype(BF16)
    else:
        z_ref[...] = (acc * jnp.where(j < q_tiles, Q_SCALE, 1.0)).astype(BF16)

    if kv_out:
        n_seq, k_rows, _ = k_ref.shape
        tile_groups = acc.shape[1] // LANES
        lane_groups = tile_groups * (k_tiles[1] - k_tiles[0])
        seq = k_rows // lane_groups
        for jj in range(k_tiles[1] - k_tiles[0]):
            @pl.when(j == k_tiles[0] + jj)
            def _(jj=jj):
                for b in range(n_seq):
                    for g in range(tile_groups):
                        k_ref[b, pl.ds(jj * tile_groups + g, seq, stride=lane_groups), :] = (
                            acc[b * seq:(b + 1) * seq, g * LANES:(g + 1) * LANES])

        @pl.when((j >= v_tiles[0]) & (j < v_tiles[1]))
        def _():
            for b in range(n_seq):
                v_ref[b] = acc[b * seq:(b + 1) * seq, :]


def _in_proj(h, w_in, layer, *, qk_w, pool_cols, rope_tabs=None, kv_bufs=None):
    t, d = h.shape
    n = w_in.shape[2]
    kv_out = kv_bufs is not None
    tm, tn = TM_IN, TN_IN
    nq = qk_w // tn
    n_pool = pool_cols // tn
    rope = rope_tabs is not None

    def w_tile(j):
        return jnp.where(j < 3 * nq, j + n_pool, jnp.where(j < 3 * nq + n_pool, j - 3 * nq, j))

    in_specs = [
        pl.BlockSpec((tm, d), lambda i, j: (i, 0)),
        pl.BlockSpec((None, d, tn), lambda i, j: (layer, 0, w_tile(j))),
    ]
    args = [h, w_in]
    if rope:
        tab_blocks = rope_tabs[0].shape[0] // tm
        tab_spec = pl.BlockSpec((tm, LANES), lambda i, j: (i % tab_blocks, 0))
        in_specs += [tab_spec, tab_spec]
        args += list(rope_tabs)
    out_specs = [pl.BlockSpec((tm, tn), lambda i, j: (i, j))]
    out_shape = [jax.ShapeDtypeStruct((t, n), BF16)]
    aliases = {}
    if kv_out:
        k_buf, v_buf = kv_bufs
        kv_seq = v_buf.shape[2]
        nb = tm // kv_seq
        out_specs += [
            pl.BlockSpec((nb, None) + k_buf.shape[2:], lambda i, j: (i, layer, 0, 0),
                         pipeline_mode=pl.Buffered(1)),
            pl.BlockSpec((nb, None, kv_seq, tn), lambda i, j: (i, layer, 0, jnp.clip(j - 2 * nq, 0, nq - 1))),
        ]
        out_shape += [jax.ShapeDtypeStruct(k_buf.shape, F32), jax.ShapeDtypeStruct(v_buf.shape, F32)]
        aliases = {len(args): 1, len(args) + 1: 2}
        in_specs += [pl.BlockSpec(memory_space=pl.ANY)] * 2
        args += [k_buf, v_buf]
    kern = functools.partial(_in_proj_kernel, rope=rope, kv_out=kv_out, kv_aliased=2 * kv_out,
                             q_tiles=nq, qk_tiles=2 * nq,
                             k_tiles=(nq, 2 * nq), v_tiles=(2 * nq, 3 * nq))
    return pl.pallas_call(
        kern,
        grid=(t // tm, n // tn),
        in_specs=in_specs,
        out_specs=out_specs,
        out_shape=out_shape,
        input_output_aliases=aliases,
        compiler_params=_cparams(2),
        name="in_proj_rope" if rope else "in_proj_kv",
    )(*args)


def _lambda(lq1, lk1, lq2, lk2, lam_init):
    return (jnp.exp(jnp.sum(lq1 * lk1, axis=-1, keepdims=True))
            - jnp.exp(jnp.sum(lq2 * lk2, axis=-1, keepdims=True)) + lam_init)


def _diff_attn_head(q, ks, vs, lam, gain, lam_init):
    def softmax_parts(m):
        cols = slice(m * HEAD_DIM, (m + 1) * HEAD_DIM)
        ss = [lax.dot_general(q[:, cols], k[:, cols], (((1,), (1,)), ((), ())),
                              preferred_element_type=F32) for k in ks]
        mx = functools.reduce(jnp.maximum, [jnp.max(s, axis=-1, keepdims=True) for s in ss])
        es = [jnp.exp2(s - mx) for s in ss]
        den = functools.reduce(jnp.add, [jnp.sum(e, axis=-1, keepdims=True) for e in es])
        return es, 1.0 / den

    e1, inv1 = softmax_parts(0)
    e2, inv2 = softmax_parts(1)
    w2 = lam * inv2
    o = functools.reduce(jnp.add, [
        jnp.dot((a * inv1 - b * w2).astype(BF16), v, preferred_element_type=F32)
        for a, b, v in zip(e1, e2, vs)])
    return _rms(o) * gain * (1.0 - lam_init)


def _attn_ctx_kernel(q_ref, k_ref, v_ref, lq1, lk1, lq2, lk2, g_ref, o_ref, *, lam_init):
    lam = _lambda(lq1[...], lk1[...], lq2[...], lk2[...], lam_init)
    gain = g_ref[...]
    for h in range(N_HEADS):
        cols = slice(h * V_DIM, (h + 1) * V_DIM)
        o = _diff_attn_head(q_ref[:, cols], [k_ref[:, cols]], [v_ref[:, cols]], lam, gain, lam_init)
        o_ref[:, cols] = o.astype(BF16)


def _attn_lat_kernel(q_ref, k_ref, v_ref, ck_ref, cv_ref, lq1, lk1, lq2, lk2, g_ref, o_ref, *,
                     lam_init):
    lam = _lambda(lq1[...], lk1[...], lq2[...], lk2[...], lam_init)
    q = q_ref[...]
    chunks = [(ck_ref[...].astype(BF16), cv_ref[...].astype(BF16))]
    for c in range(k_ref.shape[0] // KEY_CHUNK):
        rows = slice(c * KEY_CHUNK, (c + 1) * KEY_CHUNK)
        chunks.append((k_ref[rows, :], v_ref[rows, :]))
    outs = []
    for m in range(2):
        cols = slice(m * HEAD_DIM, (m + 1) * HEAD_DIM)
        mx = den = acc = None
        for kc, vc in chunks:
            s = lax.dot_general(q[:, cols], kc[:, cols], (((1,), (1,)), ((), ())),
                                preferred_element_type=F32)
            s_max = jnp.max(s, axis=-1, keepdims=True)
            if mx is None:
                mx = s_max
                e = jnp.exp2(s - mx)
                den = jnp.sum(e, axis=-1, keepdims=True)
                acc = jnp.dot(e.astype(BF16), vc, preferred_element_type=F32)
            else:
                mx_new = jnp.maximum(mx, s_max)
                alpha = jnp.exp2(mx - mx_new)
                e = jnp.exp2(s - mx_new)
                den = alpha * den + jnp.sum(e, axis=-1, keepdims=True)
                acc = alpha * acc + jnp.dot(e.astype(BF16), vc, preferred_element_type=F32)
                mx = mx_new
        outs.append(acc * (1.0 / den))
    o = outs[0] - lam * outs[1]
    o_ref[...] = (_rms(o) * g_ref[...] * (1.0 - lam_init)).astype(BF16)


def _lam_specs(layer, n_axes):
    idx = (lambda b: (layer, 0, 0)) if n_axes == 1 else (lambda b, h, qi: (layer, 0, 0))
    return [pl.BlockSpec((None, 1, HEAD_DIM), idx)] * 4 + [pl.BlockSpec((None, 1, V_DIM), idx)]


def _attention_ctx(z, lam_args, layer, lam_init, *, seq, width):
    t = z.shape[0]
    kern = functools.partial(_attn_ctx_kernel, lam_init=lam_init)
    return pl.pallas_call(
        kern,
        grid=(t // seq,),
        in_specs=[pl.BlockSpec((seq, width), lambda b: (b, 0)),
                  pl.BlockSpec((seq, width), lambda b: (b, 1)),
                  pl.BlockSpec((seq, width), lambda b: (b, 2))] + _lam_specs(layer, 1),
        out_specs=pl.BlockSpec((seq, width), lambda b: (b, 0)),
        out_shape=jax.ShapeDtypeStruct((t, width), BF16),
        compiler_params=_cparams(1),
        name="attention_ctx",
    )(z, z, z, *lam_args)


def _attention_lat(z, cache_k, cache_v, lam_args, layer, lam_init, *, seq, width):
    t = z.shape[0]
    nb = t // seq
    nq = seq // TQ_LAT
    past = cache_k.shape[2]
    kern = functools.partial(_attn_lat_kernel, lam_init=lam_init)
    return pl.pallas_call(
        kern,
        grid=(nb, N_HEADS, nq),
        in_specs=[
            pl.BlockSpec((TQ_LAT, V_DIM), lambda b, h, qi: (b * nq + qi, h)),
            pl.BlockSpec((seq, V_DIM), lambda b, h, qi: (b, N_HEADS + h)),
            pl.BlockSpec((seq, V_DIM), lambda b, h, qi: (b, 2 * N_HEADS + h)),
            pl.BlockSpec((None, None, past, V_DIM), lambda b, h, qi: (b, layer, 0, h)),
            pl.BlockSpec((None, None, past, V_DIM), lambda b, h, qi: (b, layer, 0, h)),
        ] + _lam_specs(layer, 3),
        out_specs=pl.BlockSpec((TQ_LAT, V_DIM), lambda b, h, qi: (b * nq + qi, h)),
        out_shape=jax.ShapeDtypeStruct((t, width), BF16),
        compiler_params=_cparams(3),
        name="attention_lat",
    )(z, z, z, cache_k, cache_v, *lam_args)


def _seq_masks(row0, rows, seq):
    p = (row0 + lax.broadcasted_iota(jnp.int32, (rows, 1), 0)) % seq
    return (p != 0).astype(F32), (p != seq - 1).astype(F32)


def _pool_mixer(u_ref, upad_ref, pw_ref, ps_ref, lhs_ref, row0, seq):
    halo = BF16_SUBLANES
    tm = u_ref.shape[0]
    pg = pw_ref.shape[-1]
    span = POOL_SUB + 2 * halo
    t_idx = lax.broadcasted_iota(jnp.int32, (POOL_SUB, span), 0)
    c_idx = lax.broadcasted_iota(jnp.int32, (POOL_SUB, span), 1)
    off = c_idx - halo - t_idx
    t_col = lax.broadcasted_iota(jnp.int32, (POOL_SUB, 1), 0)
    for r in range(tm // POOL_SUB):
        pos0 = (row0 + r * POOL_SUB) % seq
        pos_c = pos0 + c_idx - halo
        in_seq = (pos_c >= 0) & (pos_c < seq)
        pos_t = pos0 + t_col
        for gi, w in enumerate(POOL_WINDOWS):
            half = w // 2
            band = ((off >= -half) & (off < w - half) & in_seq).astype(F32).astype(BF16)
            cnt = jnp.minimum(pos_t + (w - half), seq) - jnp.maximum(pos_t - half, 0)
            cols = slice(gi * pg, (gi + 1) * pg)
            win = jnp.dot(band, upad_ref[r * POOL_SUB:r * POOL_SUB + span, cols],
                          preferred_element_type=F32)
            mean = win / cnt.astype(F32)
            dlt = mean - u_ref[r * POOL_SUB:(r + 1) * POOL_SUB, cols].astype(F32)
            y = jnp.dot(dlt.astype(BF16), pw_ref[gi], preferred_element_type=F32) * ps_ref[:, cols]
            lhs_ref[r * POOL_SUB:(r + 1) * POOL_SUB, cols] = y.astype(BF16)


def _branches_kernel(u_ref, up_ref, un_ref, cb_ref, ccx_ref, ccxp_ref, ccxn_ref, ya_ref, g_ref,
                     wp_ref, wa_ref, wc_ref, pw_ref, ps_ref, cw_ref, cbias_ref, o_ref,
                     acc_ref, lhs_ref, upad_ref, ppad_ref, *, seq):
    i = pl.program_id(0)
    s = pl.program_id(1)
    tm = u_ref.shape[0]
    halo = BF16_SUBLANES
    cwid = cb_ref.shape[1]
    row0 = i * tm

    def gated(lhs, w):
        gate = jax.nn.sigmoid(g_ref[...].astype(F32))
        return gate * jnp.dot(lhs, w, preferred_element_type=F32)

    @pl.when(s == 0)
    def _():
        upad_ref[0:halo, :] = up_ref[...]
        upad_ref[halo:halo + tm, :] = u_ref[...]
        upad_ref[halo + tm:, :] = un_ref[...]
        _pool_mixer(u_ref, upad_ref, pw_ref, ps_ref, lhs_ref, row0, seq)
        acc_ref[...] = gated(lhs_ref[...], wp_ref[...])

    @pl.when((s == 1) | (s == 2))
    def _():
        kc = ya_ref.shape[1]
        acc_ref[...] += gated(ya_ref[...], wa_ref[pl.ds(pl.multiple_of((s - 1) * kc, kc), kc), :])

    @pl.when(s == 3)
    def _():
        def prod(ref):
            v = ref[...].astype(F32)
            return v[:, :cwid] * v[:, cwid:]

        ppad_ref[0:halo, :] = prod(ccxp_ref)
        ppad_ref[halo:halo + tm, :] = prod(ccx_ref)
        ppad_ref[halo + tm:, :] = prod(ccxn_ref)
        m_prev, m_next = _seq_masks(row0, tm, seq)
        p_prev = ppad_ref[halo - 1:halo - 1 + tm, :] * m_prev
        p_next = ppad_ref[halo + 1:halo + 1 + tm, :] * m_next
        conv = (p_prev * cw_ref[0:1, :] + ppad_ref[halo:halo + tm, :] * cw_ref[1:2, :]
                + p_next * cw_ref[2:3, :] + cbias_ref[...])
        y = cb_ref[...].astype(F32) * conv
        o_ref[...] = (acc_ref[...] + gated(y.astype(BF16), wc_ref[...])).astype(BF16)


def _branches(z, y_attn, w_br, pool_w, pool_scale, conv_w, conv_b, layer, *, seq, d, pool_w_cols,
              conv_cols, col0):
    t = z.shape[0]
    tm = TM_MIX
    halo = BF16_SUBLANES
    hb = tm // halo
    last_hb = t // halo - 1
    kc = w_br[1].shape[1] // 2
    resident = lambda w: pl.BlockSpec((None,) + w.shape[1:], lambda i, s: (layer, 0, 0),
                                      pipeline_mode=pl.Buffered(1))
    prev = lambda i, s: jnp.maximum(i * hb - 1, 0)
    nxt = lambda i, s: jnp.minimum((i + 1) * hb, last_hb)
    pool_blk = col0["pool"] // pool_w_cols
    cb_blk = col0["cb"] // conv_cols
    ccx_blk = col0["ccx"] // (2 * conv_cols)
    g_blk = col0["g"] // d
    kern = functools.partial(_branches_kernel, seq=seq)
    return pl.pallas_call(
        kern,
        grid=(t // tm, 4),
        in_specs=[
            pl.BlockSpec((tm, pool_w_cols), lambda i, s: (i, pool_blk)),
            pl.BlockSpec((halo, pool_w_cols), lambda i, s: (prev(i, s), pool_blk)),
            pl.BlockSpec((halo, pool_w_cols), lambda i, s: (nxt(i, s), pool_blk)),
            pl.BlockSpec((tm, conv_cols), lambda i, s: (i, cb_blk)),
            pl.BlockSpec((tm, 2 * conv_cols), lambda i, s: (i, ccx_blk)),
            pl.BlockSpec((halo, 2 * conv_cols), lambda i, s: (prev(i, s), ccx_blk)),
            pl.BlockSpec((halo, 2 * conv_cols), lambda i, s: (nxt(i, s), ccx_blk)),
            pl.BlockSpec((tm, kc), lambda i, s: (i, jnp.clip(s - 1, 0, 1))),
            pl.BlockSpec((tm, d), lambda i, s: (i, g_blk + (s + 1) // 2)),
            resident(w_br[0]), resident(w_br[1]), resident(w_br[2]),
            pl.BlockSpec((None,) + pool_w.shape[1:], lambda i, s: (layer, 0, 0, 0)),
            pl.BlockSpec((None, 1, pool_w_cols), lambda i, s: (layer, 0, 0)),
            pl.BlockSpec((None, 3, conv_cols), lambda i, s: (layer, 0, 0)),
            pl.BlockSpec((None, 1, conv_cols), lambda i, s: (layer, 0, 0)),
        ],
        out_specs=pl.BlockSpec((tm, d), lambda i, s: (i, 0)),
        out_shape=jax.ShapeDtypeStruct((t, d), BF16),
        scratch_shapes=[
            pltpu.VMEM((tm, d), F32),
            pltpu.VMEM((tm, pool_w_cols), BF16),
            pltpu.VMEM((tm + 2 * halo, pool_w_cols), BF16),
            pltpu.VMEM((tm + 2 * halo, conv_cols), F32),
        ],
        compiler_params=_cparams(2),
        name="branches",
    )(z, z, z, z, z, z, z, y_attn, z, *w_br, pool_w, pool_scale, conv_w, conv_b)


def _out_proj_kernel(m_ref, w_ref, x_ref, gt_ref, g_ref, gn_ref, scn_ref, shn_ref, o_ref, h_ref):
    half = m_ref.shape[0] // 2
    for r in range(2):
        rows = slice(r * half, (r + 1) * half)
        m = jnp.dot(m_ref[rows, :], w_ref[...], preferred_element_type=F32)
        x1 = x_ref[rows, :] + _rms(m) * (gt_ref[0] * g_ref[...])
        o_ref[rows, :] = x1
        h_ref[rows, :] = _modulated_norm(x1, gn_ref, scn_ref, shn_ref)


def _out_proj(merged, w_out, x, mod3, g_post, g_pre_ffn, layer, mod_row):
    t, d = x.shape
    tm = TM_MIX
    mod_idx = lambda m: (lambda i: (mod_row(i * tm) * N_MOD + m, 0, 0))
    row = pl.BlockSpec((tm, d), lambda i: (i, 0))
    vec = pl.BlockSpec((1, d), lambda i: (0, 0))
    return pl.pallas_call(
        _out_proj_kernel,
        grid=(t // tm,),
        in_specs=[row, pl.BlockSpec((None, d, d), lambda i: (layer, 0, 0)), row,
                  pl.BlockSpec((1, 1, d), mod_idx(2)), vec, vec,
                  pl.BlockSpec((1, 1, d), mod_idx(4)), pl.BlockSpec((1, 1, d), mod_idx(3))],
        out_specs=[row, row],
        out_shape=[jax.ShapeDtypeStruct((t, d), F32), jax.ShapeDtypeStruct((t, d), BF16)],
        compiler_params=_cparams(1),
        name="out_proj",
    )(merged, w_out, x, mod3, g_post, g_pre_ffn, mod3, mod3)


def _gelu_tanh(x):
    c = math.sqrt(2.0 / math.pi)
    hx = 0.5 * x
    return hx * jnp.tanh(x * (c + (c * 0.044715) * (x * x))) + hx


def _ffn_layout(tm, seq):
    halo = BF16_SUBLANES
    pad = F32_SUBLANES
    if seq > tm:
        assert seq % tm == 0 and (tm + 2 * halo) % (2 * halo) == 0
        rows = tm + 2 * halo
        return (rows, rows, [(0, rows // 2, 0), (rows // 2, rows // 2, rows // 2)],
                [(halo, tm // 2), (halo + tm // 2, tm // 2)], [])
    assert tm % seq == 0
    nseg = tm // seq
    pieces = [(s * seq, seq, pad + s * (seq + pad)) for s in range(nseg)]
    outs = [(pad + s * (seq + pad), seq) for s in range(nseg)]
    zero_rows = [(s * (seq + pad), pad) for s in range(nseg + 1)]
    return tm, pad + nseg * (seq + pad), pieces, outs, zero_rows


def _ffn_kernel(*refs, seq, with_halo, emit_next):
    refs = list(refs)
    h_ref = refs.pop(0)
    hp_ref, hn_ref = (refs.pop(0), refs.pop(0)) if with_halo else (None, None)
    x_ref, gt_ref, gpost_ref = refs.pop(0), refs.pop(0), refs.pop(0)
    next_refs = (refs.pop(0), refs.pop(0), refs.pop(0)) if emit_next else None
    wg_ref, wv_ref, cp_ref, wd_ref = refs[:4]
    refs = refs[4:]
    o_ref = refs.pop(0)
    hnext_ref = refs.pop(0) if emit_next else None
    lhs_ref = refs.pop(0) if with_halo else h_ref
    acc_ref, u_ref = refs
    i = pl.program_id(0)
    f = pl.program_id(1)
    tm = x_ref.shape[0]
    halo = BF16_SUBLANES
    _, _, pieces, outs, zero_rows = _ffn_layout(tm, seq)

    @pl.when(f == 0)
    def _():
        if with_halo:
            has_prev = (i * tm) % seq != 0
            has_next = ((i + 1) * tm) % seq != 0
            lhs_ref[0:halo, :] = jnp.where(has_prev, hp_ref[...], jnp.zeros_like(hp_ref))
            lhs_ref[halo:halo + tm, :] = h_ref[...]
            lhs_ref[halo + tm:, :] = jnp.where(has_next, hn_ref[...], jnp.zeros_like(hn_ref))

        def zero_rows_of_acc(r, carry):
            acc_ref[pl.ds(pl.multiple_of(r * ZERO_ROWS, ZERO_ROWS), ZERO_ROWS), :] = jnp.zeros(
                (ZERO_ROWS, acc_ref.shape[1]), F32)
            return carry

        lax.fori_loop(0, tm // ZERO_ROWS, zero_rows_of_acc, 0)
        for start, rows in zero_rows:
            u_ref[:, start:start + rows, :] = jnp.zeros((2, rows, u_ref.shape[2]), F32)

    def conv_half(slot, w_ref):
        for lhs0, rows, u0 in pieces:
            u_ref[slot, u0:u0 + rows, :] = jnp.dot(lhs_ref[lhs0:lhs0 + rows, :], w_ref[...],
                                                   preferred_element_type=F32)
        p = slot * CONV_PARAM_ROWS
        return [
            u_ref[slot, u0 - 1:u0 - 1 + rows, :] * cp_ref[p:p + 1, :]
            + u_ref[slot, u0:u0 + rows, :] * cp_ref[p + 1:p + 2, :]
            + u_ref[slot, u0 + 1:u0 + 1 + rows, :] * cp_ref[p + 2:p + 3, :] + cp_ref[p + 3:p + 4, :]
            for u0, rows in outs]

    row = 0
    for gate, val in zip(conv_half(0, wg_ref), conv_half(1, wv_ref)):
        a = (_gelu_tanh(gate) * val).astype(BF16)
        acc_ref[row:row + a.shape[0], :] += jnp.dot(a, wd_ref[...], preferred_element_type=F32)
        row += a.shape[0]

    @pl.when(f == pl.num_programs(1) - 1)
    def _():
        x2 = x_ref[...] + _rms(acc_ref[...]) * (gt_ref[0] * gpost_ref[...])
        o_ref[...] = x2
        if emit_next:
            hnext_ref[...] = _modulated_norm(x2, *next_refs)


def _ffn(h, x, mod3, g_post, w_up, conv_params, w_down, layer, mod_row, *, seq, next_norm=None):
    t, d = x.shape
    tm, tf = TM_MIX, TF_FFN
    halo = BF16_SUBLANES
    hb = tm // halo
    last_hb = t // halo - 1
    nf = w_down.shape[1] // tf
    with_halo = seq > tm
    emit_next = next_norm is not None
    lhs_rows, u_rows, _, _, _ = _ffn_layout(tm, seq)
    row = pl.BlockSpec((tm, d), lambda i, f: (i, 0))
    vec = pl.BlockSpec((1, d), lambda i, f: (0, 0))

    def mod_spec(row_fn, m):
        return pl.BlockSpec((1, 1, d), lambda i, f: (row_fn(i * tm) * N_MOD + m, 0, 0))

    in_specs, args = [row], [h]
    if with_halo:
        in_specs += [pl.BlockSpec((halo, d), lambda i, f: (jnp.maximum(i * hb - 1, 0), 0)),
                     pl.BlockSpec((halo, d), lambda i, f: (jnp.minimum((i + 1) * hb, last_hb), 0))]
        args += [h, h]
    in_specs += [row, mod_spec(mod_row, 5), vec]
    args += [x, mod3, g_post]
    if emit_next:
        in_specs += [vec, mod_spec(next_norm[1], 1), mod_spec(next_norm[1], 0)]
        args += [next_norm[0], mod3, mod3]
    up_tile = pl.BlockSpec((None, None, d, tf), lambda i, f: (layer, f, 0, 0))
    in_specs += [up_tile, up_tile, pl.BlockSpec((None, 2 * CONV_PARAM_ROWS, tf), lambda i, f: (layer, 0, f)),
                 pl.BlockSpec((None, tf, d), lambda i, f: (layer, f, 0))]
    args += [*w_up, conv_params, w_down]
    scratch = [pltpu.VMEM((lhs_rows, d), BF16)] if with_halo else []
    scratch += [pltpu.VMEM((tm, d), F32), pltpu.VMEM((2, u_rows, tf), F32)]
    kern = functools.partial(_ffn_kernel, seq=seq, with_halo=with_halo, emit_next=emit_next)
    return pl.pallas_call(
        kern,
        grid=(t // tm, nf),
        in_specs=in_specs,
        out_specs=[row, row] if emit_next else [row],
        out_shape=[jax.ShapeDtypeStruct((t, d), F32)] + ([jax.ShapeDtypeStruct((t, d), BF16)] if emit_next else []),
        scratch_shapes=scratch,
        compiler_params=_cparams(2),
        name="ffn_halo" if with_halo else "ffn",
    )(*args)


def _prep_up_kernel(w_ref, g_ref, v_ref, *, d_ff):
    n_tiles, rows, tf = g_ref.shape
    for half, o_ref in enumerate((g_ref, v_ref)):
        for f in range(n_tiles):
            lo, hi = f * tf, min((f + 1) * tf, d_ff)
            o_ref[f, :, 0:hi - lo] = w_ref[:, half * d_ff + lo:half * d_ff + hi].astype(BF16)
            if hi - lo < tf:
                o_ref[f, :, hi - lo:] = jnp.zeros((rows, tf - (hi - lo)), BF16)


def _prep_down_kernel(w_ref, o_ref, *, d_ff):
    o_ref[0:d_ff, :] = w_ref[...].astype(BF16)
    o_ref[d_ff:, :] = jnp.zeros((o_ref.shape[0] - d_ff, o_ref.shape[1]), BF16)


def _prep_ffn_weights(w_up, w_down, d_ffp):
    depth, d, _ = w_up.shape
    d_ff = w_down.shape[1]
    n_tiles = d_ffp // TF_FFN
    half = jax.ShapeDtypeStruct((depth, n_tiles, d, TF_FFN), BF16)
    up = pl.pallas_call(
        functools.partial(_prep_up_kernel, d_ff=d_ff),
        grid=(depth, d // PREP_BLOCK),
        in_specs=[pl.BlockSpec((None, PREP_BLOCK, 2 * d_ff), lambda l, r: (l, r, 0))],
        out_specs=[pl.BlockSpec((None, n_tiles, PREP_BLOCK, TF_FFN), lambda l, r: (l, 0, r, 0))] * 2,
        out_shape=[half, half],
        compiler_params=_cparams(2),
        name="prep_w_up",
    )(w_up)
    down = pl.pallas_call(
        functools.partial(_prep_down_kernel, d_ff=d_ff),
        grid=(depth, d // PREP_BLOCK),
        in_specs=[pl.BlockSpec((None, d_ff, PREP_BLOCK), lambda l, c: (l, 0, c))],
        out_specs=pl.BlockSpec((None, d_ffp, PREP_BLOCK), lambda l, c: (l, 0, c)),
        out_shape=jax.ShapeDtypeStruct((depth, d_ffp, d), BF16),
        compiler_params=_cparams(2),
        name="prep_w_down",
    )(w_down)
    return tuple(up), down


def _rope_tables(seq):
    rows = seq // GRID_W
    row = jnp.repeat(jnp.arange(rows), GRID_W)
    col = jnp.tile(jnp.arange(GRID_W), rows)
    inv = ROPE_BASE ** (-jnp.arange(ROPE_FREQS, dtype=F32) / ROPE_FREQS)
    ang = jnp.stack([row, col], axis=-1).astype(F32)[:, :, None] * inv
    cos, sin = jnp.cos(ang), jnp.sin(ang)
    cos_t = jnp.concatenate([cos[:, 0], cos[:, 0], cos[:, 1], cos[:, 1]], axis=-1)
    sin_t = jnp.concatenate([-sin[:, 0], sin[:, 0], -sin[:, 1], sin[:, 1]], axis=-1)
    return cos_t, sin_t


def _pad_axis(a, axis, size):
    pad = [(0, 0)] * a.ndim
    pad[axis] = (0, size - a.shape[axis])
    return jnp.pad(a, pad)


def kernel(x_prompt, x_sample, cache_k, cache_v, c, c_ctx, w_mod, b_mod, g_pre_mix, g_post_mix,
           g_pre_ffn, g_post_ffn, w_in, pool_w, pool_scale, lambda_q1, lambda_k1, lambda_q2, lambda_k2,
           g_subln, conv_w, conv_b, w_br_pool, w_br_attn, w_br_conv, w_out, ffn_w_up, ffn_conv_w,
           ffn_conv_b, ffn_w_down):
    batch, seq, d = x_prompt.shape
    dec_batch, dec_seq, _ = x_sample.shape
    depth = w_in.shape[0]
    past = cache_k.shape[2]
    qk_w = N_HEADS * 2 * HEAD_DIM
    v_w = N_HEADS * V_DIM
    pool_cols = pool_scale.shape[1]
    conv_cols = conv_b.shape[1]
    d_ff = ffn_w_down.shape[1]
    d_ffp = -(-d_ff // TF_FFN) * TF_FFN
    assert qk_w == v_w and dec_batch + 1 <= MOD_ROWS

    col0 = {"pool": 3 * qk_w, "cb": 3 * qk_w + pool_cols, "ccx": 3 * qk_w + pool_cols + conv_cols,
            "g": 3 * qk_w + pool_cols + 3 * conv_cols}
    w_br = tuple(w.astype(BF16) for w in (w_br_pool, w_br_attn, w_br_conv))
    w_out_b = w_out.astype(BF16)
    pool_w_b = pool_w.astype(BF16)

    def padded_half(a, start, axis, dtype):
        part = lax.slice_in_dim(a, start, start + d_ff, axis=axis).astype(dtype)
        zeros = jnp.zeros(part.shape[:axis] + (d_ffp - d_ff,) + part.shape[axis + 1:], dtype)
        return jnp.concatenate([part, zeros], axis=axis)

    w_up, w_down = _prep_ffn_weights(ffn_w_up, ffn_w_down, d_ffp)
    fcp = jnp.concatenate([blk for start in (0, d_ff) for blk in (
        padded_half(ffn_conv_w, start, 2, F32), padded_half(ffn_conv_b[:, None, :], start, 2, F32))], axis=1)

    cvec = _pad_axis(jnp.concatenate([c_ctx[None, :], c], axis=0), 0, MOD_ROWS)
    mod = _modulation(cvec, w_mod, b_mod)
    mod3 = mod.reshape(depth * MOD_ROWS * N_MOD, 1, d)

    rope_tabs = _rope_tables(dec_seq)
    ck = cache_k.reshape(dec_batch, depth, past, qk_w)
    cv = cache_v.reshape(dec_batch, depth, past, v_w)
    row1 = lambda a: a[:, None, :]
    lam_args = (row1(lambda_q1), row1(lambda_k1), row1(lambda_q2), row1(lambda_k2), row1(g_subln))
    pool_scale3, conv_b3 = row1(pool_scale), row1(conv_b)

    xc = x_prompt.reshape(batch * seq, d)
    xl = x_sample.reshape(dec_batch * dec_seq, d)
    ctx_row = lambda l: (lambda r: l * MOD_ROWS)
    lat_row = lambda l: (lambda r: l * MOD_ROWS + 1 + r // dec_seq)
    gl = lambda a, l: a[l][None, :]
    hc = _prenorm(xc, mod3, gl(g_pre_mix, 0), ctx_row(0))
    hl = _prenorm(xl, mod3, gl(g_pre_mix, 0), lat_row(0))
    kv_bufs = (jnp.zeros((batch, depth, seq * (qk_w // LANES), LANES), F32),
               jnp.zeros((batch, depth, seq, v_w), F32))
    for l in range(depth):
        lam_init = 0.8 - 0.6 * math.exp(-0.3 * l)
        last = l == depth - 1
        mix = functools.partial(_branches, w_br=w_br, pool_w=pool_w_b, pool_scale=pool_scale3,
                                conv_w=conv_w, conv_b=conv_b3, layer=l, d=d, pool_w_cols=pool_cols,
                                conv_cols=conv_cols, col0=col0)
        ffn = functools.partial(_ffn, mod3=mod3, g_post=gl(g_post_ffn, l), w_up=w_up, conv_params=fcp,
                                w_down=w_down, layer=l)
        out_proj = functools.partial(_out_proj, w_out=w_out_b, mod3=mod3, g_post=gl(g_post_mix, l),
                                     g_pre_ffn=gl(g_pre_ffn, l), layer=l)
        next_norm = lambda row: None if last else (gl(g_pre_mix, l + 1), row(l + 1))

        z, *kv_bufs = _in_proj(hc, w_in, l, qk_w=qk_w, pool_cols=pool_cols, kv_bufs=kv_bufs)
        ya = _attention_ctx(z, lam_args, l, lam_init, seq=seq, width=v_w)
        merged = mix(z, ya, seq=seq)
        xc, h2 = out_proj(merged, x=xc, mod_row=ctx_row(l))
        res = ffn(h2, xc, mod_row=ctx_row(l), seq=seq, next_norm=next_norm(ctx_row))
        xc, hc = res if not last else (res[0], None)

        z = _in_proj(hl, w_in, l, qk_w=qk_w, pool_cols=pool_cols, rope_tabs=rope_tabs)[0]
        ya = _attention_lat(z, ck, cv, lam_args, l, lam_init, seq=dec_seq, width=v_w)
        merged = mix(z, ya, seq=dec_seq)
        xl, h2 = out_proj(merged, x=xl, mod_row=lat_row(l))
        res = ffn(h2, xl, mod_row=lat_row(l), seq=dec_seq, next_norm=next_norm(lat_row))
        xl, hl = res if not last else (res[0], None)

    return (xc.reshape(batch, seq, d), xl.reshape(dec_batch, dec_seq, d),
            kv_bufs[0].reshape(batch, depth, seq, N_HEADS, 2, HEAD_DIM),
            kv_bufs[1].reshape(batch, depth, seq, N_HEADS, V_DIM))
```
